```python
import math
import jax
import jax.numpy as jnp
from jax import lax
import numpy as np

D_MODEL = 1024
BATCH = 8
SEQ = 2048
DEPTH = 4
DEC_BATCH = 128
DEC_SEQ = 4
PAST_LEN = 2048
PAGE_SIZE = 128

HEAD_DIM = 64
A_GROUPS = 4
A_WIDTH = A_GROUPS * HEAD_DIM
A_CONV = 3
B_HEADS = 6
B_DK = HEAD_DIM
B_DV = HEAD_DIM
B_WIDTH = B_HEADS * B_DV
B_QKV = 2 * B_HEADS * B_DK + B_WIDTH
B_CONV = 4
GDN_CHUNK = 64
C_PAIRS = ((128, 1), (512, 4), (2048, 16))
C_HPG = 2
C_HEADS = len(C_PAIRS) * C_HPG
C_WIDTH = C_HEADS * HEAD_DIM
ATTN_BLOCK = 128
ROT_DIM = HEAD_DIM // 4
ROPE_THETA = 500000.0
MIX_WIDTH = A_WIDTH + B_WIDTH + C_WIDTH
EPS = 1e-6
NEG = -1e30
IN_SPLITS = (A_WIDTH, A_WIDTH, A_WIDTH, A_WIDTH,
             B_HEADS * B_DK, B_HEADS * B_DK, B_WIDTH, B_WIDTH, B_HEADS, B_HEADS,
             C_WIDTH, C_WIDTH, C_WIDTH, C_WIDTH)
IN_WIDTH = sum(IN_SPLITS)

kernel_name = 'hybrid_conv_deltanet_dilated_step'


def rmsnorm(x, w):
    xf = x.astype(jnp.float32)
    y = xf * lax.rsqrt(jnp.mean(xf * xf, axis=-1, keepdims=True) + EPS) * w.astype(jnp.float32)
    return y.astype(x.dtype)


def l2norm(x):
    return x * lax.rsqrt(jnp.sum(x * x, axis=-1, keepdims=True) + 1e-6)


def split_cols(u):
    out, start = [], 0
    for n in IN_SPLITS:
        out.append(u[..., start:start + n])
        start += n
    return out


def partial_rope(x, pos):
    half = ROT_DIM // 2
    inv_freq = ROPE_THETA ** (-jnp.arange(half, dtype=jnp.float32) * (2.0 / ROT_DIM))
    ang = pos.astype(jnp.float32)[:, None] * inv_freq[None, :]
    cos = jnp.cos(ang)[None, :, None, :]
    sin = jnp.sin(ang)[None, :, None, :]
    xf = x.astype(jnp.float32)
    x1, x2, rest = xf[..., :half], xf[..., half:ROT_DIM], xf[..., ROT_DIM:]
    return jnp.concatenate([x1 * cos - x2 * sin, x2 * cos + x1 * sin, rest], axis=-1).astype(x.dtype)


def causal_dwconv(x, buf, w):
    k_w = w.shape[0]
    t = x.shape[1]
    xp = jnp.concatenate([buf.astype(x.dtype), x], axis=1)
    y = xp[:, 0:t] * w[0]
    for j in range(1, k_w):
        y = y + xp[:, j:j + t] * w[j]
    return y, xp[:, -(k_w - 1):]


def gated_delta_rule(q, k, v, g, beta, s0):
    bsz, t, h, dk = k.shape
    dv = v.shape[-1]
    c = min(GDN_CHUNK, t)
    n = -(-t // c)
    pad = n * c - t

    def chunks(a):
        a = jnp.pad(a, [(0, 0), (0, pad)] + [(0, 0)] * (a.ndim - 2))
        a = a.reshape((bsz, n, c) + a.shape[2:])
        return jnp.moveaxis(a, 3, 1)

    qc, kc, vc, gc, bc = chunks(q), chunks(k), chunks(v), chunks(g), chunks(beta)
    gcum = jnp.cumsum(gc, axis=-1)
    tri = jnp.tril(jnp.ones((c, c), bool))
    strict = jnp.tril(jnp.ones((c, c), bool), -1)
    diff = gcum[..., :, None] - gcum[..., None, :]
    decay = jnp.where(tri, jnp.exp(jnp.where(tri, diff, 0.0)), 0.0)
    kb = kc * bc[..., None]
    lmat = jnp.where(strict, jnp.einsum('bhnid,bhnjd->bhnij', kb, kc) * decay, 0.0)
    eye = jnp.eye(c, dtype=jnp.float32)
    tinv = lax.linalg.triangular_solve(lmat + eye, jnp.broadcast_to(eye, lmat.shape),
                                       left_side=True, lower=True)
    u = tinv @ (vc * bc[..., None])
    w = tinv @ (kb * jnp.exp(gcum)[..., None])
    a_intra = jnp.einsum('bhnid,bhnjd->bhnij', qc, kc) * decay

    def step(s, xs):
        q_i, k_i, u_i, w_i, a_i, g_i = xs
        v_new = u_i - w_i @ s
        o = (q_i * jnp.exp(g_i)[..., None]) @ s + a_i @ v_new
        g_last = g_i[..., -1]
        s = s * jnp.exp(g_last)[..., None, None] + jnp.einsum(
            'bhcd,bhce->bhde', k_i * jnp.exp(g_last[..., None] - g_i)[..., None], v_new)
        return s, o

    xs = tuple(jnp.moveaxis(a, 2, 0) for a in (qc, kc, u, w, a_intra, gcum))
    s_fin, o = lax.scan(step, s0.astype(jnp.float32), xs)
    o = o.transpose(1, 0, 3, 2, 4).reshape(bsz, n * c, h, dv)[:, :t]
    return o, s_fin


def dilated_band_attention(q, k, v, dil, taps):
    bsz, t, h, d = q.shape
    ls = t // dil
    nb = -(-ls // ATTN_BLOCK)
    pad = nb * ATTN_BLOCK - ls

    def streams(a):
        a = a.reshape(bsz, ls, dil, h, d).transpose(0, 2, 1, 3, 4)
        a = jnp.pad(a, ((0, 0), (0, 0), (0, pad), (0, 0), (0, 0)))
        return a.reshape(bsz, dil, nb, ATTN_BLOCK, h, d).astype(jnp.float32)

    def with_prev(a):
        prev = jnp.pad(a[:, :, :-1], ((0, 0), (0, 0), (1, 0), (0, 0), (0, 0), (0, 0)))
        return jnp.concatenate([prev, a], axis=3)

    qs = streams(q)
    kk, vv = with_prev(streams(k)), with_prev(streams(v))
    s = jnp.einsum('brnqhd,brnkhd->brnqhk', qs, kk) * (HEAD_DIM ** -0.5)
    qi = jnp.arange(ATTN_BLOCK)[:, None]
    kj = jnp.arange(2 * ATTN_BLOCK)[None, :]
    dist = ATTN_BLOCK + qi - kj
    band = (dist >= 0) & (dist <= taps)
    blk = jnp.arange(nb)[:, None, None]
    valid = band[None] & ((blk > 0) | (kj >= ATTN_BLOCK)[None])
    s = jnp.where(valid[None, None, :, :, None, :], s, NEG)
    m = jnp.max(s, axis=-1, keepdims=True)
    p = jnp.exp(s - m)
    den = jnp.sum(p, axis=-1)
    o = jnp.einsum('brnqhk,brnkhd->brnqhd', p, vv) / den[..., None]
    lse = m[..., 0] + jnp.log(den)
    o = o.reshape(bsz, dil, nb * ATTN_BLOCK, h, d)[:, :, :ls].transpose(0, 2, 1, 3, 4).reshape(bsz, t, h, d)
    lse = lse.reshape(bsz, dil, nb * ATTN_BLOCK, h)[:, :, :ls].transpose(0, 2, 1, 3).reshape(bsz, t, h)
    return o, lse


def dilated_gather_attention(q, k, v, buf, dil, taps):
    bsz, td, h, d = q.shape
    lb = buf.shape[1]
    rel = jnp.arange(td)[:, None] - jnp.arange(taps + 1)[None, :] * dil
    in_new = rel >= 0
    idx_b = lb + rel
    valid = in_new | (idx_b >= 0)
    past = buf[:, jnp.clip(idx_b, 0, lb - 1)]
    ridx = jnp.clip(rel, 0, td - 1)
    sel = in_new[None, :, :, None, None]
    keys = jnp.where(sel, k[:, ridx], past[:, :, :, 0].astype(k.dtype)).astype(jnp.float32)
    vals = jnp.where(sel, v[:, ridx], past[:, :, :, 1].astype(v.dtype)).astype(jnp.float32)
    s = jnp.einsum('bqhd,bqkhd->bqhk', q.astype(jnp.float32), keys) * (HEAD_DIM ** -0.5)
    s = jnp.where(valid[None, :, None, :], s, NEG)
    m = jnp.max(s, axis=-1, keepdims=True)
    p = jnp.exp(s - m)
    den = jnp.sum(p, axis=-1)
    o = jnp.einsum('bqhk,bqkhd->bqhd', p, vals) / den[..., None]
    lse = m[..., 0] + jnp.log(den)
    return o, lse


def mixer_sublayer(hn, pos, buf_a, buf_b, s0, kv_bufs, w_in_l, w_out_l, conv_a_w_l, conv_b_w_l,
                   a_log_l, dt_bias_l, gdn_norm_w_l):
    bsz, t, _ = hn.shape
    u = hn @ w_in_l
    (a_x, a_cg, a_bg, a_z, b_q, b_k, b_v, b_z, b_a, b_b, c_q, c_k, c_v, c_z) = split_cols(u)

    a_conv, buf_a_new = causal_dwconv(a_cg * a_x, buf_a, conv_a_w_l)
    y_a = a_bg * a_conv * jax.nn.silu(a_z)

    qkv, buf_b_new = causal_dwconv(jnp.concatenate([b_q, b_k, b_v], axis=-1), buf_b, conv_b_w_l)
    qkv = jax.nn.silu(qkv).astype(jnp.float32)
    nqk = B_HEADS * B_DK
    gq = l2norm(qkv[..., :nqk].reshape(bsz, t, B_HEADS, B_DK)) * (B_DK ** -0.5)
    gk = l2norm(qkv[..., nqk:2 * nqk].reshape(bsz, t, B_HEADS, B_DK))
    gv = qkv[..., 2 * nqk:].reshape(bsz, t, B_HEADS, B_DV)
    g = -jnp.exp(a_log_l.astype(jnp.float32)) * jax.nn.softplus(
        b_a.astype(jnp.float32) + dt_bias_l.astype(jnp.float32))
    beta = jax.nn.sigmoid(b_b.astype(jnp.float32))
    o_b, s_new = gated_delta_rule(gq, gk, gv, g, beta, s0)
    y_b = rmsnorm(o_b, gdn_norm_w_l).reshape(bsz, t, B_WIDTH).astype(hn.dtype) * jax.nn.silu(b_z)

    cq = partial_rope(c_q.reshape(bsz, t, C_HEADS, HEAD_DIM), pos)
    ck = partial_rope(c_k.reshape(bsz, t, C_HEADS, HEAD_DIM), pos)
    cv = c_v.reshape(bsz, t, C_HEADS, HEAD_DIM)
    outs, lses, kv_new = [], [], []
    for gi, (win, dil) in enumerate(C_PAIRS):
        taps = win // dil
        qg = cq[:, :, gi * C_HPG:(gi + 1) * C_HPG]
        kg = ck[:, :, gi * C_HPG:(gi + 1) * C_HPG]
        vg = cv[:, :, gi * C_HPG:(gi + 1) * C_HPG]
        kv_rows = jnp.stack([kg, vg], axis=2)
        if kv_bufs is None:
            o, lse = dilated_band_attention(qg, kg, vg, dil, taps)
            kv_new.append(kv_rows[:, -min(win, t):])
        else:
            o, lse = dilated_gather_attention(qg, kg, vg, kv_bufs[gi], dil, taps)
            kv_new.append(kv_rows)
        outs.append(o)
        lses.append(lse)
    alpha = jax.nn.softmax(jnp.stack(lses, axis=2), axis=2)
    o_c = (jnp.stack(outs, axis=2) * alpha[..., None]).reshape(bsz, t, C_WIDTH).astype(hn.dtype)
    y_c = o_c * jax.nn.silu(c_z)

    mix = jnp.concatenate([y_a, y_b, y_c], axis=-1)
    return mix @ w_out_l, buf_a_new, buf_b_new, s_new, kv_new


def trunk(x, c, pos, conv_a_state, conv_b_state, gdn_state, kv_caches, w_in, w_out, w_ada, b_ada,
          norm_w, conv_a_w, conv_b_w, a_log, dt_bias, gdn_norm_w, final_norm_w, w_ada_final, b_ada_final):
    bsz = x.shape[0]
    new_a, new_b, new_g = [], [], []
    new_kv = [[] for _ in C_PAIRS]
    for l in range(DEPTH):
        if conv_a_state is None:
            buf_a = jnp.zeros((bsz, A_CONV - 1, A_WIDTH), x.dtype)
            buf_b = jnp.zeros((bsz, B_CONV - 1, B_QKV), x.dtype)
            s0 = jnp.zeros((bsz, B_HEADS, B_DK, B_DV), jnp.float32)
            kv_bufs = None
        else:
            buf_a = conv_a_state[l]
            buf_b = conv_b_state[l]
            s0 = gdn_state[l]
            kv_bufs = tuple(cc[l] for cc in kv_caches)
        shift, scale, gate = jnp.split(c @ w_ada[l] + b_ada[l], 3, axis=-1)
        hn = rmsnorm(x, norm_w[l]) * (1 + scale[:, None]) + shift[:, None]
        out, buf_a, buf_b, s_new, kv_rows = mixer_sublayer(
            hn, pos, buf_a, buf_b, s0, kv_bufs, w_in[l], w_out[l], conv_a_w[l], conv_b_w[l],
            a_log[l], dt_bias[l], gdn_norm_w[l])
        x = x + (1 + gate[:, None]) * out
        new_a.append(buf_a)
        new_b.append(buf_b)
        new_g.append(s_new)
        for gi in range(len(C_PAIRS)):
            new_kv[gi].append(kv_rows[gi])
    shift, scale = jnp.split(c @ w_ada_final + b_ada_final, 2, axis=-1)
    y = rmsnorm(x, final_norm_w) * (1 + scale[:, None]) + shift[:, None]
    return (y, jnp.stack(new_a), jnp.stack(new_b), jnp.stack(new_g),
            [jnp.stack(r) for r in new_kv])


def setup_inputs(seed: int = 0) -> dict:
    key = jax.random.key(seed)
    ks = jax.random.split(key, 24)
    f32 = jnp.float32

    def nrm(k, shape, s):
        return jax.random.normal(k, shape, f32) * s

    def kv(k, win):
        return nrm(k, (DEPTH, DEC_BATCH, min(win, PAST_LEN), 2, C_HPG, HEAD_DIM), 1.0)

    dt = jnp.exp(jax.random.uniform(ks[18], (DEPTH, B_HEADS), f32, math.log(1e-3), math.log(1e-1)))
    return {
        'x_prompt': nrm(ks[0], (BATCH, SEQ, D_MODEL), 1.0),
        'x_sample': nrm(ks[1], (DEC_BATCH, DEC_SEQ, D_MODEL), 1.0),
        'state_conv_a': nrm(ks[2], (DEPTH, DEC_BATCH, A_CONV - 1, A_WIDTH), 1.0),
        'state_conv_b': nrm(ks[3], (DEPTH, DEC_BATCH, B_CONV - 1, B_QKV), 1.0),
        'state_gdn': nrm(ks[4], (DEPTH, DEC_BATCH, B_HEADS, B_DK, B_DV), 0.1),
        'cache_kv_w128': kv(ks[5], C_PAIRS[0][0]),
        'cache_kv_w512': kv(ks[6], C_PAIRS[1][0]),
        'cache_kv_w2048': kv(ks[7], C_PAIRS[2][0]),
        'c_prompt': nrm(ks[8], (BATCH, D_MODEL), 1.0),
        'c_sample': nrm(ks[9], (DEC_BATCH, D_MODEL), 1.0),
        'w_in': nrm(ks[10], (DEPTH, D_MODEL, IN_WIDTH), D_MODEL ** -0.5),
        'w_out': nrm(ks[11], (DEPTH, MIX_WIDTH, D_MODEL), MIX_WIDTH ** -0.5),
        'w_ada': nrm(ks[12], (DEPTH, D_MODEL, 3 * D_MODEL), 0.1 * D_MODEL ** -0.5),
        'b_ada': nrm(ks[13], (DEPTH, 3 * D_MODEL), 0.01),
        'norm_w': 1.0 + nrm(ks[14], (DEPTH, D_MODEL), 0.02),
        'conv_a_w': nrm(ks[15], (DEPTH, A_CONV, A_WIDTH), A_CONV ** -0.5),
        'conv_b_w': nrm(ks[16], (DEPTH, B_CONV, B_QKV), B_CONV ** -0.5),
        'a_log': jnp.log(jax.random.uniform(ks[17], (DEPTH, B_HEADS), f32, 1.0, 16.0)),
        'dt_bias': dt + jnp.log(-jnp.expm1(-dt)),
        'gdn_norm_w': 1.0 + nrm(ks[19], (DEPTH, B_DV), 0.02),
        'final_norm_w': 1.0 + nrm(ks[20], (D_MODEL,), 0.02),
        'w_ada_final': nrm(ks[21], (D_MODEL, 2 * D_MODEL), 0.1 * D_MODEL ** -0.5),
        'b_ada_final': nrm(ks[22], (2 * D_MODEL,), 0.01),
    }


def reference(x_prompt, x_sample, state_conv_a, state_conv_b, state_gdn, cache_kv_w128, cache_kv_w512,
              cache_kv_w2048, c_prompt, c_sample, w_in, w_out, w_ada, b_ada, norm_w, conv_a_w, conv_b_w,
              a_log, dt_bias, gdn_norm_w, final_norm_w, w_ada_final, b_ada_final):
    pos_p = jnp.arange(x_prompt.shape[1])
    y_prompt, ca_p, cb_p, g_p, kv_p = trunk(
        x_prompt, c_prompt, pos_p, None, None, None, None, w_in, w_out, w_ada, b_ada, norm_w,
        conv_a_w, conv_b_w, a_log, dt_bias, gdn_norm_w, final_norm_w, w_ada_final, b_ada_final)
    pos_s = PAST_LEN + jnp.arange(x_sample.shape[1])
    y_sample, ca_s, cb_s, g_s, kv_s = trunk(
        x_sample, c_sample, pos_s, state_conv_a, state_conv_b, state_gdn,
        (cache_kv_w128, cache_kv_w512, cache_kv_w2048), w_in, w_out, w_ada, b_ada, norm_w,
        conv_a_w, conv_b_w, a_log, dt_bias, gdn_norm_w, final_norm_w, w_ada_final, b_ada_final)
    return (y_prompt, y_sample, ca_p, ca_s, cb_p, cb_s, g_p, g_s,
            kv_p[0], kv_s[0], kv_p[1], kv_s[1], kv_p[2], kv_s[2])
```

```python
import functools
import math

import jax
import jax.numpy as jnp
from jax import lax
from jax.experimental import pallas as pl
from jax.experimental.pallas import tpu as pltpu

F32 = jnp.float32
BF16 = jnp.bfloat16
HIGHEST = lax.Precision.HIGHEST

D_MODEL = 1024
HEAD = 64
A_WIDTH = 256
A_CONV = 3
B_HEADS = 6
B_WIDTH = B_HEADS * HEAD
B_QKV = 3 * B_WIDTH
B_CONV = 4
GDN_CHUNK = 64
GROUPS = ((128, 1), (512, 4), (2048, 16))
HPG = 2
GROUP_W = HPG * HEAD
C_WIDTH = len(GROUPS) * GROUP_W
ATTN_BLOCK = 128
ROT_DIM = HEAD // 4
ROPE_THETA = 500000.0
PAST_LEN = 2048
EPS = 1e-6
NEG = -1e30
LANE = 128
SUBLANE = 8

COL_A = (0, 1024)
COL_BQKV = (1024, 2176)
COL_BZ = (2176, 2560)
COL_CQ = (2560, 2944)
COL_CK = (2944, 3328)
COL_CV = (3328, 3712)
COL_CZ = (3712, 4096)
COL_AB = (4096, 4224)
IN_PERM_WIDTH = 4224
SRC_AB = (2560, 2572)

VMEM_LIMIT_BYTES = 56 * 1024 * 1024


def _params(n_axes):
    return pltpu.CompilerParams(dimension_semantics=("arbitrary",) * n_axes,
                                vmem_limit_bytes=VMEM_LIMIT_BYTES)


def _round_up(x, m):
    return -(-x // m) * m


def _silu(x):
    return x * jax.nn.sigmoid(x)


def _dot_nt(a, b, **kw):
    return lax.dot_general(a, b, (((1,), (1,)), ((), ())), preferred_element_type=F32, **kw)


def _dot_tn(a, b, **kw):
    return lax.dot_general(a, b, (((0,), (0,)), ((), ())), preferred_element_type=F32, **kw)


def _ada_kernel(c_ref, w_ref, b_ref, o_ref):
    c = c_ref[...].astype(BF16)
    w = w_ref[0].astype(BF16)
    o_ref[0] = jnp.dot(c, w, preferred_element_type=F32) + b_ref[0]


def _ada(c_all, w, b):
    n_layers, _, n = w.shape
    r = c_all.shape[0]
    tn = 1024
    return pl.pallas_call(
        _ada_kernel,
        grid=(n_layers, n // tn),
        in_specs=[pl.BlockSpec((r, D_MODEL), lambda l, j: (0, 0)),
                  pl.BlockSpec((1, D_MODEL, tn), lambda l, j: (l, 0, j)),
                  pl.BlockSpec((1, 1, tn), lambda l, j: (l, 0, j))],
        out_specs=pl.BlockSpec((1, r, tn), lambda l, j: (l, 0, j)),
        out_shape=jax.ShapeDtypeStruct((n_layers, r, n), F32),
        compiler_params=_params(2),
        name="ada",
    )(c_all, w, b.reshape(n_layers, 1, n))


def _inproj_kernel(x_ref, sh_ref, sc_ref, nw_ref, w_ref, cos_ref, sa_ref, sb_ref,
                   ua_ref, bqkv_ref, bz_ref, bab_ref, qrot_ref, kv0_ref, kv1_ref, kv2_ref, cz_ref):
    x = x_ref[...]
    var = jnp.mean(x * x, axis=-1, keepdims=True)
    hn = x * lax.rsqrt(var + EPS) * nw_ref[...]
    hn = hn * (1.0 + sc_ref[0]) + sh_ref[0]
    hb = hn.astype(BF16)

    def mm(cols):
        return jnp.dot(hb, w_ref[:, cols[0]:cols[1]], preferred_element_type=F32)

    ua_ref[...] = mm(COL_A)
    bqkv_ref[...] = mm(COL_BQKV)
    bz_ref[...] = mm(COL_BZ)
    bab_ref[...] = mm(COL_AB)
    cz_ref[...] = mm(COL_CZ)
    cq = mm(COL_CQ)
    ck = mm(COL_CK)
    cv = mm(COL_CV)
    cos = cos_ref[...]
    sa = sa_ref[...]
    sb = sb_ref[...]

    def rope(t):
        return t * cos + pltpu.roll(t, LANE - ROT_DIM // 2, 1) * sa + pltpu.roll(t, ROT_DIM // 2, 1) * sb

    for g, kv_ref in enumerate((kv0_ref, kv1_ref, kv2_ref)):
        lo, hi = g * GROUP_W, (g + 1) * GROUP_W
        qrot_ref[:, lo:hi] = rope(cq[:, lo:hi])
        kv_ref[:, :GROUP_W] = rope(ck[:, lo:hi])
        kv_ref[:, GROUP_W:] = cv[:, lo:hi]


def _inproj(x_rows, mod, norm_w, w_perm, rope_tabs, tm):
    m = x_rows.shape[0]
    nblk = m // tm
    nb_mod, r, _ = mod.shape
    tiles_per_mod = nblk // nb_mod
    nt_tab = rope_tabs[0].shape[0] // tm
    row = lambda w: pl.BlockSpec((tm, w), lambda i: (i, 0))
    tab = pl.BlockSpec((tm, LANE), lambda i: (i % nt_tab, 0))
    widths = (1024, B_QKV, B_WIDTH, LANE, C_WIDTH, 2 * GROUP_W, 2 * GROUP_W, 2 * GROUP_W, C_WIDTH)
    return pl.pallas_call(
        _inproj_kernel,
        grid=(nblk,),
        in_specs=[row(D_MODEL),
                  pl.BlockSpec((1, r, D_MODEL), lambda i: (i // tiles_per_mod, 0, 0)),
                  pl.BlockSpec((1, r, D_MODEL), lambda i: (i // tiles_per_mod, 0, 1)),
                  pl.BlockSpec((1, D_MODEL), lambda i: (0, 0)),
                  pl.BlockSpec((D_MODEL, IN_PERM_WIDTH), lambda i: (0, 0)),
                  tab, tab, tab],
        out_specs=[row(w) for w in widths],
        out_shape=[jax.ShapeDtypeStruct((m, w), F32) for w in widths],
        compiler_params=_params(1),
        name="inproj",
    )(x_rows, mod, mod, norm_w.reshape(1, D_MODEL), w_perm, *rope_tabs)


def _conv_kernel(ua_ref, bqkv_ref, bab_ref, bufa_ref, bufb_ref, wa_ref, wb_ref, alog_ref, dtb_ref, seg_ref,
                 ya_ref, q_ref, k_ref, v_ref, g_ref, beta_ref, sta_ref, stb_ref,
                 xpa_ref, xpb_ref, *, stride, tt):
    t = pl.program_id(1)

    def causal_conv(xp_ref, buf_ref, st_ref, x_new, w_ref, taps):
        halo = (taps - 1) * stride
        x0 = _round_up(halo, SUBLANE)

        @pl.when(t == 0)
        def _():
            xp_ref[x0 - halo:x0, :] = buf_ref[0]

        @pl.when(t > 0)
        def _():
            xp_ref[x0 - halo:x0, :] = xp_ref[x0 + tt - halo:x0 + tt, :]

        xp_ref[x0:x0 + tt, :] = x_new
        st_ref[0] = xp_ref[x0 + tt - halo:x0 + tt, :]
        y = xp_ref[x0 - halo:x0 - halo + tt, :] * w_ref[0:1, :]
        for j in range(1, taps):
            lo = x0 - halo + j * stride
            y = y + xp_ref[lo:lo + tt, :] * w_ref[j:j + 1, :]
        return y

    ua = ua_ref[...]
    ax, acg, abg, az = (ua[:, i * A_WIDTH:(i + 1) * A_WIDTH] for i in range(4))
    conv_a = causal_conv(xpa_ref, bufa_ref, sta_ref, acg * ax, wa_ref, A_CONV)
    ya_ref[...] = abg * conv_a * _silu(az)

    qkv = _silu(causal_conv(xpb_ref, bufb_ref, stb_ref, bqkv_ref[...], wb_ref, B_CONV))
    seg = seg_ref[...]

    def l2norm(x):
        ss = jnp.dot(x * x, seg, preferred_element_type=F32, precision=HIGHEST)
        return x * lax.rsqrt(ss + 1e-6)

    q_ref[...] = l2norm(qkv[:, :B_WIDTH]) * (HEAD ** -0.5)
    k_ref[...] = l2norm(qkv[:, B_WIDTH:2 * B_WIDTH])
    v_ref[...] = qkv[:, 2 * B_WIDTH:]

    ab = bab_ref[...]
    z = ab + dtb_ref[...]
    softplus = jnp.maximum(z, 0.0) + jnp.log1p(jnp.exp(-jnp.abs(z)))
    g_ref[...] = -jnp.exp(alog_ref[...]) * softplus
    beta_ref[...] = jax.nn.sigmoid(pltpu.roll(ab, LANE - B_HEADS, 1))


def _conv(ua, bqkv, bab, buf_a, buf_b, conv_a_w, conv_b_w, a_log, dt_bias, seg, *, groups, stride, tt):
    m = ua.shape[0]
    nt = m // (groups * tt)
    halo_a, halo_b = (A_CONV - 1) * stride, (B_CONV - 1) * stride
    row = lambda w: pl.BlockSpec((tt, w), lambda g, t: (g * nt + t, 0))
    full = lambda a: pl.BlockSpec(a.shape, lambda g, t: (0,) * a.ndim)
    per_group = lambda h, w: pl.BlockSpec((1, h, w), lambda g, t: (g, 0, 0))
    alog = jnp.zeros((1, LANE), F32).at[0, :B_HEADS].set(a_log)
    dtb = jnp.zeros((1, LANE), F32).at[0, :B_HEADS].set(dt_bias)
    out_w = (A_WIDTH, B_WIDTH, B_WIDTH, B_WIDTH, LANE, LANE)
    return pl.pallas_call(
        functools.partial(_conv_kernel, stride=stride, tt=tt),
        grid=(groups, nt),
        in_specs=[row(1024), row(B_QKV), row(LANE), per_group(halo_a, A_WIDTH), per_group(halo_b, B_QKV),
                  full(conv_a_w), full(conv_b_w), full(alog), full(dtb), full(seg)],
        out_specs=[row(w) for w in out_w] + [per_group(halo_a, A_WIDTH), per_group(halo_b, B_QKV)],
        out_shape=[jax.ShapeDtypeStruct((m, w), F32) for w in out_w]
        + [jax.ShapeDtypeStruct((groups, halo_a, A_WIDTH), F32),
           jax.ShapeDtypeStruct((groups, halo_b, B_QKV), F32)],
        scratch_shapes=[pltpu.VMEM((_round_up(halo_a, SUBLANE) + tt, A_WIDTH), F32),
                        pltpu.VMEM((_round_up(halo_b, SUBLANE) + tt, B_QKV), F32)],
        compiler_params=_params(2),
        name="conv",
    )(ua, bqkv, bab, buf_a, buf_b, conv_a_w, conv_b_w, alog, dtb, seg)


def _gdn_prompt_kernel(q_ref, k_ref, v_ref, g_ref, beta_ref, z_ref, nw_ref, tri_ref,
                       y_ref, sfin_ref, s_ref):
    c = pl.program_id(1)
    n = GDN_CHUNK

    @pl.when(c == 0)
    def _():
        s_ref[...] = jnp.zeros_like(s_ref)

    q = q_ref[...]
    k = k_ref[...]
    v = v_ref[...]
    beta = beta_ref[...]
    gc = jnp.dot(tri_ref[...], g_ref[...], preferred_element_type=F32, precision=HIGHEST)
    gct = gc.T
    ri = lax.broadcasted_iota(jnp.int32, (n, n), 0)
    ci = lax.broadcasted_iota(jnp.int32, (n, n), 1)
    tri = ri >= ci
    strict = ri > ci
    eye = jnp.where(ri == ci, 1.0, 0.0).astype(F32)
    nw = nw_ref[...]
    outs = []
    for h in range(B_HEADS):
        lo, hi = h * HEAD, (h + 1) * HEAD
        qh, kh, vh = q[:, lo:hi], k[:, lo:hi], v[:, lo:hi]
        gcol = gc[:, h:h + 1]
        grow = gct[h:h + 1, :]
        bcol = beta[:, h:h + 1]
        decay = jnp.where(tri, jnp.exp(jnp.where(tri, gcol - grow, 0.0)), 0.0)
        kb = kh * bcol
        lmat = jnp.where(strict, _dot_nt(kb, kh) * decay, 0.0)
        x = eye - lmat
        p = lmat
        for _ in range(5):
            p = jnp.dot(p, p, preferred_element_type=F32)
            x = x + jnp.dot(x, p, preferred_element_type=F32)
        egc = jnp.exp(gcol)
        u = jnp.dot(x, vh * bcol, preferred_element_type=F32)
        w = jnp.dot(x, kb * egc, preferred_element_type=F32)
        a_intra = _dot_nt(qh, kh) * decay
        s = s_ref[h]
        v_new = u - jnp.dot(w, s, preferred_element_type=F32)
        o = jnp.dot(qh * egc, s, preferred_element_type=F32) + jnp.dot(a_intra, v_new, preferred_element_type=F32)
        g_last = gcol[n - 1:n, :]
        s_new = s * jnp.exp(g_last) + _dot_tn(kh * jnp.exp(g_last - gcol), v_new)
        s_ref[h] = s_new
        sfin_ref[0, h] = s_new
        ms = jnp.mean(o * o, axis=-1, keepdims=True)
        outs.append(o * lax.rsqrt(ms + EPS) * nw)
    y_ref[...] = jnp.concatenate(outs, axis=1) * _silu(z_ref[...])


def _gdn_prompt(q, k, v, g, beta, z, norm_w, batch):
    m = q.shape[0]
    nc = m // (batch * GDN_CHUNK)
    n = GDN_CHUNK
    row = lambda w: pl.BlockSpec((n, w), lambda b, c: (b * nc + c, 0))
    tri = jnp.tril(jnp.ones((n, n), F32))
    return pl.pallas_call(
        _gdn_prompt_kernel,
        grid=(batch, nc),
        in_specs=[row(B_WIDTH), row(B_WIDTH), row(B_WIDTH), row(LANE), row(LANE), row(B_WIDTH),
                  pl.BlockSpec((1, HEAD), lambda b, c: (0, 0)),
                  pl.BlockSpec((n, n), lambda b, c: (0, 0))],
        out_specs=[row(B_WIDTH), pl.BlockSpec((1, B_HEADS, HEAD, HEAD), lambda b, c: (b, 0, 0, 0))],
        out_shape=[jax.ShapeDtypeStruct((m, B_WIDTH), F32),
                   jax.ShapeDtypeStruct((batch, B_HEADS, HEAD, HEAD), F32)],
        scratch_shapes=[pltpu.VMEM((B_HEADS, HEAD, HEAD), F32)],
        compiler_params=_params(2),
        name="gdn_prompt",
    )(q, k, v, g, beta, z, norm_w.reshape(1, HEAD), tri)


def _gdn_sample_kernel(q_ref, k_ref, v_ref, g_ref, beta_ref, z_ref, nw_ref, s0_ref, y_ref, s_ref, *, steps):
    h = pl.program_id(0)
    bsz = s_ref.shape[-1]
    zero = jnp.zeros((HEAD, bsz), F32)

    def bcast_row(ref, i, dk):
        return jnp.broadcast_to(ref[i, pl.ds(dk, 1), :], (HEAD, bsz))

    def head_row(ref, i):
        return ref[i, pl.ds(h, 1), :]

    def finish(i, o):
        ms = jnp.mean(o * o, axis=0, keepdims=True)
        y_ref[i] = o * lax.rsqrt(ms + EPS) * nw_ref[...] * _silu(z_ref[i])

    gam = jnp.exp(head_row(g_ref, 0))

    def first(dk, ks):
        s = s0_ref[0, dk] * gam
        s_ref[0, dk] = s
        return ks + bcast_row(k_ref, 0, dk) * s

    ks = lax.fori_loop(0, HEAD, first, zero)
    vn = head_row(beta_ref, 0) * (v_ref[0] - ks)

    for i in range(1, steps):
        gam = jnp.exp(head_row(g_ref, i))

        def mid(dk, carry, i=i, vn=vn, gam=gam):
            o, ks = carry
            s = s_ref[0, dk] + bcast_row(k_ref, i - 1, dk) * vn
            o = o + bcast_row(q_ref, i - 1, dk) * s
            s = s * gam
            s_ref[0, dk] = s
            return o, ks + bcast_row(k_ref, i, dk) * s

        o, ks = lax.fori_loop(0, HEAD, mid, (zero, zero))
        finish(i - 1, o)
        vn = head_row(beta_ref, i) * (v_ref[i] - ks)

    def last(dk, o):
        s = s_ref[0, dk] + bcast_row(k_ref, steps - 1, dk) * vn
        s_ref[0, dk] = s
        return o + bcast_row(q_ref, steps - 1, dk) * s

    finish(steps - 1, lax.fori_loop(0, HEAD, last, zero))


def _gdn_sample(qt, kt, vt, gt, bt, zt, nwb, s0t):
    steps, _, bsz = qt.shape
    per_head = pl.BlockSpec((steps, HEAD, bsz), lambda h: (0, h, 0))
    small = pl.BlockSpec((steps, SUBLANE, bsz), lambda h: (0, 0, 0))
    state = pl.BlockSpec((1, HEAD, HEAD, bsz), lambda h: (h, 0, 0, 0))
    return pl.pallas_call(
        functools.partial(_gdn_sample_kernel, steps=steps),
        grid=(B_HEADS,),
        in_specs=[per_head, per_head, per_head, small, small, per_head,
                  pl.BlockSpec((HEAD, bsz), lambda h: (0, 0)), state],
        out_specs=[per_head, state],
        out_shape=[jax.ShapeDtypeStruct((steps, B_WIDTH, bsz), F32),
                   jax.ShapeDtypeStruct((B_HEADS, HEAD, HEAD, bsz), F32)],
        compiler_params=_params(1),
        name="gdn_sample",
    )(qt, kt, vt, gt, bt, zt, nwb, s0t)


def _attn_prompt_kernel(q_ref, kc_ref, vc_ref, kp_ref, vp_ref, o_ref, lse_ref):
    i = pl.program_id(2)
    n = ATTN_BLOCK
    q = q_ref[0]
    kc, vc, kp, vp = kc_ref[0], vc_ref[0], kp_ref[0], vp_ref[0]
    lane = lax.broadcasted_iota(jnp.int32, (n, GROUP_W), 1)
    head0 = lane < HEAD
    qi = lax.broadcasted_iota(jnp.int32, (n, n), 0)
    kj = lax.broadcasted_iota(jnp.int32, (n, n), 1)
    mask_c = kj <= qi
    mask_p = jnp.logical_and(kj >= qi, i > 0)
    res = []
    for sel in (head0, jnp.logical_not(head0)):
        qh = jnp.where(sel, q, 0.0)
        sc = jnp.where(mask_c, _dot_nt(qh, kc) * (HEAD ** -0.5), NEG)
        sp = jnp.where(mask_p, _dot_nt(qh, kp) * (HEAD ** -0.5), NEG)
        mx = jnp.maximum(jnp.max(sc, axis=-1, keepdims=True), jnp.max(sp, axis=-1, keepdims=True))
        pc = jnp.exp(sc - mx)
        pp = jnp.exp(sp - mx)
        den = jnp.sum(pc, axis=-1, keepdims=True) + jnp.sum(pp, axis=-1, keepdims=True)
        acc = jnp.dot(pc, vc, preferred_element_type=F32) + jnp.dot(pp, vp, preferred_element_type=F32)
        res.append((acc / den, mx + jnp.log(den)))
    o_ref[0] = jnp.where(head0, res[0][0], res[1][0])
    lse_ref[0] = jnp.where(head0, jnp.broadcast_to(res[0][1], (n, GROUP_W)),
                           jnp.broadcast_to(res[1][1], (n, GROUP_W)))


def _attn_prompt(qrot, kv, group, batch):
    _, dil = GROUPS[group]
    m = qrot.shape[0]
    t = m // batch
    ls = t // dil
    nb = ls // ATTN_BLOCK
    n = ATTN_BLOCK
    qv = qrot.reshape(batch, ls, dil * C_WIDTH)
    kvv = kv.reshape(batch, ls, dil * 2 * GROUP_W)
    blk = lambda f: pl.BlockSpec((1, n, GROUP_W), f)
    out = jax.ShapeDtypeStruct((batch, ls, dil * GROUP_W), F32)
    o, lse = pl.pallas_call(
        _attn_prompt_kernel,
        grid=(batch, dil, nb),
        in_specs=[blk(lambda b, r, i: (b, i, r * len(GROUPS) + group)),
                  blk(lambda b, r, i: (b, i, 2 * r)),
                  blk(lambda b, r, i: (b, i, 2 * r + 1)),
                  blk(lambda b, r, i: (b, jnp.maximum(i - 1, 0), 2 * r)),
                  blk(lambda b, r, i: (b, jnp.maximum(i - 1, 0), 2 * r + 1))],
        out_specs=[blk(lambda b, r, i: (b, i, r)), blk(lambda b, r, i: (b, i, r))],
        out_shape=[out, out],
        compiler_params=_params(3),
        name=f"attn_prompt_g{group}",
    )(qv, kvv, kvv, kvv, kvv)
    return o.reshape(m, GROUP_W), lse.reshape(m, GROUP_W)


def _attn_sample_kernel(q_ref, n0_ref, n1_ref, n2_ref, c0_ref, c1_ref, c2_ref, o_ref, lse_ref, *, bt, steps):
    rows = 2 * steps
    n = ATTN_BLOCK
    scale = HEAD ** -0.5
    lane = lax.broadcasted_iota(jnp.int32, (rows, GROUP_W), 1)
    rix = lax.broadcasted_iota(jnp.int32, (rows, GROUP_W), 0)
    own_head = (lane < HEAD) == (rix < steps)
    key = lax.broadcasted_iota(jnp.int32, (rows, n), 1)
    qry = lax.broadcasted_iota(jnp.int32, (rows, n), 0) % steps
    nkey = lax.broadcasted_iota(jnp.int32, (rows, rows), 1)
    nqry = lax.broadcasted_iota(jnp.int32, (rows, rows), 0) % steps
    new_refs = (n0_ref, n1_ref, n2_ref)
    cache_refs = (c0_ref, c1_ref, c2_ref)

    def one(b, carry):
        q_all = q_ref[b]
        for g, (_, dil) in enumerate(GROUPS):
            qm = jnp.where(own_head, q_all[:, g * GROUP_W:(g + 1) * GROUP_W], 0.0)
            new = new_refs[g][b]
            s_new = _dot_nt(qm, new[:, :GROUP_W]) * scale
            if dil == 1:
                kbuf = cache_refs[g][0, b, :, 0:GROUP_W]
                s_buf = jnp.where(key >= qry, _dot_nt(qm, kbuf) * scale, NEG)
                new_ok = jnp.logical_and(nkey < steps, nkey <= nqry)
            else:
                s_buf = jnp.zeros((rows, n), F32)
                for i in range(steps):
                    kbuf = cache_refs[g][0, b, :, i * 2 * GROUP_W:i * 2 * GROUP_W + GROUP_W]
                    s_buf = jnp.where(qry == i, _dot_nt(qm, kbuf) * scale, s_buf)
                new_ok = nkey == nqry
            s_new = jnp.where(new_ok, s_new, NEG)
            mx = jnp.maximum(jnp.max(s_buf, axis=-1, keepdims=True), jnp.max(s_new, axis=-1, keepdims=True))
            p_buf = jnp.exp(s_buf - mx)
            p_new = jnp.exp(s_new - mx)
            den = jnp.sum(p_buf, axis=-1, keepdims=True) + jnp.sum(p_new, axis=-1, keepdims=True)
            acc = jnp.dot(p_new, new[:, GROUP_W:], preferred_element_type=F32)
            if dil == 1:
                acc = acc + jnp.dot(p_buf, cache_refs[g][0, b, :, GROUP_W:2 * GROUP_W],
                                    preferred_element_type=F32)
            else:
                for i in range(steps):
                    vbuf = cache_refs[g][0, b, :, i * 2 * GROUP_W + GROUP_W:(i + 1) * 2 * GROUP_W]
                    pv = jnp.dot(p_buf, vbuf, preferred_element_type=F32)
                    acc = acc + jnp.where(rix == i, pv, 0.0) + jnp.where(rix == i + steps, pv, 0.0)
            o_full = acc / den
            lse_full = jnp.broadcast_to(mx + jnp.log(den), (rows, GROUP_W))
            head0 = lane[:steps] < HEAD
            o_ref[b, :, g * GROUP_W:(g + 1) * GROUP_W] = jnp.where(head0, o_full[:steps], o_full[steps:])
            lse_ref[b, :, g * GROUP_W:(g + 1) * GROUP_W] = jnp.where(head0, lse_full[:steps], lse_full[steps:])
        return carry

    lax.fori_loop(0, bt, one, 0)


def _attn_sample(q2, new_kv, caches, layer, bt):
    bsz, rows, _ = q2.shape
    steps = rows // 2
    per_b = lambda w: pl.BlockSpec((bt, rows, w), lambda i: (i, 0, 0))
    cache_spec = lambda w: pl.BlockSpec((1, bt, ATTN_BLOCK, w), lambda i: (layer, i, 0, 0))
    out_spec = pl.BlockSpec((bt, steps, C_WIDTH), lambda i: (i, 0, 0))
    out = jax.ShapeDtypeStruct((bsz, steps, C_WIDTH), F32)
    return pl.pallas_call(
        functools.partial(_attn_sample_kernel, bt=bt, steps=steps),
        grid=(bsz // bt,),
        in_specs=[per_b(C_WIDTH), per_b(2 * GROUP_W), per_b(2 * GROUP_W), per_b(2 * GROUP_W),
                  cache_spec(2 * GROUP_W), cache_spec(steps * 2 * GROUP_W), cache_spec(steps * 2 * GROUP_W)],
        out_specs=[out_spec, out_spec],
        out_shape=[out, out],
        compiler_params=_params(1),
        name="attn_sample",
    )(q2, *new_kv, *caches)


def _outproj_kernel(ya_ref, yb_ref, o0_ref, o1_ref, o2_ref, l0_ref, l1_ref, l2_ref, cz_ref, x_ref, gate_ref,
                    w_ref, xo_ref):
    lses = (l0_ref[...], l1_ref[...], l2_ref[...])
    mx = jnp.maximum(jnp.maximum(lses[0], lses[1]), lses[2])
    es = [jnp.exp(l - mx) for l in lses]
    tot = es[0] + es[1] + es[2]
    cz = cz_ref[...]

    def mm(y, lo):
        return jnp.dot(y.astype(BF16), w_ref[lo:lo + y.shape[1], :], preferred_element_type=F32)

    out = mm(ya_ref[...], 0) + mm(yb_ref[...], A_WIDTH)
    for g, o_ref in enumerate((o0_ref, o1_ref, o2_ref)):
        yc = o_ref[...] * (es[g] / tot) * _silu(cz[:, g * GROUP_W:(g + 1) * GROUP_W])
        out = out + mm(yc, A_WIDTH + B_WIDTH + g * GROUP_W)
    xo_ref[...] = x_ref[...] + (1.0 + gate_ref[0]) * out


def _outproj(ya, yb, os_, lses, cz, x_rows, mod, w_out, tm):
    m = x_rows.shape[0]
    nblk = m // tm
    nb_mod, r, _ = mod.shape
    tiles_per_mod = nblk // nb_mod
    row = lambda w: pl.BlockSpec((tm, w), lambda i: (i, 0))
    return pl.pallas_call(
        _outproj_kernel,
        grid=(nblk,),
        in_specs=[row(A_WIDTH), row(B_WIDTH)] + [row(GROUP_W)] * 6 + [row(C_WIDTH), row(D_MODEL),
                  pl.BlockSpec((1, r, D_MODEL), lambda i: (i // tiles_per_mod, 0, 2)),
                  pl.BlockSpec((D_MODEL, D_MODEL), lambda i: (0, 0))],
        out_specs=row(D_MODEL),
        out_shape=jax.ShapeDtypeStruct((m, D_MODEL), F32),
        compiler_params=_params(1),
        name="outproj",
    )(ya, yb, *os_, *lses, cz, x_rows, mod, w_out)


def _final_kernel(x_ref, sh_ref, sc_ref, nw_ref, y_ref):
    x = x_ref[...]
    var = jnp.mean(x * x, axis=-1, keepdims=True)
    y_ref[...] = x * lax.rsqrt(var + EPS) * nw_ref[...] * (1.0 + sc_ref[0]) + sh_ref[0]


def _final(x_rows, mod, norm_w, tm):
    m = x_rows.shape[0]
    nblk = m // tm
    nb_mod, r, _ = mod.shape
    tiles_per_mod = nblk // nb_mod
    row = pl.BlockSpec((tm, D_MODEL), lambda i: (i, 0))
    return pl.pallas_call(
        _final_kernel,
        grid=(nblk,),
        in_specs=[row,
                  pl.BlockSpec((1, r, D_MODEL), lambda i: (i // tiles_per_mod, 0, 0)),
                  pl.BlockSpec((1, r, D_MODEL), lambda i: (i // tiles_per_mod, 0, 1)),
                  pl.BlockSpec((1, D_MODEL), lambda i: (0, 0))],
        out_specs=row,
        out_shape=jax.ShapeDtypeStruct((m, D_MODEL), F32),
        compiler_params=_params(1),
        name="final_norm",
    )(x_rows, mod, mod, norm_w.reshape(1, D_MODEL))


def _rope_tables(pos):
    half = ROT_DIM // 2
    inv_freq = ROPE_THETA ** (-jnp.arange(half, dtype=F32) * (2.0 / ROT_DIM))
    ang = pos.astype(F32)[:, None] * inv_freq[None, :]
    cos, sin = jnp.cos(ang), jnp.sin(ang)
    rows = pos.shape[0]
    one = jnp.ones((rows, HEAD - ROT_DIM), F32)
    zero_r = jnp.zeros((rows, HEAD - ROT_DIM), F32)
    zero_h = jnp.zeros((rows, half), F32)
    cos_h = jnp.concatenate([cos, cos, one], axis=1)
    sa_h = jnp.concatenate([-sin, zero_h, zero_r], axis=1)
    sb_h = jnp.concatenate([zero_h, sin, zero_r], axis=1)
    return tuple(jnp.concatenate([t] * HPG, axis=1) for t in (cos_h, sa_h, sb_h))


def _permute_w_in(w_in):
    depth = w_in.shape[0]
    pad = jnp.zeros((depth, D_MODEL, IN_PERM_WIDTH - w_in.shape[2]), w_in.dtype)
    return jnp.concatenate([w_in[:, :, :SRC_AB[0]], w_in[:, :, SRC_AB[1]:], w_in[:, :, SRC_AB[0]:SRC_AB[1]], pad],
                           axis=2).astype(BF16)


def _segment_ones():
    idx = jnp.arange(B_WIDTH) // HEAD
    return (idx[:, None] == idx[None, :]).astype(F32)


def _prompt_trunk(x, mods, mod_final, wts, seg):
    batch, t, _ = x.shape
    depth = wts["w_in"].shape[0]
    m = batch * t
    tm = 256
    rows = x.reshape(m, D_MODEL)
    tabs = _rope_tables(jnp.arange(t))
    zeros_a = jnp.zeros((batch, A_CONV - 1, A_WIDTH), F32)
    zeros_b = jnp.zeros((batch, B_CONV - 1, B_QKV), F32)
    st_a, st_b, st_g, kvs = [], [], [], [[] for _ in GROUPS]
    for l in range(depth):
        mod = mods[l].reshape(batch, 1, 3 * D_MODEL)
        ua, bqkv, bz, bab, qrot, kv0, kv1, kv2, cz = _inproj(rows, mod, wts["norm_w"][l], wts["w_in"][l], tabs, tm)
        ya, q, k, v, g, beta, sa, sb = _conv(ua, bqkv, bab, zeros_a, zeros_b, wts["conv_a_w"][l],
                                             wts["conv_b_w"][l], wts["a_log"][l], wts["dt_bias"][l], seg,
                                             groups=batch, stride=1, tt=tm)
        yb, s_fin = _gdn_prompt(q, k, v, g, beta, bz, wts["gdn_norm_w"][l], batch)
        os_, lses = [], []
        for gi, kv in enumerate((kv0, kv1, kv2)):
            o, lse = _attn_prompt(qrot, kv, gi, batch)
            os_.append(o)
            lses.append(lse)
            win = min(GROUPS[gi][0], t)
            kvs[gi].append(kv.reshape(batch, t, 2, HPG, HEAD)[:, t - win:])
        rows = _outproj(ya, yb, os_, lses, cz, rows, mod, wts["w_out"][l], tm)
        st_a.append(sa)
        st_b.append(sb)
        st_g.append(s_fin)
    y = _final(rows, mod_final.reshape(batch, 1, 2 * D_MODEL), wts["final_norm_w"], tm).reshape(batch, t, D_MODEL)
    return y, jnp.stack(st_a), jnp.stack(st_b), jnp.stack(st_g), [jnp.stack(r) for r in kvs]


def _sample_trunk(x, mods, mod_final, state_a, state_b, state_g, caches, past_len, wts, seg):
    bsz, steps, _ = x.shape
    depth = wts["w_in"].shape[0]
    m = bsz * steps
    tm = bsz

    def time_major(a):
        return a.transpose(1, 0, 2).reshape(a.shape[1] * bsz, a.shape[2])

    def batch_major(a, n):
        return a.reshape(n, bsz, a.shape[1]).transpose(1, 0, 2)

    def lanes_batch(a, n):
        return a.reshape(n, bsz, a.shape[1]).transpose(0, 2, 1)

    rows = time_major(x)
    tabs = _rope_tables(jnp.repeat(past_len + jnp.arange(steps), bsz))
    cache_views = [c.reshape(c.shape[0], bsz, ATTN_BLOCK, -1) for c in caches]
    st_a, st_b, st_g, kvs = [], [], [], [[] for _ in GROUPS]
    for l in range(depth):
        mod = mods[l][None]
        ua, bqkv, bz, bab, qrot, kv0, kv1, kv2, cz = _inproj(rows, mod, wts["norm_w"][l], wts["w_in"][l], tabs, tm)
        buf_a = time_major(state_a[l])[None]
        buf_b = time_major(state_b[l])[None]
        ya, q, k, v, g, beta, sa, sb = _conv(ua, bqkv, bab, buf_a, buf_b, wts["conv_a_w"][l],
                                             wts["conv_b_w"][l], wts["a_log"][l], wts["dt_bias"][l], seg,
                                             groups=1, stride=bsz, tt=m)
        nwb = jnp.broadcast_to(wts["gdn_norm_w"][l][:, None], (HEAD, bsz))
        yt, s_t = _gdn_sample(lanes_batch(q, steps), lanes_batch(k, steps), lanes_batch(v, steps),
                              lanes_batch(g[:, :SUBLANE], steps), lanes_batch(beta[:, :SUBLANE], steps),
                              lanes_batch(bz, steps), nwb, state_g[l].transpose(1, 2, 3, 0))
        yb = yt.transpose(0, 2, 1).reshape(m, B_WIDTH)
        dup = lambda a: jnp.concatenate([batch_major(a, steps)] * 2, axis=1)
        o_c, lse_c = _attn_sample(dup(qrot), [dup(kv) for kv in (kv0, kv1, kv2)], cache_views, l, 8)
        o_c, lse_c = time_major(o_c), time_major(lse_c)
        os_ = [o_c[:, gi * GROUP_W:(gi + 1) * GROUP_W] for gi in range(len(GROUPS))]
        lses = [lse_c[:, gi * GROUP_W:(gi + 1) * GROUP_W] for gi in range(len(GROUPS))]
        rows = _outproj(ya, yb, os_, lses, cz, rows, mod, wts["w_out"][l], tm)
        st_a.append(batch_major(sa[0], A_CONV - 1))
        st_b.append(batch_major(sb[0], B_CONV - 1))
        st_g.append(s_t.transpose(3, 0, 1, 2))
        for gi, kv in enumerate((kv0, kv1, kv2)):
            kvs[gi].append(batch_major(kv, steps).reshape(bsz, steps, 2, HPG, HEAD))
    y = batch_major(_final(rows, mod_final[None], wts["final_norm_w"], tm), steps)
    return y, jnp.stack(st_a), jnp.stack(st_b), jnp.stack(st_g), [jnp.stack(r) for r in kvs]


def kernel(x_prompt, x_sample, state_conv_a, state_conv_b, state_gdn, cache_kv_w128, cache_kv_w512,
           cache_kv_w2048, c_prompt, c_sample, w_in, w_out, w_ada, b_ada, norm_w, conv_a_w, conv_b_w,
           a_log, dt_bias, gdn_norm_w, final_norm_w, w_ada_final, b_ada_final):
    n_prompt = c_prompt.shape[0]
    c_all = jnp.concatenate([c_prompt, c_sample], axis=0)
    mods = _ada(c_all, w_ada, b_ada)
    mod_final = _ada(c_all, w_ada_final[None], b_ada_final[None])[0]
    wts = dict(w_in=_permute_w_in(w_in), w_out=w_out.astype(BF16), norm_w=norm_w, conv_a_w=conv_a_w,
               conv_b_w=conv_b_w, a_log=a_log, dt_bias=dt_bias, gdn_norm_w=gdn_norm_w,
               final_norm_w=final_norm_w)
    seg = _segment_ones()
    y_p, ca_p, cb_p, g_p, kv_p = _prompt_trunk(x_prompt, mods[:, :n_prompt], mod_final[:n_prompt], wts, seg)
    y_s, ca_s, cb_s, g_s, kv_s = _sample_trunk(
        x_sample, mods[:, n_prompt:], mod_final[n_prompt:], state_conv_a, state_conv_b, state_gdn,
        (cache_kv_w128, cache_kv_w512, cache_kv_w2048), PAST_LEN, wts, seg)
    return (y_p, y_s, ca_p, ca_s, cb_p, cb_s, g_p, g_s,
            kv_p[0], kv_s[0], kv_p[1], kv_s[1], kv_p[2], kv_s[2])
```

```python
import functools
import math

import jax
import jax.numpy as jnp
from jax import lax
from jax.experimental import pallas as pl
from jax.experimental.pallas import tpu as pltpu

F32 = jnp.float32
BF16 = jnp.bfloat16
HIGHEST = lax.Precision.HIGHEST

D_MODEL = 1024
HEAD = 64
A_WIDTH = 256
A_CONV = 3
B_HEADS = 6
B_WIDTH = B_HEADS * HEAD
B_QKV = 3 * B_WIDTH
B_CONV = 4
GDN_CHUNK = 64
GROUPS = ((128, 1), (512, 4), (2048, 16))
HPG = 2
GROUP_W = HPG * HEAD
C_WIDTH = len(GROUPS) * GROUP_W
ATTN_BLOCK = 128
ROT_DIM = HEAD // 4
ROPE_THETA = 500000.0
PAST_LEN = 2048
EPS = 1e-6
NEG = -1e30
LANE = 128
SUBLANE = 8

COL_A = (0, 1024)
COL_BQKV = (1024, 2176)
COL_BZ = (2176, 2560)
COL_CQ = (2560, 2944)
COL_CK = (2944, 3328)
COL_CV = (3328, 3712)
COL_CZ = (3712, 4096)
COL_AB = (4096, 4224)
IN_PERM_WIDTH = 4224
SRC_AB = (2560, 2572)

VMEM_LIMIT_BYTES = 56 * 1024 * 1024


def _params(n_axes):
    return pltpu.CompilerParams(dimension_semantics=("arbitrary",) * n_axes,
                                vmem_limit_bytes=VMEM_LIMIT_BYTES)


def _round_up(x, m):
    return -(-x // m) * m


def _silu(x):
    return x * jax.nn.sigmoid(x)


def _dot_nt(a, b, **kw):
    return lax.dot_general(a, b, (((1,), (1,)), ((), ())), preferred_element_type=F32, **kw)


def _dot_tn(a, b, **kw):
    return lax.dot_general(a, b, (((0,), (0,)), ((), ())), preferred_element_type=F32, **kw)


def _ada_kernel(c_ref, w_ref, b_ref, o_ref):
    c = c_ref[...].astype(BF16)
    w = w_ref[0].astype(BF16)
    o_ref[0] = jnp.dot(c, w, preferred_element_type=F32) + b_ref[0]


def _ada(c_all, w, b):
    n_layers, _, n = w.shape
    r = c_all.shape[0]
    tn = 1024
    return pl.pallas_call(
        _ada_kernel,
        grid=(n_layers, n // tn),
        in_specs=[pl.BlockSpec((r, D_MODEL), lambda l, j: (0, 0)),
                  pl.BlockSpec((1, D_MODEL, tn), lambda l, j: (l, 0, j)),
                  pl.BlockSpec((1, 1, tn), lambda l, j: (l, 0, j))],
        out_specs=pl.BlockSpec((1, r, tn), lambda l, j: (l, 0, j)),
        out_shape=jax.ShapeDtypeStruct((n_layers, r, n), F32),
        compiler_params=_params(2),
        name="ada",
    )(c_all, w, b.reshape(n_layers, 1, n))


def _inproj_kernel(x_ref, sh_ref, sc_ref, nw_ref, w_ref, cos_ref, sa_ref, sb_ref,
                   ua_ref, bqkv_ref, bz_ref, bab_ref, qrot_ref, kv0_ref, kv1_ref, kv2_ref, cz_ref):
    x = x_ref[...]
    var = jnp.mean(x * x, axis=-1, keepdims=True)
    hn = x * lax.rsqrt(var + EPS) * nw_ref[...]
    hn = hn * (1.0 + sc_ref[0]) + sh_ref[0]
    hb = hn.astype(BF16)

    def mm(cols):
        return jnp.dot(hb, w_ref[:, cols[0]:cols[1]], preferred_element_type=F32)

    ua_ref[...] = mm(COL_A)
    bqkv_ref[...] = mm(COL_BQKV)
    bz_ref[...] = mm(COL_BZ)
    bab_ref[...] = mm(COL_AB)
    cz_ref[...] = mm(COL_CZ)
    cq = mm(COL_CQ)
    ck = mm(COL_CK)
    cv = mm(COL_CV)
    cos = cos_ref[...]
    sa = sa_ref[...]
    sb = sb_ref[...]

    def rope(t):
        return t * cos + pltpu.roll(t, LANE - ROT_DIM // 2, 1) * sa + pltpu.roll(t, ROT_DIM // 2, 1) * sb

    for g, kv_ref in enumerate((kv0_ref, kv1_ref, kv2_ref)):
        lo, hi = g * GROUP_W, (g + 1) * GROUP_W
        qrot_ref[:, lo:hi] = rope(cq[:, lo:hi])
        kv_ref[:, :GROUP_W] = rope(ck[:, lo:hi])
        kv_ref[:, GROUP_W:] = cv[:, lo:hi]


def _inproj(x_rows, mod, norm_w, w_perm, rope_tabs, tm):
    m = x_rows.shape[0]
    nblk = m // tm
    nb_mod, r, _ = mod.shape
    tiles_per_mod = nblk // nb_mod
    nt_tab = rope_tabs[0].shape[0] // tm
    row = lambda w: pl.BlockSpec((tm, w), lambda i: (i, 0))
    tab = pl.BlockSpec((tm, LANE), lambda i: (i % nt_tab, 0))
    widths = (1024, B_QKV, B_WIDTH, LANE, C_WIDTH, 2 * GROUP_W, 2 * GROUP_W, 2 * GROUP_W, C_WIDTH)
    return pl.pallas_call(
        _inproj_kernel,
        grid=(nblk,),
        in_specs=[row(D_MODEL),
                  pl.BlockSpec((1, r, D_MODEL), lambda i: (i // tiles_per_mod, 0, 0)),
                  pl.BlockSpec((1, r, D_MODEL), lambda i: (i // tiles_per_mod, 0, 1)),
                  pl.BlockSpec((1, D_MODEL), lambda i: (0, 0)),
                  pl.BlockSpec((D_MODEL, IN_PERM_WIDTH), lambda i: (0, 0)),
                  tab, tab, tab],
        out_specs=[row(w) for w in widths],
        out_shape=[jax.ShapeDtypeStruct((m, w), F32) for w in widths],
        compiler_params=_params(1),
        name="inproj",
    )(x_rows, mod, mod, norm_w.reshape(1, D_MODEL), w_perm, *rope_tabs)


def _conv_kernel(ua_ref, bqkv_ref, bab_ref, bufa_ref, bufb_ref, wa_ref, wb_ref, alog_ref, dtb_ref, seg_ref,
                 ya_ref, q_ref, k_ref, v_ref, g_ref, beta_ref, sta_ref, stb_ref,
                 xpa_ref, xpb_ref, *, stride, tt):
    t = pl.program_id(1)

    def causal_conv(xp_ref, buf_ref, st_ref, x_new, w_ref, taps):
        halo = (taps - 1) * stride
        x0 = _round_up(halo, SUBLANE)

        @pl.when(t == 0)
        def _():
            xp_ref[x0 - halo:x0, :] = buf_ref[0]

        @pl.when(t > 0)
        def _():
            xp_ref[x0 - halo:x0, :] = xp_ref[x0 + tt - halo:x0 + tt, :]

        xp_ref[x0:x0 + tt, :] = x_new
        st_ref[0] = xp_ref[x0 + tt - halo:x0 + tt, :]
        y = xp_ref[x0 - halo:x0 - halo + tt, :] * w_ref[0:1, :]
        for j in range(1, taps):
            lo = x0 - halo + j * stride
            y = y + xp_ref[lo:lo + tt, :] * w_ref[j:j + 1, :]
        return y

    ua = ua_ref[...]
    ax, acg, abg, az = (ua[:, i * A_WIDTH:(i + 1) * A_WIDTH] for i in range(4))
    conv_a = causal_conv(xpa_ref, bufa_ref, sta_ref, acg * ax, wa_ref, A_CONV)
    ya_ref[...] = abg * conv_a * _silu(az)

    qkv = _silu(causal_conv(xpb_ref, bufb_ref, stb_ref, bqkv_ref[...], wb_ref, B_CONV))
    seg = seg_ref[...]

    def l2norm(x):
        ss = jnp.dot(x * x, seg, preferred_element_type=F32, precision=HIGHEST)
        return x * lax.rsqrt(ss + 1e-6)

    q_ref[...] = l2norm(qkv[:, :B_WIDTH]) * (HEAD ** -0.5)
    k_ref[...] = l2norm(qkv[:, B_WIDTH:2 * B_WIDTH])
    v_ref[...] = qkv[:, 2 * B_WIDTH:]

    ab = bab_ref[...]
    z = ab + dtb_ref[...]
    softplus = jnp.maximum(z, 0.0) + jnp.log1p(jnp.exp(-jnp.abs(z)))
    g_ref[...] = -jnp.exp(alog_ref[...]) * softplus
    beta_ref[...] = jax.nn.sigmoid(pltpu.roll(ab, LANE - B_HEADS, 1))


def _conv(ua, bqkv, bab, buf_a, buf_b, conv_a_w, conv_b_w, a_log, dt_bias, seg, *, groups, stride, tt):
    m = ua.shape[0]
    nt = m // (groups * tt)
    halo_a, halo_b = (A_CONV - 1) * stride, (B_CONV - 1) * stride
    row = lambda w: pl.BlockSpec((tt, w), lambda g, t: (g * nt + t, 0))
    full = lambda a: pl.BlockSpec(a.shape, lambda g, t: (0,) * a.ndim)
    per_group = lambda h, w: pl.BlockSpec((1, h, w), lambda g, t: (g, 0, 0))
    alog = jnp.zeros((1, LANE), F32).at[0, :B_HEADS].set(a_log)
    dtb = jnp.zeros((1, LANE), F32).at[0, :B_HEADS].set(dt_bias)
    out_w = (A_WIDTH, B_WIDTH, B_WIDTH, B_WIDTH, LANE, LANE)
    return pl.pallas_call(
        functools.partial(_conv_kernel, stride=stride, tt=tt),
        grid=(groups, nt),
        in_specs=[row(1024), row(B_QKV), row(LANE), per_group(halo_a, A_WIDTH), per_group(halo_b, B_QKV),
                  full(conv_a_w), full(conv_b_w), full(alog), full(dtb), full(seg)],
        out_specs=[row(w) for w in out_w] + [per_group(halo_a, A_WIDTH), per_group(halo_b, B_QKV)],
        out_shape=[jax.ShapeDtypeStruct((m, w), F32) for w in out_w]
        + [jax.ShapeDtypeStruct((groups, halo_a, A_WIDTH), F32),
           jax.ShapeDtypeStruct((groups, halo_b, B_QKV), F32)],
        scratch_shapes=[pltpu.VMEM((_round_up(halo_a, SUBLANE) + tt, A_WIDTH), F32),
                        pltpu.VMEM((_round_up(halo_b, SUBLANE) + tt, B_QKV), F32)],
        compiler_params=_params(2),
        name="conv",
    )(ua, bqkv, bab, buf_a, buf_b, conv_a_w, conv_b_w, alog, dtb, seg)


def _gdn_chunk_kernel(q_ref, k_ref, v_ref, g_ref, beta_ref, tri_ref,
                      mt_ref, nn_ref, qe_ref, au_ref, gc_ref, *, chunks):
    n = GDN_CHUNK
    ri = lax.broadcasted_iota(jnp.int32, (n, n), 0)
    ci = lax.broadcasted_iota(jnp.int32, (n, n), 1)
    tri = ri >= ci
    strict = ri > ci
    eye = jnp.where(ri == ci, 1.0, 0.0).astype(F32)
    for c in range(chunks):
        rows = slice(c * n, (c + 1) * n)
        q = q_ref[rows, :]
        k = k_ref[rows, :]
        v = v_ref[rows, :]
        beta = beta_ref[rows, :]
        gc = jnp.dot(tri_ref[...], g_ref[rows, :], preferred_element_type=F32, precision=HIGHEST)
        gc_ref[rows, :] = gc
        gct = gc.T
        mts, nns, qes, aus = [], [], [], []
        for h in range(B_HEADS):
            lo, hi = h * HEAD, (h + 1) * HEAD
            qh, kh, vh = q[:, lo:hi], k[:, lo:hi], v[:, lo:hi]
            gcol = gc[:, h:h + 1]
            grow = gct[h:h + 1, :]
            bcol = beta[:, h:h + 1]
            decay = jnp.where(tri, jnp.exp(jnp.where(tri, gcol - grow, 0.0)), 0.0)
            kb = kh * bcol
            lmat = jnp.where(strict, _dot_nt(kb, kh) * decay, 0.0)
            x = eye - lmat
            p = lmat
            for _ in range(5):
                p = jnp.dot(p, p, preferred_element_type=F32)
                x = x + jnp.dot(x, p, preferred_element_type=F32)
            egc = jnp.exp(gcol)
            uw = jnp.dot(x, jnp.concatenate([vh * bcol, kb * egc], axis=1), preferred_element_type=F32)
            a_intra = _dot_nt(qh, kh) * decay
            kd = kh * jnp.exp(gcol[n - 1:n, :] - gcol)
            auw = jnp.dot(a_intra, uw, preferred_element_type=F32)
            kuw = _dot_tn(kd, uw)
            aus.append(auw[:, :HEAD])
            qes.append(qh * egc - auw[:, HEAD:])
            nns.append(kuw[:, :HEAD])
            mts.append(-kuw[:, HEAD:])
        mt_ref[rows, :] = jnp.concatenate(mts, axis=1)
        nn_ref[rows, :] = jnp.concatenate(nns, axis=1)
        qe_ref[rows, :] = jnp.concatenate(qes, axis=1)
        au_ref[rows, :] = jnp.concatenate(aus, axis=1)


def _gdn_scan_kernel(mt_ref, nn_ref, qe_ref, au_ref, gc_ref, z_ref, nw_ref, y_ref, sfin_ref, s_ref, *, batch):
    c = pl.program_id(0)
    n = GDN_CHUNK

    @pl.when(c == 0)
    def _():
        s_ref[...] = jnp.zeros_like(s_ref)

    nw = nw_ref[...]
    for b in range(batch):
        mt, nn, qe, au = mt_ref[b], nn_ref[b], qe_ref[b], au_ref[b]
        eg = jnp.exp(gc_ref[b, n - 1:n, :])
        outs = []
        for h in range(B_HEADS):
            lo, hi = h * HEAD, (h + 1) * HEAD
            s = s_ref[b, h]
            r = jnp.dot(jnp.concatenate([mt[:, lo:hi], qe[:, lo:hi]], axis=0), s, preferred_element_type=F32)
            s_new = s * eg[:, h:h + 1] + r[:n] + nn[:, lo:hi]
            s_ref[b, h] = s_new
            sfin_ref[b, h] = s_new
            o = r[n:] + au[:, lo:hi]
            ms = jnp.mean(o * o, axis=-1, keepdims=True)
            outs.append(o * lax.rsqrt(ms + EPS) * nw)
        y_ref[b] = jnp.concatenate(outs, axis=1) * _silu(z_ref[b])


def _gdn_prompt(q, k, v, g, beta, z, norm_w, batch, chunks_per_step=4):
    m = q.shape[0]
    t = m // batch
    n = GDN_CHUNK
    nc = t // n
    rows = chunks_per_step * n
    row = lambda w: pl.BlockSpec((rows, w), lambda i: (i, 0))
    tri = jnp.tril(jnp.ones((n, n), F32))
    wide = jax.ShapeDtypeStruct((m, B_WIDTH), F32)
    mt, nn, qe, au, gc = pl.pallas_call(
        functools.partial(_gdn_chunk_kernel, chunks=chunks_per_step),
        grid=(m // rows,),
        in_specs=[row(B_WIDTH), row(B_WIDTH), row(B_WIDTH), row(LANE), row(LANE),
                  pl.BlockSpec((n, n), lambda i: (0, 0))],
        out_specs=[row(B_WIDTH)] * 4 + [row(LANE)],
        out_shape=[wide] * 4 + [jax.ShapeDtypeStruct((m, LANE), F32)],
        compiler_params=_params(1),
        name="gdn_chunk",
    )(q, k, v, g, beta, tri)
    per_chunk = lambda w: pl.BlockSpec((batch, n, w), lambda c: (0, c, 0))
    state = pl.BlockSpec((batch, B_HEADS, HEAD, HEAD), lambda c: (0, 0, 0, 0))
    view = lambda a: a.reshape(batch, t, a.shape[1])
    y, s_fin = pl.pallas_call(
        functools.partial(_gdn_scan_kernel, batch=batch),
        grid=(nc,),
        in_specs=[per_chunk(B_WIDTH)] * 4 + [per_chunk(LANE), per_chunk(B_WIDTH),
                                             pl.BlockSpec((1, HEAD), lambda c: (0, 0))],
        out_specs=[per_chunk(B_WIDTH), state],
        out_shape=[jax.ShapeDtypeStruct((batch, t, B_WIDTH), F32),
                   jax.ShapeDtypeStruct((batch, B_HEADS, HEAD, HEAD), F32)],
        scratch_shapes=[pltpu.VMEM((batch, B_HEADS, HEAD, HEAD), F32)],
        compiler_params=_params(1),
        name="gdn_scan",
    )(view(mt), view(nn), view(qe), view(au), view(gc), view(z), norm_w.reshape(1, HEAD))
    return y.reshape(m, B_WIDTH), s_fin


def _gdn_sample_kernel(q_ref, k_ref, v_ref, g_ref, beta_ref, z_ref, nw_ref, s0_ref, y_ref, s_ref, *, steps):
    h = pl.program_id(0)
    bsz = s_ref.shape[-1]
    zero = jnp.zeros((HEAD, bsz), F32)

    def bcast_row(ref, i, dk):
        return jnp.broadcast_to(ref[i, pl.ds(dk, 1), :], (HEAD, bsz))

    def head_row(ref, i):
        return ref[i, pl.ds(h, 1), :]

    def finish(i, o):
        ms = jnp.mean(o * o, axis=0, keepdims=True)
        y_ref[i] = o * lax.rsqrt(ms + EPS) * nw_ref[...] * _silu(z_ref[i])

    gam = jnp.exp(head_row(g_ref, 0))

    def first(dk, ks):
        s = s0_ref[0, dk] * gam
        s_ref[0, dk] = s
        return ks + bcast_row(k_ref, 0, dk) * s

    ks = lax.fori_loop(0, HEAD, first, zero)
    vn = head_row(beta_ref, 0) * (v_ref[0] - ks)

    for i in range(1, steps):
        gam = jnp.exp(head_row(g_ref, i))

        def mid(dk, carry, i=i, vn=vn, gam=gam):
            o, ks = carry
            s = s_ref[0, dk] + bcast_row(k_ref, i - 1, dk) * vn
            o = o + bcast_row(q_ref, i - 1, dk) * s
            s = s * gam
            s_ref[0, dk] = s
            return o, ks + bcast_row(k_ref, i, dk) * s

        o, ks = lax.fori_loop(0, HEAD, mid, (zero, zero))
        finish(i - 1, o)
        vn = head_row(beta_ref, i) * (v_ref[i] - ks)

    def last(dk, o):
        s = s_ref[0, dk] + bcast_row(k_ref, steps - 1, dk) * vn
        s_ref[0, dk] = s
        return o + bcast_row(q_ref, steps - 1, dk) * s

    finish(steps - 1, lax.fori_loop(0, HEAD, last, zero))


def _gdn_sample(qt, kt, vt, gt, bt, zt, nwb, s0t):
    steps, _, bsz = qt.shape
    per_head = pl.BlockSpec((steps, HEAD, bsz), lambda h: (0, h, 0))
    small = pl.BlockSpec((steps, SUBLANE, bsz), lambda h: (0, 0, 0))
    state = pl.BlockSpec((1, HEAD, HEAD, bsz), lambda h: (h, 0, 0, 0))
    return pl.pallas_call(
        functools.partial(_gdn_sample_kernel, steps=steps),
        grid=(B_HEADS,),
        in_specs=[per_head, per_head, per_head, small, small, per_head,
                  pl.BlockSpec((HEAD, bsz), lambda h: (0, 0)), state],
        out_specs=[per_head, state],
        out_shape=[jax.ShapeDtypeStruct((steps, B_WIDTH, bsz), F32),
                   jax.ShapeDtypeStruct((B_HEADS, HEAD, HEAD, bsz), F32)],
        compiler_params=_params(1),
        name="gdn_sample",
    )(qt, kt, vt, gt, bt, zt, nwb, s0t)


def _attn_prompt_kernel(q_ref, kc_ref, vc_ref, kp_ref, vp_ref, o_ref, lse_ref):
    i = pl.program_id(2)
    n = ATTN_BLOCK
    q = q_ref[0]
    kc, vc, kp, vp = kc_ref[0], vc_ref[0], kp_ref[0], vp_ref[0]
    lane = lax.broadcasted_iota(jnp.int32, (n, GROUP_W), 1)
    head0 = lane < HEAD
    qi = lax.broadcasted_iota(jnp.int32, (n, n), 0)
    kj = lax.broadcasted_iota(jnp.int32, (n, n), 1)
    mask_c = kj <= qi
    mask_p = jnp.logical_and(kj >= qi, i > 0)
    res = []
    for sel in (head0, jnp.logical_not(head0)):
        qh = jnp.where(sel, q, 0.0)
        sc = jnp.where(mask_c, _dot_nt(qh, kc) * (HEAD ** -0.5), NEG)
        sp = jnp.where(mask_p, _dot_nt(qh, kp) * (HEAD ** -0.5), NEG)
        mx = jnp.maximum(jnp.max(sc, axis=-1, keepdims=True), jnp.max(sp, axis=-1, keepdims=True))
        pc = jnp.exp(sc - mx)
        pp = jnp.exp(sp - mx)
        den = jnp.sum(pc, axis=-1, keepdims=True) + jnp.sum(pp, axis=-1, keepdims=True)
        acc = jnp.dot(pc, vc, preferred_element_type=F32) + jnp.dot(pp, vp, preferred_element_type=F32)
        res.append((acc / den, mx + jnp.log(den)))
    o_ref[0] = jnp.where(head0, res[0][0], res[1][0])
    lse_ref[0] = jnp.where(head0, jnp.broadcast_to(res[0][1], (n, GROUP_W)),
                           jnp.broadcast_to(res[1][1], (n, GROUP_W)))


def _attn_prompt(qrot, kv, group, batch):
    _, dil = GROUPS[group]
    m = qrot.shape[0]
    t = m // batch
    ls = t // dil
    nb = ls // ATTN_BLOCK
    n = ATTN_BLOCK
    qv = qrot.reshape(batch, ls, dil * C_WIDTH)
    kvv = kv.reshape(batch, ls, dil * 2 * GROUP_W)
    blk = lambda f: pl.BlockSpec((1, n, GROUP_W), f)
    out = jax.ShapeDtypeStruct((batch, ls, dil * GROUP_W), F32)
    o, lse = pl.pallas_call(
        _attn_prompt_kernel,
        grid=(batch, dil, nb),
        in_specs=[blk(lambda b, r, i: (b, i, r * len(GROUPS) + group)),
                  blk(lambda b, r, i: (b, i, 2 * r)),
                  blk(lambda b, r, i: (b, i, 2 * r + 1)),
                  blk(lambda b, r, i: (b, jnp.maximum(i - 1, 0), 2 * r)),
                  blk(lambda b, r, i: (b, jnp.maximum(i - 1, 0), 2 * r + 1))],
        out_specs=[blk(lambda b, r, i: (b, i, r)), blk(lambda b, r, i: (b, i, r))],
        out_shape=[out, out],
        compiler_params=_params(3),
        name=f"attn_prompt_g{group}",
    )(qv, kvv, kvv, kvv, kvv)
    return o.reshape(m, GROUP_W), lse.reshape(m, GROUP_W)


def _attn_sample_kernel(q_ref, n0_ref, n1_ref, n2_ref, c0_ref, c1_ref, c2_ref, o_ref, lse_ref, *, bt, steps):
    rows = 2 * steps
    scale = HEAD ** -0.5
    lane = lax.broadcasted_iota(jnp.int32, (rows, GROUP_W), 1)
    rix = lax.broadcasted_iota(jnp.int32, (rows, GROUP_W), 0)
    own_head = (lane < HEAD) == (rix < steps)
    nkey = lax.broadcasted_iota(jnp.int32, (rows, rows), 1)
    nqry = lax.broadcasted_iota(jnp.int32, (rows, rows), 0) % steps
    new_refs = (n0_ref, n1_ref, n2_ref)
    cache_refs = (c0_ref, c1_ref, c2_ref)

    def one(b, carry):
        q_all = q_ref[b]
        for g, (win, dil) in enumerate(GROUPS):
            qm = jnp.where(own_head, q_all[:, g * GROUP_W:(g + 1) * GROUP_W], 0.0)
            new = new_refs[g][b]
            s_new = _dot_nt(qm, new[:, :GROUP_W]) * scale
            kt = cache_refs[g][0, b, 0].reshape(GROUP_W, win)
            vt = cache_refs[g][0, b, 1].reshape(GROUP_W, win)
            pos = lax.broadcasted_iota(jnp.int32, (rows, win), 1)
            qry = lax.broadcasted_iota(jnp.int32, (rows, win), 0) % steps
            if dil == 1:
                ok = pos >= qry
                new_ok = jnp.logical_and(nkey < steps, nkey <= nqry)
            else:
                ok = (pos % dil) == qry
                new_ok = nkey == nqry
            s_buf = jnp.where(ok, jnp.dot(qm, kt, preferred_element_type=F32) * scale, NEG)
            s_new = jnp.where(new_ok, s_new, NEG)
            mx = jnp.maximum(jnp.max(s_buf, axis=-1, keepdims=True), jnp.max(s_new, axis=-1, keepdims=True))
            p_buf = jnp.exp(s_buf - mx)
            p_new = jnp.exp(s_new - mx)
            den = jnp.sum(p_buf, axis=-1, keepdims=True) + jnp.sum(p_new, axis=-1, keepdims=True)
            acc = jnp.dot(p_new, new[:, GROUP_W:], preferred_element_type=F32) + _dot_nt(p_buf, vt)
            o_full = acc / den
            lse_full = jnp.broadcast_to(mx + jnp.log(den), (rows, GROUP_W))
            head0 = lane[:steps] < HEAD
            o_ref[b, :, g * GROUP_W:(g + 1) * GROUP_W] = jnp.where(head0, o_full[:steps], o_full[steps:])
            lse_ref[b, :, g * GROUP_W:(g + 1) * GROUP_W] = jnp.where(head0, lse_full[:steps], lse_full[steps:])
        return carry

    lax.fori_loop(0, bt, one, 0)


def _attn_sample(q2, new_kv, caches, layer, bt):
    bsz, rows, _ = q2.shape
    steps = rows // 2
    per_b = lambda w: pl.BlockSpec((bt, rows, w), lambda i: (i, 0, 0))
    cache_spec = lambda win: pl.BlockSpec((1, bt, 2, HPG, HEAD, win), lambda i: (layer, i, 0, 0, 0, 0))
    out_spec = pl.BlockSpec((bt, steps, C_WIDTH), lambda i: (i, 0, 0))
    out = jax.ShapeDtypeStruct((bsz, steps, C_WIDTH), F32)
    return pl.pallas_call(
        functools.partial(_attn_sample_kernel, bt=bt, steps=steps),
        grid=(bsz // bt,),
        in_specs=[per_b(C_WIDTH), per_b(2 * GROUP_W), per_b(2 * GROUP_W), per_b(2 * GROUP_W)]
        + [cache_spec(win) for win, _ in GROUPS],
        out_specs=[out_spec, out_spec],
        out_shape=[out, out],
        compiler_params=_params(1),
        name="attn_sample",
    )(q2, *new_kv, *caches)


def _outproj_kernel(ya_ref, yb_ref, o0_ref, o1_ref, o2_ref, l0_ref, l1_ref, l2_ref, cz_ref, x_ref, gate_ref,
                    w_ref, xo_ref):
    lses = (l0_ref[...], l1_ref[...], l2_ref[...])
    mx = jnp.maximum(jnp.maximum(lses[0], lses[1]), lses[2])
    es = [jnp.exp(l - mx) for l in lses]
    tot = es[0] + es[1] + es[2]
    cz = cz_ref[...]

    def mm(y, lo):
        return jnp.dot(y.astype(BF16), w_ref[lo:lo + y.shape[1], :], preferred_element_type=F32)

    out = mm(ya_ref[...], 0) + mm(yb_ref[...], A_WIDTH)
    for g, o_ref in enumerate((o0_ref, o1_ref, o2_ref)):
        yc = o_ref[...] * (es[g] / tot) * _silu(cz[:, g * GROUP_W:(g + 1) * GROUP_W])
        out = out + mm(yc, A_WIDTH + B_WIDTH + g * GROUP_W)
    xo_ref[...] = x_ref[...] + (1.0 + gate_ref[0]) * out


def _outproj(ya, yb, os_, lses, cz, x_rows, mod, w_out, tm):
    m = x_rows.shape[0]
    nblk = m // tm
    nb_mod, r, _ = mod.shape
    tiles_per_mod = nblk // nb_mod
    row = lambda w: pl.BlockSpec((tm, w), lambda i: (i, 0))
    return pl.pallas_call(
        _outproj_kernel,
        grid=(nblk,),
        in_specs=[row(A_WIDTH), row(B_WIDTH)] + [row(GROUP_W)] * 6 + [row(C_WIDTH), row(D_MODEL),
                  pl.BlockSpec((1, r, D_MODEL), lambda i: (i // tiles_per_mod, 0, 2)),
                  pl.BlockSpec((D_MODEL, D_MODEL), lambda i: (0, 0))],
        out_specs=row(D_MODEL),
        out_shape=jax.ShapeDtypeStruct((m, D_MODEL), F32),
        compiler_params=_params(1),
        name="outproj",
    )(ya, yb, *os_, *lses, cz, x_rows, mod, w_out)


def _final_kernel(x_ref, sh_ref, sc_ref, nw_ref, y_ref):
    x = x_ref[...]
    var = jnp.mean(x * x, axis=-1, keepdims=True)
    y_ref[...] = x * lax.rsqrt(var + EPS) * nw_ref[...] * (1.0 + sc_ref[0]) + sh_ref[0]


def _final(x_rows, mod, norm_w, tm):
    m = x_rows.shape[0]
    nblk = m // tm
    nb_mod, r, _ = mod.shape
    tiles_per_mod = nblk // nb_mod
    row = pl.BlockSpec((tm, D_MODEL), lambda i: (i, 0))
    return pl.pallas_call(
        _final_kernel,
        grid=(nblk,),
        in_specs=[row,
                  pl.BlockSpec((1, r, D_MODEL), lambda i: (i // tiles_per_mod, 0, 0)),
                  pl.BlockSpec((1, r, D_MODEL), lambda i: (i // tiles_per_mod, 0, 1)),
                  pl.BlockSpec((1, D_MODEL), lambda i: (0, 0))],
        out_specs=row,
        out_shape=jax.ShapeDtypeStruct((m, D_MODEL), F32),
        compiler_params=_params(1),
        name="final_norm",
    )(x_rows, mod, mod, norm_w.reshape(1, D_MODEL))


def _rope_tables(pos):
    half = ROT_DIM // 2
    inv_freq = ROPE_THETA ** (-jnp.arange(half, dtype=F32) * (2.0 / ROT_DIM))
    ang = pos.astype(F32)[:, None] * inv_freq[None, :]
    cos, sin = jnp.cos(ang), jnp.sin(ang)
    rows = pos.shape[0]
    one = jnp.ones((rows, HEAD - ROT_DIM), F32)
    zero_r = jnp.zeros((rows, HEAD - ROT_DIM), F32)
    zero_h = jnp.zeros((rows, half), F32)
    cos_h = jnp.concatenate([cos, cos, one], axis=1)
    sa_h = jnp.concatenate([-sin, zero_h, zero_r], axis=1)
    sb_h = jnp.concatenate([zero_h, sin, zero_r], axis=1)
    return tuple(jnp.concatenate([t] * HPG, axis=1) for t in (cos_h, sa_h, sb_h))


def _permute_w_in(w_in):
    depth = w_in.shape[0]
    pad = jnp.zeros((depth, D_MODEL, IN_PERM_WIDTH - w_in.shape[2]), w_in.dtype)
    return jnp.concatenate([w_in[:, :, :SRC_AB[0]], w_in[:, :, SRC_AB[1]:], w_in[:, :, SRC_AB[0]:SRC_AB[1]], pad],
                           axis=2).astype(BF16)


def _segment_ones():
    idx = jnp.arange(B_WIDTH) // HEAD
    return (idx[:, None] == idx[None, :]).astype(F32)


def _prompt_trunk(x, mods, mod_final, wts, seg):
    batch, t, _ = x.shape
    depth = wts["w_in"].shape[0]
    m = batch * t
    tm = 256
    rows = x.reshape(m, D_MODEL)
    tabs = _rope_tables(jnp.arange(t))
    zeros_a = jnp.zeros((batch, A_CONV - 1, A_WIDTH), F32)
    zeros_b = jnp.zeros((batch, B_CONV - 1, B_QKV), F32)
    st_a, st_b, st_g, kvs = [], [], [], [[] for _ in GROUPS]
    for l in range(depth):
        mod = mods[l].reshape(batch, 1, 3 * D_MODEL)
        ua, bqkv, bz, bab, qrot, kv0, kv1, kv2, cz = _inproj(rows, mod, wts["norm_w"][l], wts["w_in"][l], tabs, tm)
        ya, q, k, v, g, beta, sa, sb = _conv(ua, bqkv, bab, zeros_a, zeros_b, wts["conv_a_w"][l],
                                             wts["conv_b_w"][l], wts["a_log"][l], wts["dt_bias"][l], seg,
                                             groups=batch, stride=1, tt=tm)
        yb, s_fin = _gdn_prompt(q, k, v, g, beta, bz, wts["gdn_norm_w"][l], batch)
        os_, lses = [], []
        for gi, kv in enumerate((kv0, kv1, kv2)):
            o, lse = _attn_prompt(qrot, kv, gi, batch)
            os_.append(o)
            lses.append(lse)
            win = min(GROUPS[gi][0], t)
            kvs[gi].append(kv.reshape(batch, t, 2, HPG, HEAD)[:, t - win:])
        rows = _outproj(ya, yb, os_, lses, cz, rows, mod, wts["w_out"][l], tm)
        st_a.append(sa)
        st_b.append(sb)
        st_g.append(s_fin)
    y = _final(rows, mod_final.reshape(batch, 1, 2 * D_MODEL), wts["final_norm_w"], tm).reshape(batch, t, D_MODEL)
    return y, jnp.stack(st_a), jnp.stack(st_b), jnp.stack(st_g), [jnp.stack(r) for r in kvs]


def _sample_trunk(x, mods, mod_final, state_a, state_b, state_g, caches, past_len, wts, seg):
    bsz, steps, _ = x.shape
    depth = wts["w_in"].shape[0]
    m = bsz * steps
    tm = bsz

    def time_major(a):
        return a.transpose(1, 0, 2).reshape(a.shape[1] * bsz, a.shape[2])

    def batch_major(a, n):
        return a.reshape(n, bsz, a.shape[1]).transpose(1, 0, 2)

    def lanes_batch(a, n):
        return a.reshape(n, bsz, a.shape[1]).transpose(0, 2, 1)

    rows = time_major(x)
    tabs = _rope_tables(jnp.repeat(past_len + jnp.arange(steps), bsz))
    cache_views = [c.transpose(0, 1, 3, 4, 5, 2) for c in caches]
    st_a, st_b, st_g, kvs = [], [], [], [[] for _ in GROUPS]
    for l in range(depth):
        mod = mods[l][None]
        ua, bqkv, bz, bab, qrot, kv0, kv1, kv2, cz = _inproj(rows, mod, wts["norm_w"][l], wts["w_in"][l], tabs, tm)
        buf_a = time_major(state_a[l])[None]
        buf_b = time_major(state_b[l])[None]
        ya, q, k, v, g, beta, sa, sb = _conv(ua, bqkv, bab, buf_a, buf_b, wts["conv_a_w"][l],
                                             wts["conv_b_w"][l], wts["a_log"][l], wts["dt_bias"][l], seg,
                                             groups=1, stride=bsz, tt=m)
        nwb = jnp.broadcast_to(wts["gdn_norm_w"][l][:, None], (HEAD, bsz))
        yt, s_t = _gdn_sample(lanes_batch(q, steps), lanes_batch(k, steps), lanes_batch(v, steps),
                              lanes_batch(g[:, :SUBLANE], steps), lanes_batch(beta[:, :SUBLANE], steps),
                              lanes_batch(bz, steps), nwb, state_g[l].transpose(1, 2, 3, 0))
        yb = yt.transpose(0, 2, 1).reshape(m, B_WIDTH)
        dup = lambda a: jnp.concatenate([batch_major(a, steps)] * 2, axis=1)
        o_c, lse_c = _attn_sample(dup(qrot), [dup(kv) for kv in (kv0, kv1, kv2)], cache_views, l, 4)
        o_c, lse_c = time_major(o_c), time_major(lse_c)
        os_ = [o_c[:, gi * GROUP_W:(gi + 1) * GROUP_W] for gi in range(len(GROUPS))]
        lses = [lse_c[:, gi * GROUP_W:(gi + 1) * GROUP_W] for gi in range(len(GROUPS))]
        rows = _outproj(ya, yb, os_, lses, cz, rows, mod, wts["w_out"][l], tm)
        st_a.append(batch_major(sa[0], A_CONV - 1))
        st_b.append(batch_major(sb[0], B_CONV - 1))
        st_g.append(s_t.transpose(3, 0, 1, 2))
        for gi, kv in enumerate((kv0, kv1, kv2)):
            kvs[gi].append(batch_major(kv, steps).reshape(bsz, steps, 2, HPG, HEAD))
    y = batch_major(_final(rows, mod_final[None], wts["final_norm_w"], tm), steps)
    return y, jnp.stack(st_a), jnp.stack(st_b), jnp.stack(st_g), [jnp.stack(r) for r in kvs]


def kernel(x_prompt, x_sample, state_conv_a, state_conv_b, state_gdn, cache_kv_w128, cache_kv_w512,
           cache_kv_w2048, c_prompt, c_sample, w_in, w_out, w_ada, b_ada, norm_w, conv_a_w, conv_b_w,
           a_log, dt_bias, gdn_norm_w, final_norm_w, w_ada_final, b_ada_final):
    n_prompt = c_prompt.shape[0]
    c_all = jnp.concatenate([c_prompt, c_sample], axis=0)
    mods = _ada(c_all, w_ada, b_ada)
    mod_final = _ada(c_all, w_ada_final[None], b_ada_final[None])[0]
    wts = dict(w_in=_permute_w_in(w_in), w_out=w_out.astype(BF16), norm_w=norm_w, conv_a_w=conv_a_w,
               conv_b_w=conv_b_w, a_log=a_log, dt_bias=dt_bias, gdn_norm_w=gdn_norm_w,
               final_norm_w=final_norm_w)
    seg = _segment_ones()
    y_p, ca_p, cb_p, g_p, kv_p = _prompt_trunk(x_prompt, mods[:, :n_prompt], mod_final[:n_prompt], wts, seg)
    y_s, ca_s, cb_s, g_s, kv_s = _sample_trunk(
        x_sample, mods[:, n_prompt:], mod_final[n_prompt:], state_conv_a, state_conv_b, state_gdn,
        (cache_kv_w128, cache_kv_w512, cache_kv_w2048), PAST_LEN, wts, seg)
    return (y_p, y_s, ca_p, ca_s, cb_p, cb_s, g_p, g_s,
            kv_p[0], kv_s[0], kv_p[1], kv_s[1], kv_p[2], kv_s[2])
```

```python
import functools
import math

import jax
import jax.numpy as jnp
from jax import lax
from jax.experimental import pallas as pl
from jax.experimental.pallas import tpu as pltpu

F32 = jnp.float32
BF16 = jnp.bfloat16
HIGHEST = lax.Precision.HIGHEST

D_MODEL = 1024
HEAD = 64
A_WIDTH = 256
A_CONV = 3
B_HEADS = 6
B_WIDTH = B_HEADS * HEAD
B_QKV = 3 * B_WIDTH
B_CONV = 4
GDN_CHUNK = 64
GROUPS = ((128, 1), (512, 4), (2048, 16))
HPG = 2
GROUP_W = HPG * HEAD
C_WIDTH = len(GROUPS) * GROUP_W
ATTN_BLOCK = 128
ROT_DIM = HEAD // 4
ROPE_THETA = 500000.0
PAST_LEN = 2048
EPS = 1e-6
NEG = -1e30
LANE = 128
SUBLANE = 8

COL_A = (0, 1024)
COL_BQKV = (1024, 2176)
COL_BZ = (2176, 2560)
COL_CQ = (2560, 2944)
COL_CK = (2944, 3328)
COL_CV = (3328, 3712)
COL_CZ = (3712, 4096)
COL_AB = (4096, 4224)
IN_PERM_WIDTH = 4224
SRC_AB = (2560, 2572)

VMEM_LIMIT_BYTES = 56 * 1024 * 1024


def _params(n_axes):
    return pltpu.CompilerParams(dimension_semantics=("arbitrary",) * n_axes,
                                vmem_limit_bytes=VMEM_LIMIT_BYTES)


def _round_up(x, m):
    return -(-x // m) * m


def _silu(x):
    return x * jax.nn.sigmoid(x)


def _dot_nt(a, b, **kw):
    return lax.dot_general(a, b, (((1,), (1,)), ((), ())), preferred_element_type=F32, **kw)


def _dot_tn(a, b, **kw):
    return lax.dot_general(a, b, (((0,), (0,)), ((), ())), preferred_element_type=F32, **kw)


def _ada_kernel(c_ref, w_ref, b_ref, o_ref):
    c = c_ref[...].astype(BF16)
    w = w_ref[0].astype(BF16)
    o_ref[0] = jnp.dot(c, w, preferred_element_type=F32) + b_ref[0]


def _ada(c_all, w, b):
    n_layers, _, n = w.shape
    r = c_all.shape[0]
    tn = 1024
    return pl.pallas_call(
        _ada_kernel,
        grid=(n_layers, n // tn),
        in_specs=[pl.BlockSpec((r, D_MODEL), lambda l, j: (0, 0)),
                  pl.BlockSpec((1, D_MODEL, tn), lambda l, j: (l, 0, j)),
                  pl.BlockSpec((1, 1, tn), lambda l, j: (l, 0, j))],
        out_specs=pl.BlockSpec((1, r, tn), lambda l, j: (l, 0, j)),
        out_shape=jax.ShapeDtypeStruct((n_layers, r, n), F32),
        compiler_params=_params(2),
        name="ada",
    )(c_all, w, b.reshape(n_layers, 1, n))


def _inproj_kernel(x_ref, sh_ref, sc_ref, nw_ref, w_ref, cos_ref, sa_ref, sb_ref,
                   ua_ref, bqkv_ref, bz_ref, bab_ref, qrot_ref, kv0_ref, kv1_ref, kv2_ref, cz_ref):
    x = x_ref[...]
    var = jnp.mean(x * x, axis=-1, keepdims=True)
    hn = x * lax.rsqrt(var + EPS) * nw_ref[...]
    hn = hn * (1.0 + sc_ref[0]) + sh_ref[0]
    hb = hn.astype(BF16)

    def mm(cols):
        return jnp.dot(hb, w_ref[:, cols[0]:cols[1]], preferred_element_type=F32)

    ua_ref[...] = mm(COL_A)
    bqkv_ref[...] = mm(COL_BQKV)
    bz_ref[...] = mm(COL_BZ)
    bab_ref[...] = mm(COL_AB)
    cz_ref[...] = mm(COL_CZ)
    cq = mm(COL_CQ)
    ck = mm(COL_CK)
    cv = mm(COL_CV)
    cos = cos_ref[...]
    sa = sa_ref[...]
    sb = sb_ref[...]

    def rope(t):
        return t * cos + pltpu.roll(t, LANE - ROT_DIM // 2, 1) * sa + pltpu.roll(t, ROT_DIM // 2, 1) * sb

    for g, kv_ref in enumerate((kv0_ref, kv1_ref, kv2_ref)):
        lo, hi = g * GROUP_W, (g + 1) * GROUP_W
        qrot_ref[:, lo:hi] = rope(cq[:, lo:hi])
        kv_ref[:, :GROUP_W] = rope(ck[:, lo:hi])
        kv_ref[:, GROUP_W:] = cv[:, lo:hi]


def _inproj(x_rows, mod, norm_w, w_perm, rope_tabs, tm):
    m = x_rows.shape[0]
    nblk = m // tm
    nb_mod, r, _ = mod.shape
    tiles_per_mod = nblk // nb_mod
    nt_tab = rope_tabs[0].shape[0] // tm
    row = lambda w: pl.BlockSpec((tm, w), lambda i: (i, 0))
    tab = pl.BlockSpec((tm, LANE), lambda i: (i % nt_tab, 0))
    widths = (1024, B_QKV, B_WIDTH, LANE, C_WIDTH, 2 * GROUP_W, 2 * GROUP_W, 2 * GROUP_W, C_WIDTH)
    return pl.pallas_call(
        _inproj_kernel,
        grid=(nblk,),
        in_specs=[row(D_MODEL),
                  pl.BlockSpec((1, r, D_MODEL), lambda i: (i // tiles_per_mod, 0, 0)),
                  pl.BlockSpec((1, r, D_MODEL), lambda i: (i // tiles_per_mod, 0, 1)),
                  pl.BlockSpec((1, D_MODEL), lambda i: (0, 0)),
                  pl.BlockSpec((D_MODEL, IN_PERM_WIDTH), lambda i: (0, 0)),
                  tab, tab, tab],
        out_specs=[row(w) for w in widths],
        out_shape=[jax.ShapeDtypeStruct((m, w), F32) for w in widths],
        compiler_params=_params(1),
        name="inproj",
    )(x_rows, mod, mod, norm_w.reshape(1, D_MODEL), w_perm, *rope_tabs)


def _conv_kernel(ua_ref, bqkv_ref, bab_ref, bufa_ref, bufb_ref, wa_ref, wb_ref, alog_ref, dtb_ref, seg_ref,
                 ya_ref, q_ref, k_ref, v_ref, g_ref, beta_ref, sta_ref, stb_ref,
                 xpa_ref, xpb_ref, *, stride, tt):
    t = pl.program_id(1)

    def causal_conv(xp_ref, buf_ref, st_ref, x_new, w_ref, taps):
        halo = (taps - 1) * stride
        x0 = _round_up(halo, SUBLANE)

        @pl.when(t == 0)
        def _():
            xp_ref[x0 - halo:x0, :] = buf_ref[0]

        @pl.when(t > 0)
        def _():
            xp_ref[x0 - halo:x0, :] = xp_ref[x0 + tt - halo:x0 + tt, :]

        xp_ref[x0:x0 + tt, :] = x_new
        st_ref[0] = xp_ref[x0 + tt - halo:x0 + tt, :]
        y = xp_ref[x0 - halo:x0 - halo + tt, :] * w_ref[0:1, :]
        for j in range(1, taps):
            lo = x0 - halo + j * stride
            y = y + xp_ref[lo:lo + tt, :] * w_ref[j:j + 1, :]
        return y

    ua = ua_ref[...]
    ax, acg, abg, az = (ua[:, i * A_WIDTH:(i + 1) * A_WIDTH] for i in range(4))
    conv_a = causal_conv(xpa_ref, bufa_ref, sta_ref, acg * ax, wa_ref, A_CONV)
    ya_ref[...] = abg * conv_a * _silu(az)

    qkv = _silu(causal_conv(xpb_ref, bufb_ref, stb_ref, bqkv_ref[...], wb_ref, B_CONV))
    seg = seg_ref[...].astype(BF16)

    def l2norm(x):
        sq = x * x
        hi = sq.astype(BF16)
        lo = (sq - hi.astype(F32)).astype(BF16)
        ss = jnp.dot(hi, seg, preferred_element_type=F32) + jnp.dot(lo, seg, preferred_element_type=F32)
        return x * lax.rsqrt(ss + 1e-6)

    q_ref[...] = l2norm(qkv[:, :B_WIDTH]) * (HEAD ** -0.5)
    k_ref[...] = l2norm(qkv[:, B_WIDTH:2 * B_WIDTH])
    v_ref[...] = qkv[:, 2 * B_WIDTH:]

    ab = bab_ref[...]
    z = ab + dtb_ref[...]
    softplus = jnp.maximum(z, 0.0) + jnp.log1p(jnp.exp(-jnp.abs(z)))
    g_ref[...] = -jnp.exp(alog_ref[...]) * softplus
    beta_ref[...] = jax.nn.sigmoid(pltpu.roll(ab, LANE - B_HEADS, 1))


def _conv(ua, bqkv, bab, buf_a, buf_b, conv_a_w, conv_b_w, a_log, dt_bias, seg, *, groups, stride, tt):
    m = ua.shape[0]
    nt = m // (groups * tt)
    halo_a, halo_b = (A_CONV - 1) * stride, (B_CONV - 1) * stride
    row = lambda w: pl.BlockSpec((tt, w), lambda g, t: (g * nt + t, 0))
    full = lambda a: pl.BlockSpec(a.shape, lambda g, t: (0,) * a.ndim)
    per_group = lambda h, w: pl.BlockSpec((1, h, w), lambda g, t: (g, 0, 0))
    alog = jnp.zeros((1, LANE), F32).at[0, :B_HEADS].set(a_log)
    dtb = jnp.zeros((1, LANE), F32).at[0, :B_HEADS].set(dt_bias)
    out_w = (A_WIDTH, B_WIDTH, B_WIDTH, B_WIDTH, LANE, LANE)
    return pl.pallas_call(
        functools.partial(_conv_kernel, stride=stride, tt=tt),
        grid=(groups, nt),
        in_specs=[row(1024), row(B_QKV), row(LANE), per_group(halo_a, A_WIDTH), per_group(halo_b, B_QKV),
                  full(conv_a_w), full(conv_b_w), full(alog), full(dtb), full(seg)],
        out_specs=[row(w) for w in out_w] + [per_group(halo_a, A_WIDTH), per_group(halo_b, B_QKV)],
        out_shape=[jax.ShapeDtypeStruct((m, w), F32) for w in out_w]
        + [jax.ShapeDtypeStruct((groups, halo_a, A_WIDTH), F32),
           jax.ShapeDtypeStruct((groups, halo_b, B_QKV), F32)],
        scratch_shapes=[pltpu.VMEM((_round_up(halo_a, SUBLANE) + tt, A_WIDTH), F32),
                        pltpu.VMEM((_round_up(halo_b, SUBLANE) + tt, B_QKV), F32)],
        compiler_params=_params(2),
        name="conv",
    )(ua, bqkv, bab, buf_a, buf_b, conv_a_w, conv_b_w, alog, dtb, seg)


def _gdn_chunk_kernel(q_ref, k_ref, v_ref, g_ref, beta_ref, tri_ref,
                      mt_ref, nn_ref, qe_ref, au_ref, gc_ref, *, chunks):
    n = GDN_CHUNK
    ri = lax.broadcasted_iota(jnp.int32, (n, n), 0)
    ci = lax.broadcasted_iota(jnp.int32, (n, n), 1)
    tri = ri >= ci
    strict = ri > ci
    eye = jnp.where(ri == ci, 1.0, 0.0).astype(F32)
    pairs = []
    for c in range(chunks):
        rows = slice(c * n, (c + 1) * n)
        q = q_ref[rows, :]
        k = k_ref[rows, :]
        v = v_ref[rows, :]
        beta = beta_ref[rows, :]
        gc = jnp.dot(tri_ref[...], g_ref[rows, :], preferred_element_type=F32, precision=HIGHEST)
        gc_ref[rows, :] = gc
        gct = gc.T
        for h in range(B_HEADS):
            lo, hi = h * HEAD, (h + 1) * HEAD
            qh, kh, vh = q[:, lo:hi], k[:, lo:hi], v[:, lo:hi]
            gcol = gc[:, h:h + 1]
            grow = gct[h:h + 1, :]
            bcol = beta[:, h:h + 1]
            decay = jnp.where(tri, jnp.exp(jnp.where(tri, gcol - grow, 0.0)), 0.0)
            pairs.append(dict(qh=qh, kh=kh, vh=vh, gcol=gcol, bcol=bcol, decay=decay, kb=kh * bcol))
    for pr in pairs:
        pr["kk"] = _dot_nt(pr["kb"], pr["kh"])
        pr["qk"] = _dot_nt(pr["qh"], pr["kh"])
    for pr in pairs:
        lmat = jnp.where(strict, pr["kk"] * pr["decay"], 0.0)
        pr["x"] = eye - lmat
        pr["p"] = lmat
    for _ in range(5):
        for pr in pairs:
            pr["p"] = jnp.dot(pr["p"], pr["p"], preferred_element_type=F32)
        for pr in pairs:
            pr["x"] = pr["x"] + jnp.dot(pr["x"], pr["p"], preferred_element_type=F32)
    for pr in pairs:
        pr["egc"] = jnp.exp(pr["gcol"])
        rhs = jnp.concatenate([pr["vh"] * pr["bcol"], pr["kb"] * pr["egc"]], axis=1)
        pr["uw"] = jnp.dot(pr["x"], rhs, preferred_element_type=F32)
    for pr in pairs:
        gcol = pr["gcol"]
        kd = pr["kh"] * jnp.exp(gcol[n - 1:n, :] - gcol)
        pr["auw"] = jnp.dot(pr["qk"] * pr["decay"], pr["uw"], preferred_element_type=F32)
        pr["kuw"] = _dot_tn(kd, pr["uw"])
    for c in range(chunks):
        rows = slice(c * n, (c + 1) * n)
        sel = pairs[c * B_HEADS:(c + 1) * B_HEADS]
        mt_ref[rows, :] = jnp.concatenate([-pr["kuw"][:, HEAD:] for pr in sel], axis=1)
        nn_ref[rows, :] = jnp.concatenate([pr["kuw"][:, :HEAD] for pr in sel], axis=1)
        qe_ref[rows, :] = jnp.concatenate([pr["qh"] * pr["egc"] - pr["auw"][:, HEAD:] for pr in sel], axis=1)
        au_ref[rows, :] = jnp.concatenate([pr["auw"][:, :HEAD] for pr in sel], axis=1)


def _gdn_scan_kernel(mt_ref, nn_ref, qe_ref, au_ref, gc_ref, z_ref, nw_ref, y_ref, sfin_ref, s_ref, *, batch):
    c = pl.program_id(0)
    n = GDN_CHUNK

    @pl.when(c == 0)
    def _():
        s_ref[...] = jnp.zeros_like(s_ref)

    nw = nw_ref[...]
    pairs = []
    for b in range(batch):
        mt, qe = mt_ref[b], qe_ref[b]
        for h in range(B_HEADS):
            lo, hi = h * HEAD, (h + 1) * HEAD
            pairs.append(dict(b=b, h=h, lhs=jnp.concatenate([mt[:, lo:hi], qe[:, lo:hi]], axis=0), s=s_ref[b, h]))
    for pr in pairs:
        pr["r"] = jnp.dot(pr["lhs"], pr["s"], preferred_element_type=F32)
    for b in range(batch):
        nn, au = nn_ref[b], au_ref[b]
        eg = jnp.exp(gc_ref[b, n - 1:n, :])
        outs = []
        for pr in pairs[b * B_HEADS:(b + 1) * B_HEADS]:
            h = pr["h"]
            lo, hi = h * HEAD, (h + 1) * HEAD
            s_new = pr["s"] * eg[:, h:h + 1] + pr["r"][:n] + nn[:, lo:hi]
            s_ref[b, h] = s_new
            sfin_ref[b, h] = s_new
            o = pr["r"][n:] + au[:, lo:hi]
            ms = jnp.mean(o * o, axis=-1, keepdims=True)
            outs.append(o * lax.rsqrt(ms + EPS) * nw)
        y_ref[b] = jnp.concatenate(outs, axis=1) * _silu(z_ref[b])


def _gdn_prompt(q, k, v, g, beta, z, norm_w, batch, chunks_per_step=4):
    m = q.shape[0]
    t = m // batch
    n = GDN_CHUNK
    nc = t // n
    rows = chunks_per_step * n
    row = lambda w: pl.BlockSpec((rows, w), lambda i: (i, 0))
    tri = jnp.tril(jnp.ones((n, n), F32))
    wide = jax.ShapeDtypeStruct((m, B_WIDTH), F32)
    mt, nn, qe, au, gc = pl.pallas_call(
        functools.partial(_gdn_chunk_kernel, chunks=chunks_per_step),
        grid=(m // rows,),
        in_specs=[row(B_WIDTH), row(B_WIDTH), row(B_WIDTH), row(LANE), row(LANE),
                  pl.BlockSpec((n, n), lambda i: (0, 0))],
        out_specs=[row(B_WIDTH)] * 4 + [row(LANE)],
        out_shape=[wide] * 4 + [jax.ShapeDtypeStruct((m, LANE), F32)],
        compiler_params=_params(1),
        name="gdn_chunk",
    )(q, k, v, g, beta, tri)
    per_chunk = lambda w: pl.BlockSpec((batch, n, w), lambda c: (0, c, 0))
    state = pl.BlockSpec((batch, B_HEADS, HEAD, HEAD), lambda c: (0, 0, 0, 0))
    view = lambda a: a.reshape(batch, t, a.shape[1])
    y, s_fin = pl.pallas_call(
        functools.partial(_gdn_scan_kernel, batch=batch),
        grid=(nc,),
        in_specs=[per_chunk(B_WIDTH)] * 4 + [per_chunk(LANE), per_chunk(B_WIDTH),
                                             pl.BlockSpec((1, HEAD), lambda c: (0, 0))],
        out_specs=[per_chunk(B_WIDTH), state],
        out_shape=[jax.ShapeDtypeStruct((batch, t, B_WIDTH), F32),
                   jax.ShapeDtypeStruct((batch, B_HEADS, HEAD, HEAD), F32)],
        scratch_shapes=[pltpu.VMEM((batch, B_HEADS, HEAD, HEAD), F32)],
        compiler_params=_params(1),
        name="gdn_scan",
    )(view(mt), view(nn), view(qe), view(au), view(gc), view(z), norm_w.reshape(1, HEAD))
    return y.reshape(m, B_WIDTH), s_fin


def _gdn_sample_kernel(q_ref, k_ref, v_ref, g_ref, beta_ref, z_ref, nw_ref, s0_ref, y_ref, s_ref, *, steps):
    h = pl.program_id(0)
    bsz = s_ref.shape[-1]
    zero = jnp.zeros((HEAD, bsz), F32)

    def bcast_row(ref, i, dk):
        return jnp.broadcast_to(ref[i, pl.ds(dk, 1), :], (HEAD, bsz))

    def head_row(ref, i):
        return ref[i, pl.ds(h, 1), :]

    def finish(i, o):
        ms = jnp.mean(o * o, axis=0, keepdims=True)
        y_ref[i] = o * lax.rsqrt(ms + EPS) * nw_ref[...] * _silu(z_ref[i])

    gam = jnp.exp(head_row(g_ref, 0))

    def first(dk, ks):
        s = s0_ref[0, dk] * gam
        s_ref[0, dk] = s
        return ks + bcast_row(k_ref, 0, dk) * s

    ks = lax.fori_loop(0, HEAD, first, zero)
    vn = head_row(beta_ref, 0) * (v_ref[0] - ks)

    for i in range(1, steps):
        gam = jnp.exp(head_row(g_ref, i))

        def mid(dk, carry, i=i, vn=vn, gam=gam):
            o, ks = carry
            s = s_ref[0, dk] + bcast_row(k_ref, i - 1, dk) * vn
            o = o + bcast_row(q_ref, i - 1, dk) * s
            s = s * gam
            s_ref[0, dk] = s
            return o, ks + bcast_row(k_ref, i, dk) * s

        o, ks = lax.fori_loop(0, HEAD, mid, (zero, zero))
        finish(i - 1, o)
        vn = head_row(beta_ref, i) * (v_ref[i] - ks)

    def last(dk, o):
        s = s_ref[0, dk] + bcast_row(k_ref, steps - 1, dk) * vn
        s_ref[0, dk] = s
        return o + bcast_row(q_ref, steps - 1, dk) * s

    finish(steps - 1, lax.fori_loop(0, HEAD, last, zero))


def _gdn_sample(qt, kt, vt, gt, bt, zt, nwb, s0t):
    steps, _, bsz = qt.shape
    per_head = pl.BlockSpec((steps, HEAD, bsz), lambda h: (0, h, 0))
    small = pl.BlockSpec((steps, SUBLANE, bsz), lambda h: (0, 0, 0))
    state = pl.BlockSpec((1, HEAD, HEAD, bsz), lambda h: (h, 0, 0, 0))
    return pl.pallas_call(
        functools.partial(_gdn_sample_kernel, steps=steps),
        grid=(B_HEADS,),
        in_specs=[per_head, per_head, per_head, small, small, per_head,
                  pl.BlockSpec((HEAD, bsz), lambda h: (0, 0)), state],
        out_specs=[per_head, state],
        out_shape=[jax.ShapeDtypeStruct((steps, B_WIDTH, bsz), F32),
                   jax.ShapeDtypeStruct((B_HEADS, HEAD, HEAD, bsz), F32)],
        compiler_params=_params(1),
        name="gdn_sample",
    )(qt, kt, vt, gt, bt, zt, nwb, s0t)


def _attn_prompt_kernel(q_ref, k_ref, v_ref, o_ref, lse_ref, *, dil, nb, chains_per_stage):
    n = ATTN_BLOCK
    scale = HEAD ** -0.5
    lane = lax.broadcasted_iota(jnp.int32, (n, GROUP_W), 1)
    head0 = lane < HEAD
    qi = lax.broadcasted_iota(jnp.int32, (n, n), 0)
    kj = lax.broadcasted_iota(jnp.int32, (n, n), 1)
    mask_c = kj <= qi
    mask_p = kj >= qi

    def rows(r, i):
        if dil == 1:
            return pl.ds(i * n, n)
        return pl.ds(r + i * n * dil, n, stride=dil)

    blocks = [(r, i) for r in range(dil) for i in range(nb)]
    for first in range(0, len(blocks), chains_per_stage):
        stage = []
        for r, i in blocks[first:first + chains_per_stage]:
            q = q_ref[rows(r, i), :]
            for sel in (head0, jnp.logical_not(head0)):
                stage.append(dict(r=r, i=i, qh=jnp.where(sel, q, 0.0)))
        for ch in stage:
            r, i = ch["r"], ch["i"]
            ch["sc"] = _dot_nt(ch["qh"], k_ref[rows(r, i), :])
            if i > 0:
                ch["sp"] = _dot_nt(ch["qh"], k_ref[rows(r, i - 1), :])
        for ch in stage:
            sc = jnp.where(mask_c, ch["sc"] * scale, NEG)
            mx = jnp.max(sc, axis=-1, keepdims=True)
            if ch["i"] > 0:
                sp = jnp.where(mask_p, ch["sp"] * scale, NEG)
                mx = jnp.maximum(mx, jnp.max(sp, axis=-1, keepdims=True))
                ch["pp"] = jnp.exp(sp - mx)
            ch["pc"] = jnp.exp(sc - mx)
            den = jnp.sum(ch["pc"], axis=-1, keepdims=True)
            if ch["i"] > 0:
                den = den + jnp.sum(ch["pp"], axis=-1, keepdims=True)
            ch["den"] = den
            ch["lse"] = mx + jnp.log(den)
        for ch in stage:
            r, i = ch["r"], ch["i"]
            acc = jnp.dot(ch["pc"], v_ref[rows(r, i), :], preferred_element_type=F32)
            if i > 0:
                acc = acc + jnp.dot(ch["pp"], v_ref[rows(r, i - 1), :], preferred_element_type=F32)
            ch["o"] = acc / ch["den"]
        for c0, c1 in zip(stage[0::2], stage[1::2]):
            r, i = c0["r"], c0["i"]
            o_ref[rows(r, i), :] = jnp.where(head0, c0["o"], c1["o"])
            lse_ref[rows(r, i), :] = jnp.where(head0, jnp.broadcast_to(c0["lse"], (n, GROUP_W)),
                                               jnp.broadcast_to(c1["lse"], (n, GROUP_W)))


def _attn_prompt(qrot, kv, group, batch, chains_per_stage=4):
    _, dil = GROUPS[group]
    m = qrot.shape[0]
    t = m // batch
    nb = t // (dil * ATTN_BLOCK)
    seq = lambda col: pl.BlockSpec((t, GROUP_W), lambda b: (b, col))
    out = jax.ShapeDtypeStruct((m, GROUP_W), F32)
    return pl.pallas_call(
        functools.partial(_attn_prompt_kernel, dil=dil, nb=nb, chains_per_stage=chains_per_stage),
        grid=(batch,),
        in_specs=[seq(group), seq(0), seq(1)],
        out_specs=[seq(0), seq(0)],
        out_shape=[out, out],
        compiler_params=_params(1),
        name=f"attn_prompt_g{group}",
    )(qrot, kv, kv)


def _attn_sample_kernel(q_ref, n0_ref, n1_ref, n2_ref, c0_ref, c1_ref, c2_ref, o_ref, lse_ref, *, bt, steps):
    rows = 2 * steps
    scale = HEAD ** -0.5
    lane = lax.broadcasted_iota(jnp.int32, (rows, GROUP_W), 1)
    rix = lax.broadcasted_iota(jnp.int32, (rows, GROUP_W), 0)
    own_head = (lane < HEAD) == (rix < steps)
    nkey = lax.broadcasted_iota(jnp.int32, (rows, rows), 1)
    nqry = lax.broadcasted_iota(jnp.int32, (rows, rows), 0) % steps
    new_refs = (n0_ref, n1_ref, n2_ref)
    cache_refs = (c0_ref, c1_ref, c2_ref)

    head0 = lane[:steps] < HEAD
    chains = []
    for b in range(bt):
        q_all = q_ref[b]
        for g, (win, dil) in enumerate(GROUPS):
            qm = jnp.where(own_head, q_all[:, g * GROUP_W:(g + 1) * GROUP_W], 0.0)
            chains.append(dict(b=b, g=g, win=win, dil=dil, qm=qm, new=new_refs[g][b]))
    for ch in chains:
        kt = cache_refs[ch["g"]][0, ch["b"], 0].reshape(GROUP_W, ch["win"])
        ch["s_buf"] = jnp.dot(ch["qm"], kt, preferred_element_type=F32)
        ch["s_new"] = _dot_nt(ch["qm"], ch["new"][:, :GROUP_W])
    for ch in chains:
        win, dil = ch["win"], ch["dil"]
        pos = lax.broadcasted_iota(jnp.int32, (rows, win), 1)
        qry = lax.broadcasted_iota(jnp.int32, (rows, win), 0) % steps
        if dil == 1:
            ok = pos >= qry
            new_ok = jnp.logical_and(nkey < steps, nkey <= nqry)
        else:
            ok = (pos % dil) == qry
            new_ok = nkey == nqry
        s_buf = jnp.where(ok, ch["s_buf"] * scale, NEG)
        s_new = jnp.where(new_ok, ch["s_new"] * scale, NEG)
        mx = jnp.maximum(jnp.max(s_buf, axis=-1, keepdims=True), jnp.max(s_new, axis=-1, keepdims=True))
        ch["p_buf"] = jnp.exp(s_buf - mx)
        ch["p_new"] = jnp.exp(s_new - mx)
        ch["den"] = jnp.sum(ch["p_buf"], axis=-1, keepdims=True) + jnp.sum(ch["p_new"], axis=-1, keepdims=True)
        ch["mx"] = mx
    for ch in chains:
        vt = cache_refs[ch["g"]][0, ch["b"], 1].reshape(GROUP_W, ch["win"])
        ch["acc"] = jnp.dot(ch["p_new"], ch["new"][:, GROUP_W:], preferred_element_type=F32) + _dot_nt(ch["p_buf"], vt)
    for ch in chains:
        b, g = ch["b"], ch["g"]
        o_full = ch["acc"] / ch["den"]
        lse_full = jnp.broadcast_to(ch["mx"] + jnp.log(ch["den"]), (rows, GROUP_W))
        o_ref[b, :, g * GROUP_W:(g + 1) * GROUP_W] = jnp.where(head0, o_full[:steps], o_full[steps:])
        lse_ref[b, :, g * GROUP_W:(g + 1) * GROUP_W] = jnp.where(head0, lse_full[:steps], lse_full[steps:])


def _attn_sample(q2, new_kv, caches, layer, bt):
    bsz, rows, _ = q2.shape
    steps = rows // 2
    per_b = lambda w: pl.BlockSpec((bt, rows, w), lambda i: (i, 0, 0))
    cache_spec = lambda win: pl.BlockSpec((1, bt, 2, HPG, HEAD, win), lambda i: (layer, i, 0, 0, 0, 0))
    out_spec = pl.BlockSpec((bt, steps, C_WIDTH), lambda i: (i, 0, 0))
    out = jax.ShapeDtypeStruct((bsz, steps, C_WIDTH), F32)
    return pl.pallas_call(
        functools.partial(_attn_sample_kernel, bt=bt, steps=steps),
        grid=(bsz // bt,),
        in_specs=[per_b(C_WIDTH), per_b(2 * GROUP_W), per_b(2 * GROUP_W), per_b(2 * GROUP_W)]
        + [cache_spec(win) for win, _ in GROUPS],
        out_specs=[out_spec, out_spec],
        out_shape=[out, out],
        compiler_params=_params(1),
        name="attn_sample",
    )(q2, *new_kv, *caches)


def _outproj_kernel(ya_ref, yb_ref, o0_ref, o1_ref, o2_ref, l0_ref, l1_ref, l2_ref, cz_ref, x_ref, gate_ref,
                    w_ref, xo_ref):
    lses = (l0_ref[...], l1_ref[...], l2_ref[...])
    mx = jnp.maximum(jnp.maximum(lses[0], lses[1]), lses[2])
    es = [jnp.exp(l - mx) for l in lses]
    tot = es[0] + es[1] + es[2]
    cz = cz_ref[...]

    def mm(y, lo):
        return jnp.dot(y.astype(BF16), w_ref[lo:lo + y.shape[1], :], preferred_element_type=F32)

    out = mm(ya_ref[...], 0) + mm(yb_ref[...], A_WIDTH)
    for g, o_ref in enumerate((o0_ref, o1_ref, o2_ref)):
        yc = o_ref[...] * (es[g] / tot) * _silu(cz[:, g * GROUP_W:(g + 1) * GROUP_W])
        out = out + mm(yc, A_WIDTH + B_WIDTH + g * GROUP_W)
    xo_ref[...] = x_ref[...] + (1.0 + gate_ref[0]) * out


def _outproj(ya, yb, os_, lses, cz, x_rows, mod, w_out, tm):
    m = x_rows.shape[0]
    nblk = m // tm
    nb_mod, r, _ = mod.shape
    tiles_per_mod = nblk // nb_mod
    row = lambda w: pl.BlockSpec((tm, w), lambda i: (i, 0))
    return pl.pallas_call(
        _outproj_kernel,
        grid=(nblk,),
        in_specs=[row(A_WIDTH), row(B_WIDTH)] + [row(GROUP_W)] * 6 + [row(C_WIDTH), row(D_MODEL),
                  pl.BlockSpec((1, r, D_MODEL), lambda i: (i // tiles_per_mod, 0, 2)),
                  pl.BlockSpec((D_MODEL, D_MODEL), lambda i: (0, 0))],
        out_specs=row(D_MODEL),
        out_shape=jax.ShapeDtypeStruct((m, D_MODEL), F32),
        compiler_params=_params(1),
        name="outproj",
    )(ya, yb, *os_, *lses, cz, x_rows, mod, w_out)


def _final_kernel(x_ref, sh_ref, sc_ref, nw_ref, y_ref):
    x = x_ref[...]
    var = jnp.mean(x * x, axis=-1, keepdims=True)
    y_ref[...] = x * lax.rsqrt(var + EPS) * nw_ref[...] * (1.0 + sc_ref[0]) + sh_ref[0]


def _final(x_rows, mod, norm_w, tm):
    m = x_rows.shape[0]
    nblk = m // tm
    nb_mod, r, _ = mod.shape
    tiles_per_mod = nblk // nb_mod
    row = pl.BlockSpec((tm, D_MODEL), lambda i: (i, 0))
    return pl.pallas_call(
        _final_kernel,
        grid=(nblk,),
        in_specs=[row,
                  pl.BlockSpec((1, r, D_MODEL), lambda i: (i // tiles_per_mod, 0, 0)),
                  pl.BlockSpec((1, r, D_MODEL), lambda i: (i // tiles_per_mod, 0, 1)),
                  pl.BlockSpec((1, D_MODEL), lambda i: (0, 0))],
        out_specs=row,
        out_shape=jax.ShapeDtypeStruct((m, D_MODEL), F32),
        compiler_params=_params(1),
        name="final_norm",
    )(x_rows, mod, mod, norm_w.reshape(1, D_MODEL))


def _rope_tables(pos):
    half = ROT_DIM // 2
    inv_freq = ROPE_THETA ** (-jnp.arange(half, dtype=F32) * (2.0 / ROT_DIM))
    ang = pos.astype(F32)[:, None] * inv_freq[None, :]
    cos, sin = jnp.cos(ang), jnp.sin(ang)
    rows = pos.shape[0]
    one = jnp.ones((rows, HEAD - ROT_DIM), F32)
    zero_r = jnp.zeros((rows, HEAD - ROT_DIM), F32)
    zero_h = jnp.zeros((rows, half), F32)
    cos_h = jnp.concatenate([cos, cos, one], axis=1)
    sa_h = jnp.concatenate([-sin, zero_h, zero_r], axis=1)
    sb_h = jnp.concatenate([zero_h, sin, zero_r], axis=1)
    return tuple(jnp.concatenate([t] * HPG, axis=1) for t in (cos_h, sa_h, sb_h))


def _permute_w_in(w_in):
    depth = w_in.shape[0]
    pad = jnp.zeros((depth, D_MODEL, IN_PERM_WIDTH - w_in.shape[2]), w_in.dtype)
    return jnp.concatenate([w_in[:, :, :SRC_AB[0]], w_in[:, :, SRC_AB[1]:], w_in[:, :, SRC_AB[0]:SRC_AB[1]], pad],
                           axis=2).astype(BF16)


def _segment_ones():
    idx = jnp.arange(B_WIDTH) // HEAD
    return (idx[:, None] == idx[None, :]).astype(F32)


def _prompt_trunk(x, mods, mod_final, wts, seg):
    batch, t, _ = x.shape
    depth = wts["w_in"].shape[0]
    m = batch * t
    tm = 256
    rows = x.reshape(m, D_MODEL)
    tabs = _rope_tables(jnp.arange(t))
    zeros_a = jnp.zeros((batch, A_CONV - 1, A_WIDTH), F32)
    zeros_b = jnp.zeros((batch, B_CONV - 1, B_QKV), F32)
    st_a, st_b, st_g, kvs = [], [], [], [[] for _ in GROUPS]
    for l in range(depth):
        mod = mods[l].reshape(batch, 1, 3 * D_MODEL)
        ua, bqkv, bz, bab, qrot, kv0, kv1, kv2, cz = _inproj(rows, mod, wts["norm_w"][l], wts["w_in"][l], tabs, tm)
        ya, q, k, v, g, beta, sa, sb = _conv(ua, bqkv, bab, zeros_a, zeros_b, wts["conv_a_w"][l],
                                             wts["conv_b_w"][l], wts["a_log"][l], wts["dt_bias"][l], seg,
                                             groups=batch, stride=1, tt=tm)
        yb, s_fin = _gdn_prompt(q, k, v, g, beta, bz, wts["gdn_norm_w"][l], batch)
        os_, lses = [], []
        for gi, kv in enumerate((kv0, kv1, kv2)):
            o, lse = _attn_prompt(qrot, kv, gi, batch)
            os_.append(o)
            lses.append(lse)
            win = min(GROUPS[gi][0], t)
            kvs[gi].append(kv.reshape(batch, t, 2, HPG, HEAD)[:, t - win:])
        rows = _outproj(ya, yb, os_, lses, cz, rows, mod, wts["w_out"][l], tm)
        st_a.append(sa)
        st_b.append(sb)
        st_g.append(s_fin)
    y = _final(rows, mod_final.reshape(batch, 1, 2 * D_MODEL), wts["final_norm_w"], tm).reshape(batch, t, D_MODEL)
    return y, jnp.stack(st_a), jnp.stack(st_b), jnp.stack(st_g), [jnp.stack(r) for r in kvs]


def _sample_trunk(x, mods, mod_final, state_a, state_b, state_g, caches, past_len, wts, seg):
    bsz, steps, _ = x.shape
    depth = wts["w_in"].shape[0]
    m = bsz * steps
    tm = bsz

    def time_major(a):
        return a.transpose(1, 0, 2).reshape(a.shape[1] * bsz, a.shape[2])

    def batch_major(a, n):
        return a.reshape(n, bsz, a.shape[1]).transpose(1, 0, 2)

    def lanes_batch(a, n):
        return a.reshape(n, bsz, a.shape[1]).transpose(0, 2, 1)

    rows = time_major(x)
    tabs = _rope_tables(jnp.repeat(past_len + jnp.arange(steps), bsz))
    cache_views = [c.transpose(0, 1, 3, 4, 5, 2) for c in caches]
    st_a, st_b, st_g, kvs = [], [], [], [[] for _ in GROUPS]
    for l in range(depth):
        mod = mods[l][None]
        ua, bqkv, bz, bab, qrot, kv0, kv1, kv2, cz = _inproj(rows, mod, wts["norm_w"][l], wts["w_in"][l], tabs, tm)
        buf_a = time_major(state_a[l])[None]
        buf_b = time_major(state_b[l])[None]
        ya, q, k, v, g, beta, sa, sb = _conv(ua, bqkv, bab, buf_a, buf_b, wts["conv_a_w"][l],
                                             wts["conv_b_w"][l], wts["a_log"][l], wts["dt_bias"][l], seg,
                                             groups=1, stride=bsz, tt=m)
        nwb = jnp.broadcast_to(wts["gdn_norm_w"][l][:, None], (HEAD, bsz))
        yt, s_t = _gdn_sample(lanes_batch(q, steps), lanes_batch(k, steps), lanes_batch(v, steps),
                              lanes_batch(g[:, :SUBLANE], steps), lanes_batch(beta[:, :SUBLANE], steps),
                              lanes_batch(bz, steps), nwb, state_g[l].transpose(1, 2, 3, 0))
        yb = yt.transpose(0, 2, 1).reshape(m, B_WIDTH)
        dup = lambda a: jnp.concatenate([batch_major(a, steps)] * 2, axis=1)
        o_c, lse_c = _attn_sample(dup(qrot), [dup(kv) for kv in (kv0, kv1, kv2)], cache_views, l, 4)
        o_c, lse_c = time_major(o_c), time_major(lse_c)
        os_ = [o_c[:, gi * GROUP_W:(gi + 1) * GROUP_W] for gi in range(len(GROUPS))]
        lses = [lse_c[:, gi * GROUP_W:(gi + 1) * GROUP_W] for gi in range(len(GROUPS))]
        rows = _outproj(ya, yb, os_, lses, cz, rows, mod, wts["w_out"][l], tm)
        st_a.append(batch_major(sa[0], A_CONV - 1))
        st_b.append(batch_major(sb[0], B_CONV - 1))
        st_g.append(s_t.transpose(3, 0, 1, 2))
        for gi, kv in enumerate((kv0, kv1, kv2)):
            kvs[gi].append(batch_major(kv, steps).reshape(bsz, steps, 2, HPG, HEAD))
    y = batch_major(_final(rows, mod_final[None], wts["final_norm_w"], tm), steps)
    return y, jnp.stack(st_a), jnp.stack(st_b), jnp.stack(st_g), [jnp.stack(r) for r in kvs]


def kernel(x_prompt, x_sample, state_conv_a, state_conv_b, state_gdn, cache_kv_w128, cache_kv_w512,
           cache_kv_w2048, c_prompt, c_sample, w_in, w_out, w_ada, b_ada, norm_w, conv_a_w, conv_b_w,
           a_log, dt_bias, gdn_norm_w, final_norm_w, w_ada_final, b_ada_final):
    n_prompt = c_prompt.shape[0]
    c_all = jnp.concatenate([c_prompt, c_sample], axis=0)
    mods = _ada(c_all, w_ada, b_ada)
    mod_final = _ada(c_all, w_ada_final[None], b_ada_final[None])[0]
    wts = dict(w_in=_permute_w_in(w_in), w_out=w_out.astype(BF16), norm_w=norm_w, conv_a_w=conv_a_w,
               conv_b_w=conv_b_w, a_log=a_log, dt_bias=dt_bias, gdn_norm_w=gdn_norm_w,
               final_norm_w=final_norm_w)
    seg = _segment_ones()
    y_p, ca_p, cb_p, g_p, kv_p = _prompt_trunk(x_prompt, mods[:, :n_prompt], mod_final[:n_prompt], wts, seg)
    y_s, ca_s, cb_s, g_s, kv_s = _sample_trunk(
        x_sample, mods[:, n_prompt:], mod_final[n_prompt:], state_conv_a, state_conv_b, state_gdn,
        (cache_kv_w128, cache_kv_w512, cache_kv_w2048), PAST_LEN, wts, seg)
    return (y_p, y_s, ca_p, ca_s, cb_p, cb_s, g_p, g_s,
            kv_p[0], kv_s[0], kv_p[1], kv_s[1], kv_p[2], kv_s[2])
```

```python
import functools
import math

import jax
import jax.numpy as jnp
from jax import lax
from jax.experimental import pallas as pl
from jax.experimental.pallas import tpu as pltpu

F32 = jnp.float32
BF16 = jnp.bfloat16
HIGHEST = lax.Precision.HIGHEST

D_MODEL = 1024
HEAD = 64
A_WIDTH = 256
A_CONV = 3
B_HEADS = 6
B_WIDTH = B_HEADS * HEAD
B_QKV = 3 * B_WIDTH
B_CONV = 4
GDN_CHUNK = 64
GROUPS = ((128, 1), (512, 4), (2048, 16))
HPG = 2
GROUP_W = HPG * HEAD
C_WIDTH = len(GROUPS) * GROUP_W
ATTN_BLOCK = 128
ROT_DIM = HEAD // 4
ROPE_THETA = 500000.0
PAST_LEN = 2048
EPS = 1e-6
NEG = -1e30
LANE = 128
SUBLANE = 8

COL_A = (0, 1024)
COL_BQKV = (1024, 2176)
COL_BZ = (2176, 2560)
COL_CQ = (2560, 2944)
COL_CK = (2944, 3328)
COL_CV = (3328, 3712)
COL_CZ = (3712, 4096)
COL_AB = (4096, 4224)
IN_PERM_WIDTH = 4224
SRC_AB = (2560, 2572)

VMEM_LIMIT_BYTES = 56 * 1024 * 1024


def _params(n_axes):
    return pltpu.CompilerParams(dimension_semantics=("arbitrary",) * n_axes,
                                vmem_limit_bytes=VMEM_LIMIT_BYTES)


def _round_up(x, m):
    return -(-x // m) * m


def _silu(x):
    return x * jax.nn.sigmoid(x)


def _dot_nt(a, b, **kw):
    return lax.dot_general(a, b, (((1,), (1,)), ((), ())), preferred_element_type=F32, **kw)


def _dot_tn(a, b, **kw):
    return lax.dot_general(a, b, (((0,), (0,)), ((), ())), preferred_element_type=F32, **kw)


def _ada_kernel(c_ref, w_ref, b_ref, o_ref):
    c = c_ref[...].astype(BF16)
    w = w_ref[0].astype(BF16)
    o_ref[0] = jnp.dot(c, w, preferred_element_type=F32) + b_ref[0]


def _ada(c_all, w, b):
    n_layers, _, n = w.shape
    r = c_all.shape[0]
    tn = 1024
    return pl.pallas_call(
        _ada_kernel,
        grid=(n_layers, n // tn),
        in_specs=[pl.BlockSpec((r, D_MODEL), lambda l, j: (0, 0)),
                  pl.BlockSpec((1, D_MODEL, tn), lambda l, j: (l, 0, j)),
                  pl.BlockSpec((1, 1, tn), lambda l, j: (l, 0, j))],
        out_specs=pl.BlockSpec((1, r, tn), lambda l, j: (l, 0, j)),
        out_shape=jax.ShapeDtypeStruct((n_layers, r, n), F32),
        compiler_params=_params(2),
        name="ada",
    )(c_all, w, b.reshape(n_layers, 1, n))


def _inproj_kernel(x_ref, sh_ref, sc_ref, nw_ref, w_ref, cos_ref, sa_ref, sb_ref,
                   bufa_ref, bufb_ref, wa_ref, wb_ref, alog_ref, dtb_ref, seg_ref,
                   ya_ref, q_ref, k_ref, v_ref, g_ref, beta_ref, bz_ref, qrot_ref, kv0_ref, kv1_ref, kv2_ref, cz_ref,
                   sta_ref, stb_ref, xpa_ref, xpb_ref, *, stride, tm, tiles_per_seq):
    t = pl.program_id(0) % tiles_per_seq
    x = x_ref[...]
    reps = tm // sh_ref.shape[1] if sh_ref.shape[1] > 1 else 1
    tile_rows = lambda a: a if reps == 1 else jnp.concatenate([a] * reps, axis=0)
    var = jnp.mean(x * x, axis=-1, keepdims=True)
    hn = x * lax.rsqrt(var + EPS) * nw_ref[...]
    hn = hn * (1.0 + tile_rows(sc_ref[0])) + tile_rows(sh_ref[0])
    hb = hn.astype(BF16)

    def mm(lo, hi):
        return jnp.dot(hb, w_ref[:, lo:hi], preferred_element_type=F32)

    def halo_rows(taps):
        halo = (taps - 1) * stride
        return halo, _round_up(halo, SUBLANE)

    def load_halo(xp_ref, buf_ref, taps):
        halo, x0 = halo_rows(taps)

        @pl.when(t == 0)
        def _():
            xp_ref[x0 - halo:x0, :] = buf_ref[0]

        @pl.when(t > 0)
        def _():
            xp_ref[x0 - halo:x0, :] = xp_ref[x0 + tm - halo:x0 + tm, :]

    def conv_cols(xp_ref, x_new, w_ref, taps, lo, hi):
        halo, x0 = halo_rows(taps)
        xp_ref[x0:x0 + tm, lo:hi] = x_new
        y = xp_ref[x0 - halo:x0 - halo + tm, lo:hi] * w_ref[0:1, lo:hi]
        for j in range(1, taps):
            r0 = x0 - halo + j * stride
            y = y + xp_ref[r0:r0 + tm, lo:hi] * w_ref[j:j + 1, lo:hi]
        return y

    def store_tail(xp_ref, st_ref, taps):
        halo, x0 = halo_rows(taps)
        st_ref[0] = xp_ref[x0 + tm - halo:x0 + tm, :]

    load_halo(xpa_ref, bufa_ref, A_CONV)
    load_halo(xpb_ref, bufb_ref, B_CONV)
    seg = seg_ref[...].astype(BF16)
    cos = cos_ref[...]
    sa = sa_ref[...]
    sb = sb_ref[...]

    def rope(u):
        return u * cos + pltpu.roll(u, LANE - ROT_DIM // 2, 1) * sa + pltpu.roll(u, ROT_DIM // 2, 1) * sb

    def branch_b(lo, hi):
        act = _silu(conv_cols(xpb_ref, mm(COL_BQKV[0] + lo, COL_BQKV[0] + hi), wb_ref, B_CONV, lo, hi))
        if lo < 2 * B_WIDTH:
            sq = act * act
            hi16 = sq.astype(BF16)
            lo16 = (sq - hi16.astype(F32)).astype(BF16)
            blk = seg[:hi - lo, :hi - lo]
            ss = jnp.dot(hi16, blk, preferred_element_type=F32) + jnp.dot(lo16, blk, preferred_element_type=F32)
            act = act * lax.rsqrt(ss + 1e-6)
        for out_ref, base, scale in ((q_ref, 0, HEAD ** -0.5), (k_ref, B_WIDTH, None), (v_ref, 2 * B_WIDTH, None)):
            c0, c1 = max(lo, base), min(hi, base + B_WIDTH)
            if c0 < c1:
                piece = act[:, c0 - lo:c1 - lo]
                out_ref[:, c0 - base:c1 - base] = piece if scale is None else piece * scale

    def branch_c(g):
        lo, hi = g * GROUP_W, (g + 1) * GROUP_W
        qrot_ref[:, lo:hi] = rope(mm(COL_CQ[0] + lo, COL_CQ[0] + hi))
        kv_ref = (kv0_ref, kv1_ref, kv2_ref)[g]
        kv_ref[:, :GROUP_W] = rope(mm(COL_CK[0] + lo, COL_CK[0] + hi))
        kv_ref[:, GROUP_W:] = mm(COL_CV[0] + lo, COL_CV[0] + hi)

    wide = 2 * LANE
    branch_b(0, wide)
    bz_ref[...] = mm(*COL_BZ)
    branch_b(wide, 2 * wide)
    cz_ref[...] = mm(*COL_CZ)
    branch_b(2 * wide, 3 * wide)
    branch_c(0)
    branch_b(3 * wide, 4 * wide)
    branch_c(1)
    branch_b(4 * wide, B_QKV)
    branch_c(2)
    store_tail(xpb_ref, stb_ref, B_CONV)

    a0 = COL_A[0]
    cx = mm(a0 + A_WIDTH, a0 + 2 * A_WIDTH) * mm(a0, a0 + A_WIDTH)
    conv_a = conv_cols(xpa_ref, cx, wa_ref, A_CONV, 0, A_WIDTH)
    store_tail(xpa_ref, sta_ref, A_CONV)
    ya_ref[...] = mm(a0 + 2 * A_WIDTH, a0 + 3 * A_WIDTH) * conv_a * _silu(mm(a0 + 3 * A_WIDTH, a0 + 4 * A_WIDTH))

    ab = mm(*COL_AB)
    z = ab + dtb_ref[...]
    softplus = jnp.maximum(z, 0.0) + jnp.log1p(jnp.exp(-jnp.abs(z)))
    g_ref[...] = -jnp.exp(alog_ref[...]) * softplus
    beta_ref[...] = jax.nn.sigmoid(pltpu.roll(ab, LANE - B_HEADS, 1))


def _inproj(x_rows, mod, buf_a, buf_b, rope_tabs, wts, layer, seg, *, seqs, stride, tm):
    m = x_rows.shape[0]
    nblk = m // tm
    tiles_per_seq = nblk // seqs
    nb_mod, r, _ = mod.shape
    tiles_per_mod = nblk // nb_mod
    nt_tab = rope_tabs[0].shape[0] // tm
    halo_a, halo_b = (A_CONV - 1) * stride, (B_CONV - 1) * stride
    row = lambda w: pl.BlockSpec((tm, w), lambda i: (i, 0))
    tab = pl.BlockSpec((tm, LANE), lambda i: (i % nt_tab, 0))
    full = lambda a: pl.BlockSpec(a.shape, lambda i: (0,) * a.ndim)
    per_layer = lambda a: pl.BlockSpec((None,) + a.shape[1:], lambda i: (layer,) + (0,) * (a.ndim - 1))
    per_seq = lambda h, w: pl.BlockSpec((1, h, w), lambda i: (i // tiles_per_seq, 0, 0))
    alog = jnp.zeros((1, LANE), F32).at[0, :B_HEADS].set(wts["a_log"][layer])
    dtb = jnp.zeros((1, LANE), F32).at[0, :B_HEADS].set(wts["dt_bias"][layer])
    out_w = (A_WIDTH, B_WIDTH, B_WIDTH, B_WIDTH, LANE, LANE, B_WIDTH, C_WIDTH, 2 * GROUP_W, 2 * GROUP_W, 2 * GROUP_W,
             C_WIDTH)
    return pl.pallas_call(
        functools.partial(_inproj_kernel, stride=stride, tm=tm, tiles_per_seq=tiles_per_seq),
        grid=(nblk,),
        in_specs=[row(D_MODEL),
                  pl.BlockSpec((1, r, D_MODEL), lambda i: (i // tiles_per_mod, 0, 0)),
                  pl.BlockSpec((1, r, D_MODEL), lambda i: (i // tiles_per_mod, 0, 1)),
                  per_layer(wts["norm_w"]), per_layer(wts["w_in"]), tab, tab, tab,
                  per_seq(halo_a, A_WIDTH), per_seq(halo_b, B_QKV),
                  per_layer(wts["conv_a_w"]), per_layer(wts["conv_b_w"]), full(alog), full(dtb), full(seg)],
        out_specs=[row(w) for w in out_w] + [per_seq(halo_a, A_WIDTH), per_seq(halo_b, B_QKV)],
        out_shape=[jax.ShapeDtypeStruct((m, w), F32) for w in out_w]
        + [jax.ShapeDtypeStruct((seqs, halo_a, A_WIDTH), F32), jax.ShapeDtypeStruct((seqs, halo_b, B_QKV), F32)],
        scratch_shapes=[pltpu.VMEM((_round_up(halo_a, SUBLANE) + tm, A_WIDTH), F32),
                        pltpu.VMEM((_round_up(halo_b, SUBLANE) + tm, B_QKV), F32)],
        compiler_params=_params(1),
        name="inproj",
    )(x_rows, mod, mod, wts["norm_w"], wts["w_in"], *rope_tabs, buf_a, buf_b,
      wts["conv_a_w"], wts["conv_b_w"], alog, dtb, seg)


def _gdn_chunk_kernel(q_ref, k_ref, v_ref, g_ref, beta_ref, tri_ref,
                      mt_ref, nn_ref, qe_ref, au_ref, gc_ref, *, chunks):
    n = GDN_CHUNK
    ri = lax.broadcasted_iota(jnp.int32, (n, n), 0)
    ci = lax.broadcasted_iota(jnp.int32, (n, n), 1)
    tri = ri >= ci
    strict = ri > ci
    eye = jnp.where(ri == ci, 1.0, 0.0).astype(F32)
    pairs = []
    for c in range(chunks):
        rows = slice(c * n, (c + 1) * n)
        q = q_ref[rows, :]
        k = k_ref[rows, :]
        v = v_ref[rows, :]
        beta = beta_ref[rows, :]
        gc = jnp.dot(tri_ref[...], g_ref[rows, :], preferred_element_type=F32, precision=HIGHEST)
        gc_ref[rows, :] = gc
        gct = gc.T
        for h in range(B_HEADS):
            lo, hi = h * HEAD, (h + 1) * HEAD
            qh, kh, vh = q[:, lo:hi], k[:, lo:hi], v[:, lo:hi]
            gcol = gc[:, h:h + 1]
            grow = gct[h:h + 1, :]
            bcol = beta[:, h:h + 1]
            decay = jnp.where(tri, jnp.exp(jnp.where(tri, gcol - grow, 0.0)), 0.0)
            pairs.append(dict(qh=qh, kh=kh, vh=vh, gcol=gcol, bcol=bcol, decay=decay, kb=kh * bcol))
    for pr in pairs:
        pr["kk"] = _dot_nt(pr["kb"], pr["kh"])
        pr["qk"] = _dot_nt(pr["qh"], pr["kh"])
    for pr in pairs:
        lmat = jnp.where(strict, pr["kk"] * pr["decay"], 0.0)
        pr["x"] = eye - lmat
        pr["p"] = lmat
    for _ in range(5):
        for pr in pairs:
            pr["p"] = jnp.dot(pr["p"], pr["p"], preferred_element_type=F32)
        for pr in pairs:
            pr["x"] = pr["x"] + jnp.dot(pr["x"], pr["p"], preferred_element_type=F32)
    for pr in pairs:
        pr["egc"] = jnp.exp(pr["gcol"])
        rhs = jnp.concatenate([pr["vh"] * pr["bcol"], pr["kb"] * pr["egc"]], axis=1)
        pr["uw"] = jnp.dot(pr["x"], rhs, preferred_element_type=F32)
    for pr in pairs:
        gcol = pr["gcol"]
        kd = pr["kh"] * jnp.exp(gcol[n - 1:n, :] - gcol)
        pr["auw"] = jnp.dot(pr["qk"] * pr["decay"], pr["uw"], preferred_element_type=F32)
        pr["kuw"] = _dot_tn(kd, pr["uw"])
    for c in range(chunks):
        rows = slice(c * n, (c + 1) * n)
        sel = pairs[c * B_HEADS:(c + 1) * B_HEADS]
        mt_ref[rows, :] = jnp.concatenate([-pr["kuw"][:, HEAD:] for pr in sel], axis=1)
        nn_ref[rows, :] = jnp.concatenate([pr["kuw"][:, :HEAD] for pr in sel], axis=1)
        qe_ref[rows, :] = jnp.concatenate([pr["qh"] * pr["egc"] - pr["auw"][:, HEAD:] for pr in sel], axis=1)
        au_ref[rows, :] = jnp.concatenate([pr["auw"][:, :HEAD] for pr in sel], axis=1)


def _gdn_scan_kernel(mt_ref, nn_ref, qe_ref, au_ref, gc_ref, z_ref, nw_ref, y_ref, sfin_ref, s_ref, *, batch):
    c = pl.program_id(0)
    n = GDN_CHUNK

    @pl.when(c == 0)
    def _():
        s_ref[...] = jnp.zeros_like(s_ref)

    nw = nw_ref[...]
    pairs = []
    for b in range(batch):
        mt, qe = mt_ref[b], qe_ref[b]
        for h in range(B_HEADS):
            lo, hi = h * HEAD, (h + 1) * HEAD
            pairs.append(dict(b=b, h=h, lhs=jnp.concatenate([mt[:, lo:hi], qe[:, lo:hi]], axis=0), s=s_ref[b, h]))
    for pr in pairs:
        pr["r"] = jnp.dot(pr["lhs"], pr["s"], preferred_element_type=F32)
    for b in range(batch):
        nn, au = nn_ref[b], au_ref[b]
        eg = jnp.exp(gc_ref[b, n - 1:n, :])
        outs = []
        for pr in pairs[b * B_HEADS:(b + 1) * B_HEADS]:
            h = pr["h"]
            lo, hi = h * HEAD, (h + 1) * HEAD
            s_new = pr["s"] * eg[:, h:h + 1] + pr["r"][:n] + nn[:, lo:hi]
            s_ref[b, h] = s_new
            sfin_ref[b, h] = s_new
            o = pr["r"][n:] + au[:, lo:hi]
            ms = jnp.mean(o * o, axis=-1, keepdims=True)
            outs.append(o * lax.rsqrt(ms + EPS) * nw)
        y_ref[b] = jnp.concatenate(outs, axis=1) * _silu(z_ref[b])


def _gdn_prompt(q, k, v, g, beta, z, norm_w, batch, chunks_per_step=4):
    m = q.shape[0]
    t = m // batch
    n = GDN_CHUNK
    nc = t // n
    rows = chunks_per_step * n
    row = lambda w: pl.BlockSpec((rows, w), lambda i: (i, 0))
    tri = jnp.tril(jnp.ones((n, n), F32))
    wide = jax.ShapeDtypeStruct((m, B_WIDTH), F32)
    mt, nn, qe, au, gc = pl.pallas_call(
        functools.partial(_gdn_chunk_kernel, chunks=chunks_per_step),
        grid=(m // rows,),
        in_specs=[row(B_WIDTH), row(B_WIDTH), row(B_WIDTH), row(LANE), row(LANE),
                  pl.BlockSpec((n, n), lambda i: (0, 0))],
        out_specs=[row(B_WIDTH)] * 4 + [row(LANE)],
        out_shape=[wide] * 4 + [jax.ShapeDtypeStruct((m, LANE), F32)],
        compiler_params=_params(1),
        name="gdn_chunk",
    )(q, k, v, g, beta, tri)
    per_chunk = lambda w: pl.BlockSpec((batch, n, w), lambda c: (0, c, 0))
    state = pl.BlockSpec((batch, B_HEADS, HEAD, HEAD), lambda c: (0, 0, 0, 0))
    view = lambda a: a.reshape(batch, t, a.shape[1])
    y, s_fin = pl.pallas_call(
        functools.partial(_gdn_scan_kernel, batch=batch),
        grid=(nc,),
        in_specs=[per_chunk(B_WIDTH)] * 4 + [per_chunk(LANE), per_chunk(B_WIDTH),
                                             pl.BlockSpec((1, HEAD), lambda c: (0, 0))],
        out_specs=[per_chunk(B_WIDTH), state],
        out_shape=[jax.ShapeDtypeStruct((batch, t, B_WIDTH), F32),
                   jax.ShapeDtypeStruct((batch, B_HEADS, HEAD, HEAD), F32)],
        scratch_shapes=[pltpu.VMEM((batch, B_HEADS, HEAD, HEAD), F32)],
        compiler_params=_params(1),
        name="gdn_scan",
    )(view(mt), view(nn), view(qe), view(au), view(gc), view(z), norm_w.reshape(1, HEAD))
    return y.reshape(m, B_WIDTH), s_fin


def _gdn_sample_kernel(q_ref, k_ref, v_ref, g_ref, beta_ref, z_ref, nw_ref, s0_ref, y_ref, s_ref, *, steps):
    h = pl.program_id(0)
    bsz = s_ref.shape[-1]
    zero = jnp.zeros((HEAD, bsz), F32)

    def bcast_row(ref, i, dk):
        return jnp.broadcast_to(ref[i, pl.ds(dk, 1), :], (HEAD, bsz))

    def head_row(ref, i):
        return ref[i, pl.ds(h, 1), :]

    def finish(i, o):
        ms = jnp.mean(o * o, axis=0, keepdims=True)
        y_ref[i] = o * lax.rsqrt(ms + EPS) * nw_ref[...] * _silu(z_ref[i])

    gam = jnp.exp(head_row(g_ref, 0))

    def first(dk, ks):
        s = s0_ref[0, dk] * gam
        s_ref[0, dk] = s
        return ks + bcast_row(k_ref, 0, dk) * s

    ks = lax.fori_loop(0, HEAD, first, zero)
    vn = head_row(beta_ref, 0) * (v_ref[0] - ks)

    for i in range(1, steps):
        gam = jnp.exp(head_row(g_ref, i))

        def mid(dk, carry, i=i, vn=vn, gam=gam):
            o, ks = carry
            s = s_ref[0, dk] + bcast_row(k_ref, i - 1, dk) * vn
            o = o + bcast_row(q_ref, i - 1, dk) * s
            s = s * gam
            s_ref[0, dk] = s
            return o, ks + bcast_row(k_ref, i, dk) * s

        o, ks = lax.fori_loop(0, HEAD, mid, (zero, zero))
        finish(i - 1, o)
        vn = head_row(beta_ref, i) * (v_ref[i] - ks)

    def last(dk, o):
        s = s_ref[0, dk] + bcast_row(k_ref, steps - 1, dk) * vn
        s_ref[0, dk] = s
        return o + bcast_row(q_ref, steps - 1, dk) * s

    finish(steps - 1, lax.fori_loop(0, HEAD, last, zero))


def _gdn_sample(qt, kt, vt, gt, bt, zt, nwb, s0t, layer):
    steps, _, bsz = qt.shape
    per_head = pl.BlockSpec((steps, HEAD, bsz), lambda h: (0, h, 0))
    small = pl.BlockSpec((steps, SUBLANE, bsz), lambda h: (0, 0, 0))
    state = pl.BlockSpec((1, HEAD, HEAD, bsz), lambda h: (h, 0, 0, 0))
    state_in = pl.BlockSpec((None, 1, HEAD, HEAD, bsz), lambda h: (layer, h, 0, 0, 0))
    return pl.pallas_call(
        functools.partial(_gdn_sample_kernel, steps=steps),
        grid=(B_HEADS,),
        in_specs=[per_head, per_head, per_head, small, small, per_head,
                  pl.BlockSpec((HEAD, bsz), lambda h: (0, 0)), state_in],
        out_specs=[per_head, state],
        out_shape=[jax.ShapeDtypeStruct((steps, B_WIDTH, bsz), F32),
                   jax.ShapeDtypeStruct((B_HEADS, HEAD, HEAD, bsz), F32)],
        compiler_params=_params(1),
        name="gdn_sample",
    )(qt, kt, vt, gt, bt, zt, nwb, s0t)


def _attn_prompt_kernel(q_ref, k_ref, v_ref, o_ref, lse_ref, *, dil, nb, chains_per_stage):
    n = ATTN_BLOCK
    scale = HEAD ** -0.5
    lane = lax.broadcasted_iota(jnp.int32, (n, GROUP_W), 1)
    head0 = lane < HEAD
    qi = lax.broadcasted_iota(jnp.int32, (n, n), 0)
    kj = lax.broadcasted_iota(jnp.int32, (n, n), 1)
    mask_c = kj <= qi
    mask_p = kj >= qi

    def rows(r, i):
        if dil == 1:
            return pl.ds(i * n, n)
        return pl.ds(r + i * n * dil, n, stride=dil)

    blocks = [(r, i) for r in range(dil) for i in range(nb)]
    for first in range(0, len(blocks), chains_per_stage):
        stage = []
        for r, i in blocks[first:first + chains_per_stage]:
            q = q_ref[rows(r, i), :]
            for sel in (head0, jnp.logical_not(head0)):
                stage.append(dict(r=r, i=i, qh=jnp.where(sel, q, 0.0)))
        for ch in stage:
            r, i = ch["r"], ch["i"]
            ch["sc"] = _dot_nt(ch["qh"], k_ref[rows(r, i), :])
            if i > 0:
                ch["sp"] = _dot_nt(ch["qh"], k_ref[rows(r, i - 1), :])
        for ch in stage:
            sc = jnp.where(mask_c, ch["sc"] * scale, NEG)
            mx = jnp.max(sc, axis=-1, keepdims=True)
            if ch["i"] > 0:
                sp = jnp.where(mask_p, ch["sp"] * scale, NEG)
                mx = jnp.maximum(mx, jnp.max(sp, axis=-1, keepdims=True))
                ch["pp"] = jnp.exp(sp - mx)
            ch["pc"] = jnp.exp(sc - mx)
            den = jnp.sum(ch["pc"], axis=-1, keepdims=True)
            if ch["i"] > 0:
                den = den + jnp.sum(ch["pp"], axis=-1, keepdims=True)
            ch["den"] = den
            ch["lse"] = mx + jnp.log(den)
        for ch in stage:
            r, i = ch["r"], ch["i"]
            acc = jnp.dot(ch["pc"], v_ref[rows(r, i), :], preferred_element_type=F32)
            if i > 0:
                acc = acc + jnp.dot(ch["pp"], v_ref[rows(r, i - 1), :], preferred_element_type=F32)
            ch["o"] = acc / ch["den"]
        for c0, c1 in zip(stage[0::2], stage[1::2]):
            r, i = c0["r"], c0["i"]
            o_ref[rows(r, i), :] = jnp.where(head0, c0["o"], c1["o"])
            lse_ref[rows(r, i), :] = jnp.where(head0, jnp.broadcast_to(c0["lse"], (n, GROUP_W)),
                                               jnp.broadcast_to(c1["lse"], (n, GROUP_W)))


def _attn_prompt(qrot, kv, group, batch, chains_per_stage=4):
    _, dil = GROUPS[group]
    m = qrot.shape[0]
    t = m // batch
    nb = t // (dil * ATTN_BLOCK)
    seq = lambda col: pl.BlockSpec((t, GROUP_W), lambda b: (b, col))
    out = jax.ShapeDtypeStruct((m, GROUP_W), F32)
    return pl.pallas_call(
        functools.partial(_attn_prompt_kernel, dil=dil, nb=nb, chains_per_stage=chains_per_stage),
        grid=(batch,),
        in_specs=[seq(group), seq(0), seq(1)],
        out_specs=[seq(0), seq(0)],
        out_shape=[out, out],
        compiler_params=_params(1),
        name=f"attn_prompt_g{group}",
    )(qrot, kv, kv)


def _attn_sample_kernel(q_ref, n0_ref, n1_ref, n2_ref, c0_ref, c1_ref, c2_ref, o_ref, lse_ref, *, bt, steps):
    rows = 2 * steps
    scale = HEAD ** -0.5
    lane = lax.broadcasted_iota(jnp.int32, (rows, GROUP_W), 1)
    rix = lax.broadcasted_iota(jnp.int32, (rows, GROUP_W), 0)
    own_head = (lane < HEAD) == (rix < steps)
    nkey = lax.broadcasted_iota(jnp.int32, (rows, rows), 1)
    nqry = lax.broadcasted_iota(jnp.int32, (rows, rows), 0) % steps
    new_refs = (n0_ref, n1_ref, n2_ref)
    cache_refs = (c0_ref, c1_ref, c2_ref)

    head0 = lane[:steps] < HEAD
    chains = []
    for b in range(bt):
        q_all = q_ref[b]
        for g, (win, dil) in enumerate(GROUPS):
            qm = jnp.where(own_head, q_all[:, g * GROUP_W:(g + 1) * GROUP_W], 0.0)
            chains.append(dict(b=b, g=g, win=win, dil=dil, qm=qm, new=new_refs[g][b]))
    for ch in chains:
        kt = cache_refs[ch["g"]][0, ch["b"], 0].reshape(GROUP_W, ch["win"])
        ch["s_buf"] = jnp.dot(ch["qm"], kt, preferred_element_type=F32)
        ch["s_new"] = _dot_nt(ch["qm"], ch["new"][:, :GROUP_W])
    for ch in chains:
        win, dil = ch["win"], ch["dil"]
        pos = lax.broadcasted_iota(jnp.int32, (rows, win), 1)
        qry = lax.broadcasted_iota(jnp.int32, (rows, win), 0) % steps
        if dil == 1:
            ok = pos >= qry
            new_ok = jnp.logical_and(nkey < steps, nkey <= nqry)
        else:
            ok = (pos % dil) == qry
            new_ok = nkey == nqry
        s_buf = jnp.where(ok, ch["s_buf"] * scale, NEG)
        s_new = jnp.where(new_ok, ch["s_new"] * scale, NEG)
        mx = jnp.maximum(jnp.max(s_buf, axis=-1, keepdims=True), jnp.max(s_new, axis=-1, keepdims=True))
        ch["p_buf"] = jnp.exp(s_buf - mx)
        ch["p_new"] = jnp.exp(s_new - mx)
        ch["den"] = jnp.sum(ch["p_buf"], axis=-1, keepdims=True) + jnp.sum(ch["p_new"], axis=-1, keepdims=True)
        ch["mx"] = mx
    for ch in chains:
        vt = cache_refs[ch["g"]][0, ch["b"], 1].reshape(GROUP_W, ch["win"])
        ch["acc"] = jnp.dot(ch["p_new"], ch["new"][:, GROUP_W:], preferred_element_type=F32) + _dot_nt(ch["p_buf"], vt)
    for ch in chains:
        b, g = ch["b"], ch["g"]
        o_full = ch["acc"] / ch["den"]
        lse_full = jnp.broadcast_to(ch["mx"] + jnp.log(ch["den"]), (rows, GROUP_W))
        o_ref[b, :, g * GROUP_W:(g + 1) * GROUP_W] = jnp.where(head0, o_full[:steps], o_full[steps:])
        lse_ref[b, :, g * GROUP_W:(g + 1) * GROUP_W] = jnp.where(head0, lse_full[:steps], lse_full[steps:])


def _attn_sample(q2, new_kv, caches, layer, bt):
    bsz, rows, _ = q2.shape
    steps = rows // 2
    per_b = lambda w: pl.BlockSpec((bt, rows, w), lambda i: (i, 0, 0))
    cache_spec = lambda win: pl.BlockSpec((1, bt, 2, HPG, HEAD, win), lambda i: (layer, i, 0, 0, 0, 0))
    out_spec = pl.BlockSpec((bt, steps, C_WIDTH), lambda i: (i, 0, 0))
    out = jax.ShapeDtypeStruct((bsz, steps, C_WIDTH), F32)
    return pl.pallas_call(
        functools.partial(_attn_sample_kernel, bt=bt, steps=steps),
        grid=(bsz // bt,),
        in_specs=[per_b(C_WIDTH), per_b(2 * GROUP_W), per_b(2 * GROUP_W), per_b(2 * GROUP_W)]
        + [cache_spec(win) for win, _ in GROUPS],
        out_specs=[out_spec, out_spec],
        out_shape=[out, out],
        compiler_params=_params(1),
        name="attn_sample",
    )(q2, *new_kv, *caches)


def _outproj_kernel(ya_ref, yb_ref, o0_ref, o1_ref, o2_ref, l0_ref, l1_ref, l2_ref, cz_ref, x_ref, gate_ref,
                    w_ref, xo_ref):
    lses = (l0_ref[...], l1_ref[...], l2_ref[...])
    mx = jnp.maximum(jnp.maximum(lses[0], lses[1]), lses[2])
    es = [jnp.exp(l - mx) for l in lses]
    tot = es[0] + es[1] + es[2]
    cz = cz_ref[...]

    def mm(y, lo):
        return jnp.dot(y.astype(BF16), w_ref[lo:lo + y.shape[1], :], preferred_element_type=F32)

    out = mm(ya_ref[...], 0) + mm(yb_ref[...], A_WIDTH)
    for g, o_ref in enumerate((o0_ref, o1_ref, o2_ref)):
        yc = o_ref[...] * (es[g] / tot) * _silu(cz[:, g * GROUP_W:(g + 1) * GROUP_W])
        out = out + mm(yc, A_WIDTH + B_WIDTH + g * GROUP_W)
    xo_ref[...] = x_ref[...] + (1.0 + gate_ref[0]) * out


def _outproj(ya, yb, os_, lses, cz, x_rows, mod, w_out, layer, tm):
    m = x_rows.shape[0]
    nblk = m // tm
    nb_mod, r, _ = mod.shape
    tiles_per_mod = nblk // nb_mod
    row = lambda w: pl.BlockSpec((tm, w), lambda i: (i, 0))
    return pl.pallas_call(
        _outproj_kernel,
        grid=(nblk,),
        in_specs=[row(A_WIDTH), row(B_WIDTH)] + [row(GROUP_W)] * 6 + [row(C_WIDTH), row(D_MODEL),
                  pl.BlockSpec((1, r, D_MODEL), lambda i: (i // tiles_per_mod, 0, 2)),
                  pl.BlockSpec((None, D_MODEL, D_MODEL), lambda i: (layer, 0, 0))],
        out_specs=row(D_MODEL),
        out_shape=jax.ShapeDtypeStruct((m, D_MODEL), F32),
        compiler_params=_params(1),
        name="outproj",
    )(ya, yb, *os_, *lses, cz, x_rows, mod, w_out)


def _final_kernel(x_ref, sh_ref, sc_ref, nw_ref, y_ref):
    x = x_ref[...]
    var = jnp.mean(x * x, axis=-1, keepdims=True)
    y_ref[...] = x * lax.rsqrt(var + EPS) * nw_ref[...] * (1.0 + sc_ref[0]) + sh_ref[0]


def _final(x_rows, mod, norm_w, tm):
    m = x_rows.shape[0]
    nblk = m // tm
    nb_mod, r, _ = mod.shape
    tiles_per_mod = nblk // nb_mod
    row = pl.BlockSpec((tm, D_MODEL), lambda i: (i, 0))
    return pl.pallas_call(
        _final_kernel,
        grid=(nblk,),
        in_specs=[row,
                  pl.BlockSpec((1, r, D_MODEL), lambda i: (i // tiles_per_mod, 0, 0)),
                  pl.BlockSpec((1, r, D_MODEL), lambda i: (i // tiles_per_mod, 0, 1)),
                  pl.BlockSpec((1, D_MODEL), lambda i: (0, 0))],
        out_specs=row,
        out_shape=jax.ShapeDtypeStruct((m, D_MODEL), F32),
        compiler_params=_params(1),
        name="final_norm",
    )(x_rows, mod, mod, norm_w.reshape(1, D_MODEL))


def _rope_tables(pos):
    half = ROT_DIM // 2
    inv_freq = ROPE_THETA ** (-jnp.arange(half, dtype=F32) * (2.0 / ROT_DIM))
    ang = pos.astype(F32)[:, None] * inv_freq[None, :]
    cos, sin = jnp.cos(ang), jnp.sin(ang)
    rows = pos.shape[0]
    one = jnp.ones((rows, HEAD - ROT_DIM), F32)
    zero_r = jnp.zeros((rows, HEAD - ROT_DIM), F32)
    zero_h = jnp.zeros((rows, half), F32)
    cos_h = jnp.concatenate([cos, cos, one], axis=1)
    sa_h = jnp.concatenate([-sin, zero_h, zero_r], axis=1)
    sb_h = jnp.concatenate([zero_h, sin, zero_r], axis=1)
    return tuple(jnp.concatenate([t] * HPG, axis=1) for t in (cos_h, sa_h, sb_h))


def _permute_w_in(w_in):
    depth = w_in.shape[0]
    pad = jnp.zeros((depth, D_MODEL, IN_PERM_WIDTH - w_in.shape[2]), w_in.dtype)
    return jnp.concatenate([w_in[:, :, :SRC_AB[0]], w_in[:, :, SRC_AB[1]:], w_in[:, :, SRC_AB[0]:SRC_AB[1]], pad],
                           axis=2).astype(BF16)


def _segment_ones():
    idx = jnp.arange(B_WIDTH) // HEAD
    return (idx[:, None] == idx[None, :]).astype(F32)


def _prompt_trunk(x, mods, mod_final, wts, seg):
    batch, t, _ = x.shape
    depth = wts["w_in"].shape[0]
    m = batch * t
    tm = 512
    rows = x.reshape(m, D_MODEL)
    tabs = _rope_tables(jnp.arange(t))
    zeros_a = jnp.zeros((batch, A_CONV - 1, A_WIDTH), F32)
    zeros_b = jnp.zeros((batch, B_CONV - 1, B_QKV), F32)
    st_a, st_b, st_g, kvs = [], [], [], [[] for _ in GROUPS]
    for l in range(depth):
        mod = mods[l].reshape(batch, 1, 3 * D_MODEL)
        ya, q, k, v, g, beta, bz, qrot, kv0, kv1, kv2, cz, sa, sb = _inproj(
            rows, mod, zeros_a, zeros_b, tabs, wts, l, seg, seqs=batch, stride=1, tm=tm)
        yb, s_fin = _gdn_prompt(q, k, v, g, beta, bz, wts["gdn_norm_w"][l], batch)
        os_, lses = [], []
        for gi, kv in enumerate((kv0, kv1, kv2)):
            o, lse = _attn_prompt(qrot, kv, gi, batch)
            os_.append(o)
            lses.append(lse)
            win = min(GROUPS[gi][0], t)
            kvs[gi].append(kv.reshape(batch, t, 2, HPG, HEAD)[:, t - win:])
        rows = _outproj(ya, yb, os_, lses, cz, rows, mod, wts["w_out"], l, tm)
        st_a.append(sa)
        st_b.append(sb)
        st_g.append(s_fin)
    y = _final(rows, mod_final.reshape(batch, 1, 2 * D_MODEL), wts["final_norm_w"], tm).reshape(batch, t, D_MODEL)
    return y, jnp.stack(st_a), jnp.stack(st_b), jnp.stack(st_g), [jnp.stack(r) for r in kvs]


def _sample_trunk(x, mods, mod_final, state_a, state_b, state_g, caches, past_len, wts, seg):
    bsz, steps, _ = x.shape
    depth = wts["w_in"].shape[0]
    m = bsz * steps
    tm = bsz

    def time_major(a):
        return a.transpose(1, 0, 2).reshape(a.shape[1] * bsz, a.shape[2])

    def batch_major(a, n):
        return a.reshape(n, bsz, a.shape[1]).transpose(1, 0, 2)

    def lanes_batch(a, n):
        return a.reshape(n, bsz, a.shape[1]).transpose(0, 2, 1)

    rows = time_major(x)
    tabs = _rope_tables(jnp.repeat(past_len + jnp.arange(steps), bsz))
    cache_views = [c.transpose(0, 1, 3, 4, 5, 2) for c in caches]
    state_t = state_g.transpose(0, 2, 3, 4, 1)
    st_a, st_b, st_g, kvs = [], [], [], [[] for _ in GROUPS]
    for l in range(depth):
        mod = mods[l][None]
        buf_a = time_major(state_a[l])[None]
        buf_b = time_major(state_b[l])[None]
        ya, q, k, v, g, beta, bz, qrot, kv0, kv1, kv2, cz, sa, sb = _inproj(
            rows, mod, buf_a, buf_b, tabs, wts, l, seg, seqs=1, stride=bsz, tm=m)
        nwb = jnp.broadcast_to(wts["gdn_norm_w"][l][:, None], (HEAD, bsz))
        yt, s_t = _gdn_sample(lanes_batch(q, steps), lanes_batch(k, steps), lanes_batch(v, steps),
                              lanes_batch(g[:, :SUBLANE], steps), lanes_batch(beta[:, :SUBLANE], steps),
                              lanes_batch(bz, steps), nwb, state_t, l)
        yb = yt.transpose(0, 2, 1).reshape(m, B_WIDTH)
        dup = lambda a: jnp.concatenate([batch_major(a, steps)] * 2, axis=1)
        o_c, lse_c = _attn_sample(dup(qrot), [dup(kv) for kv in (kv0, kv1, kv2)], cache_views, l, 4)
        o_c, lse_c = time_major(o_c), time_major(lse_c)
        os_ = [o_c[:, gi * GROUP_W:(gi + 1) * GROUP_W] for gi in range(len(GROUPS))]
        lses = [lse_c[:, gi * GROUP_W:(gi + 1) * GROUP_W] for gi in range(len(GROUPS))]
        rows = _outproj(ya, yb, os_, lses, cz, rows, mod, wts["w_out"], l, tm)
        st_a.append(batch_major(sa[0], A_CONV - 1))
        st_b.append(batch_major(sb[0], B_CONV - 1))
        st_g.append(s_t.transpose(3, 0, 1, 2))
        for gi, kv in enumerate((kv0, kv1, kv2)):
            kvs[gi].append(batch_major(kv, steps).reshape(bsz, steps, 2, HPG, HEAD))
    y = batch_major(_final(rows, mod_final[None], wts["final_norm_w"], tm), steps)
    return y, jnp.stack(st_a), jnp.stack(st_b), jnp.stack(st_g), [jnp.stack(r) for r in kvs]


def kernel(x_prompt, x_sample, state_conv_a, state_conv_b, state_gdn, cache_kv_w128, cache_kv_w512,
           cache_kv_w2048, c_prompt, c_sample, w_in, w_out, w_ada, b_ada, norm_w, conv_a_w, conv_b_w,
           a_log, dt_bias, gdn_norm_w, final_norm_w, w_ada_final, b_ada_final):
    n_prompt = c_prompt.shape[0]
    c_all = jnp.concatenate([c_prompt, c_sample], axis=0)
    mods = _ada(c_all, w_ada, b_ada)
    mod_final = _ada(c_all, w_ada_final[None], b_ada_final[None])[0]
    wts = dict(w_in=_permute_w_in(w_in), w_out=w_out.astype(BF16), norm_w=norm_w[:, None, :], conv_a_w=conv_a_w,
               conv_b_w=conv_b_w, a_log=a_log, dt_bias=dt_bias, gdn_norm_w=gdn_norm_w,
               final_norm_w=final_norm_w)
    seg = _segment_ones()
    y_p, ca_p, cb_p, g_p, kv_p = _prompt_trunk(x_prompt, mods[:, :n_prompt], mod_final[:n_prompt], wts, seg)
    y_s, ca_s, cb_s, g_s, kv_s = _sample_trunk(
        x_sample, mods[:, n_prompt:], mod_final[n_prompt:], state_conv_a, state_conv_b, state_gdn,
        (cache_kv_w128, cache_kv_w512, cache_kv_w2048), PAST_LEN, wts, seg)
    return (y_p, y_s, ca_p, ca_s, cb_p, cb_s, g_p, g_s,
            kv_p[0], kv_s[0], kv_p[1], kv_s[1], kv_p[2], kv_s[2])
```

```python
import functools
import math

import jax
import jax.numpy as jnp
from jax import lax
from jax.experimental import pallas as pl
from jax.experimental.pallas import tpu as pltpu

F32 = jnp.float32
BF16 = jnp.bfloat16
HIGHEST = lax.Precision.HIGHEST

D_MODEL = 1024
HEAD = 64
A_WIDTH = 256
A_CONV = 3
B_HEADS = 6
B_WIDTH = B_HEADS * HEAD
B_QKV = 3 * B_WIDTH
B_CONV = 4
GDN_CHUNK = 64
GROUPS = ((128, 1), (512, 4), (2048, 16))
HPG = 2
GROUP_W = HPG * HEAD
C_WIDTH = len(GROUPS) * GROUP_W
ATTN_BLOCK = 128
ROT_DIM = HEAD // 4
ROPE_THETA = 500000.0
PAST_LEN = 2048
EPS = 1e-6
NEG = -1e30
LANE = 128
SUBLANE = 8

COL_A = (0, 1024)
COL_BQKV = (1024, 2176)
COL_AB = (2176, 2304)
COL_BZ = (2304, 2688)
COL_CZ = (2688, 3072)
COL_CQ = (3072, 3456)
COL_CK = (3456, 3840)
COL_CV = (3840, 4224)
IN_PERM_WIDTH = 4224
SRC_A = (0, 1024)
SRC_BQKV = (1024, 2176)
SRC_BZ = (2176, 2560)
SRC_AB = (2560, 2572)
SRC_CQKV = (2572, 3724)
SRC_CZ = (3724, 4108)

VMEM_LIMIT_BYTES = 56 * 1024 * 1024


def _params(n_axes):
    return pltpu.CompilerParams(dimension_semantics=("arbitrary",) * n_axes,
                                vmem_limit_bytes=VMEM_LIMIT_BYTES)


def _round_up(x, m):
    return -(-x // m) * m


def _silu(x):
    return x * jax.nn.sigmoid(x)


def _dot_nt(a, b, **kw):
    return lax.dot_general(a, b, (((1,), (1,)), ((), ())), preferred_element_type=F32, **kw)


def _dot_tn(a, b, **kw):
    return lax.dot_general(a, b, (((0,), (0,)), ((), ())), preferred_element_type=F32, **kw)


def _ada_kernel(c_ref, w_ref, b_ref, o_ref):
    c = c_ref[...].astype(BF16)
    w = w_ref[0].astype(BF16)
    o_ref[0] = jnp.dot(c, w, preferred_element_type=F32) + b_ref[0]


def _ada(c_all, w, b):
    n_layers, _, n = w.shape
    r = c_all.shape[0]
    tn = 1024
    return pl.pallas_call(
        _ada_kernel,
        grid=(n_layers, n // tn),
        in_specs=[pl.BlockSpec((r, D_MODEL), lambda l, j: (0, 0)),
                  pl.BlockSpec((1, D_MODEL, tn), lambda l, j: (l, 0, j)),
                  pl.BlockSpec((1, 1, tn), lambda l, j: (l, 0, j))],
        out_specs=pl.BlockSpec((1, r, tn), lambda l, j: (l, 0, j)),
        out_shape=jax.ShapeDtypeStruct((n_layers, r, n), F32),
        compiler_params=_params(2),
        name="ada",
    )(c_all, w, b.reshape(n_layers, 1, n))


def _inproj_kernel(x_ref, sh_ref, sc_ref, nw_ref, w_ref, cos_ref, sa_ref, sb_ref,
                   bufa_ref, bufb_ref, wa_ref, wb_ref, alog_ref, dtb_ref, seg_ref,
                   ya_ref, q_ref, k_ref, v_ref, g_ref, beta_ref, bz_ref, qrot_ref, kv0_ref, kv1_ref, kv2_ref, cz_ref,
                   sta_ref, stb_ref, xpa_ref, xpb_ref, *, stride, tm, tiles_per_seq):
    t = pl.program_id(0) % tiles_per_seq
    x = x_ref[...]
    reps = tm // sh_ref.shape[1] if sh_ref.shape[1] > 1 else 1
    tile_rows = lambda a: a if reps == 1 else jnp.concatenate([a] * reps, axis=0)
    var = jnp.mean(x * x, axis=-1, keepdims=True)
    hn = x * lax.rsqrt(var + EPS) * nw_ref[...]
    hn = hn * (1.0 + tile_rows(sc_ref[0])) + tile_rows(sh_ref[0])
    hb = hn.astype(BF16)

    def mm(lo, hi):
        return jnp.dot(hb, w_ref[:, lo:hi], preferred_element_type=F32)

    def halo_rows(taps):
        halo = (taps - 1) * stride
        return halo, _round_up(halo, SUBLANE)

    def load_halo(xp_ref, buf_ref, taps):
        halo, x0 = halo_rows(taps)

        @pl.when(t == 0)
        def _():
            xp_ref[x0 - halo:x0, :] = buf_ref[0]

        @pl.when(t > 0)
        def _():
            xp_ref[x0 - halo:x0, :] = xp_ref[x0 + tm - halo:x0 + tm, :]

    def conv_cols(xp_ref, x_new, w_ref, taps, lo, hi):
        halo, x0 = halo_rows(taps)
        xp_ref[x0:x0 + tm, lo:hi] = x_new
        y = xp_ref[x0 - halo:x0 - halo + tm, lo:hi] * w_ref[0:1, lo:hi]
        for j in range(1, taps):
            r0 = x0 - halo + j * stride
            y = y + xp_ref[r0:r0 + tm, lo:hi] * w_ref[j:j + 1, lo:hi]
        return y

    def store_tail(xp_ref, st_ref, taps):
        halo, x0 = halo_rows(taps)
        st_ref[0] = xp_ref[x0 + tm - halo:x0 + tm, :]

    load_halo(xpa_ref, bufa_ref, A_CONV)
    load_halo(xpb_ref, bufb_ref, B_CONV)
    seg = seg_ref[...].astype(BF16)
    cos = cos_ref[...]
    sa = sa_ref[...]
    sb = sb_ref[...]

    def rope(u):
        return u * cos + pltpu.roll(u, LANE - ROT_DIM // 2, 1) * sa + pltpu.roll(u, ROT_DIM // 2, 1) * sb

    def branch_b(raw, lo, hi):
        act = _silu(conv_cols(xpb_ref, raw, wb_ref, B_CONV, lo, hi))
        if lo < 2 * B_WIDTH:
            sq = act * act
            hi16 = sq.astype(BF16)
            lo16 = (sq - hi16.astype(F32)).astype(BF16)
            blk = seg[:hi - lo, :hi - lo]
            ss = jnp.dot(hi16, blk, preferred_element_type=F32) + jnp.dot(lo16, blk, preferred_element_type=F32)
            act = act * lax.rsqrt(ss + 1e-6)
        for out_ref, base, scale in ((q_ref, 0, HEAD ** -0.5), (k_ref, B_WIDTH, None), (v_ref, 2 * B_WIDTH, None)):
            c0, c1 = max(lo, base), min(hi, base + B_WIDTH)
            if c0 < c1:
                piece = act[:, c0 - lo:c1 - lo]
                out_ref[:, c0 - base:c1 - base] = piece if scale is None else piece * scale

    def use_b_tail(raw):
        n_v = B_QKV - 4 * wide
        branch_b(raw[:, :n_v], 4 * wide, B_QKV)
        ab = raw[:, n_v:]
        z = ab + dtb_ref[...]
        softplus = jnp.maximum(z, 0.0) + jnp.log1p(jnp.exp(-jnp.abs(z)))
        g_ref[...] = -jnp.exp(alog_ref[...]) * softplus
        beta_ref[...] = jax.nn.sigmoid(pltpu.roll(ab, LANE - B_HEADS, 1))

    def use_gates(raw):
        bz_ref[...] = raw[:, :B_WIDTH]
        cz_ref[...] = raw[:, B_WIDTH:]

    def use_c(raw):
        for g, kv_ref in enumerate((kv0_ref, kv1_ref, kv2_ref)):
            lo, hi = g * GROUP_W, (g + 1) * GROUP_W
            qrot_ref[:, lo:hi] = rope(raw[:, lo:hi])
            kv_ref[:, :GROUP_W] = rope(raw[:, C_WIDTH + lo:C_WIDTH + hi])
            kv_ref[:, GROUP_W:] = raw[:, 2 * C_WIDTH + lo:2 * C_WIDTH + hi]

    conv_a = []

    def use_a_in(raw):
        conv_a.append(conv_cols(xpa_ref, raw[:, A_WIDTH:] * raw[:, :A_WIDTH], wa_ref, A_CONV, 0, A_WIDTH))

    def use_a_out(raw):
        ya_ref[...] = raw[:, :A_WIDTH] * conv_a[0] * _silu(raw[:, A_WIDTH:])

    wide = 2 * LANE
    b0, a0 = COL_BQKV[0], COL_A[0]
    b_chunk = lambda c: ((b0 + c * wide, b0 + (c + 1) * wide), lambda raw: branch_b(raw, c * wide, (c + 1) * wide))
    jobs = [b_chunk(0), ((COL_BZ[0], COL_CZ[1]), use_gates), b_chunk(1), ((COL_CQ[0], COL_CV[1]), use_c),
            b_chunk(2), ((a0, a0 + 2 * A_WIDTH), use_a_in), b_chunk(3),
            ((a0 + 2 * A_WIDTH, a0 + 4 * A_WIDTH), use_a_out), ((b0 + 4 * wide, COL_AB[1]), use_b_tail)]
    pending = mm(*jobs[0][0])
    for i, (_, consume) in enumerate(jobs):
        ahead = mm(*jobs[i + 1][0]) if i + 1 < len(jobs) else None
        consume(pending)
        pending = ahead
    store_tail(xpb_ref, stb_ref, B_CONV)
    store_tail(xpa_ref, sta_ref, A_CONV)


def _inproj(x_rows, mod, buf_a, buf_b, rope_tabs, wts, layer, seg, *, seqs, stride, tm):
    m = x_rows.shape[0]
    nblk = m // tm
    tiles_per_seq = nblk // seqs
    nb_mod, r, _ = mod.shape
    tiles_per_mod = nblk // nb_mod
    nt_tab = rope_tabs[0].shape[0] // tm
    halo_a, halo_b = (A_CONV - 1) * stride, (B_CONV - 1) * stride
    row = lambda w: pl.BlockSpec((tm, w), lambda i: (i, 0))
    tab = pl.BlockSpec((tm, LANE), lambda i: (i % nt_tab, 0))
    full = lambda a: pl.BlockSpec(a.shape, lambda i: (0,) * a.ndim)
    per_layer = lambda a: pl.BlockSpec((None,) + a.shape[1:], lambda i: (layer,) + (0,) * (a.ndim - 1))
    per_seq = lambda h, w: pl.BlockSpec((1, h, w), lambda i: (i // tiles_per_seq, 0, 0))
    alog = jnp.zeros((1, LANE), F32).at[0, :B_HEADS].set(wts["a_log"][layer])
    dtb = jnp.zeros((1, LANE), F32).at[0, :B_HEADS].set(wts["dt_bias"][layer])
    out_w = (A_WIDTH, B_WIDTH, B_WIDTH, B_WIDTH, LANE, LANE, B_WIDTH, C_WIDTH, 2 * GROUP_W, 2 * GROUP_W, 2 * GROUP_W,
             C_WIDTH)
    return pl.pallas_call(
        functools.partial(_inproj_kernel, stride=stride, tm=tm, tiles_per_seq=tiles_per_seq),
        grid=(nblk,),
        in_specs=[row(D_MODEL),
                  pl.BlockSpec((1, r, D_MODEL), lambda i: (i // tiles_per_mod, 0, 0)),
                  pl.BlockSpec((1, r, D_MODEL), lambda i: (i // tiles_per_mod, 0, 1)),
                  per_layer(wts["norm_w"]), per_layer(wts["w_in"]), tab, tab, tab,
                  per_seq(halo_a, A_WIDTH), per_seq(halo_b, B_QKV),
                  per_layer(wts["conv_a_w"]), per_layer(wts["conv_b_w"]), full(alog), full(dtb), full(seg)],
        out_specs=[row(w) for w in out_w] + [per_seq(halo_a, A_WIDTH), per_seq(halo_b, B_QKV)],
        out_shape=[jax.ShapeDtypeStruct((m, w), F32) for w in out_w]
        + [jax.ShapeDtypeStruct((seqs, halo_a, A_WIDTH), F32), jax.ShapeDtypeStruct((seqs, halo_b, B_QKV), F32)],
        scratch_shapes=[pltpu.VMEM((_round_up(halo_a, SUBLANE) + tm, A_WIDTH), F32),
                        pltpu.VMEM((_round_up(halo_b, SUBLANE) + tm, B_QKV), F32)],
        compiler_params=_params(1),
        name="inproj",
    )(x_rows, mod, mod, wts["norm_w"], wts["w_in"], *rope_tabs, buf_a, buf_b,
      wts["conv_a_w"], wts["conv_b_w"], alog, dtb, seg)


def _gdn_chunk_kernel(q_ref, k_ref, v_ref, g_ref, beta_ref, tri_ref,
                      mt_ref, nn_ref, qe_ref, au_ref, gc_ref, *, chunks):
    n = GDN_CHUNK
    ri = lax.broadcasted_iota(jnp.int32, (n, n), 0)
    ci = lax.broadcasted_iota(jnp.int32, (n, n), 1)
    tri = ri >= ci
    strict = ri > ci
    eye = jnp.where(ri == ci, 1.0, 0.0).astype(F32)
    pairs = []
    for c in range(chunks):
        rows = slice(c * n, (c + 1) * n)
        q = q_ref[rows, :]
        k = k_ref[rows, :]
        v = v_ref[rows, :]
        beta = beta_ref[rows, :]
        gc = jnp.dot(tri_ref[...], g_ref[rows, :], preferred_element_type=F32, precision=HIGHEST)
        gc_ref[rows, :] = gc
        gct = gc.T
        for h in range(B_HEADS):
            lo, hi = h * HEAD, (h + 1) * HEAD
            qh, kh, vh = q[:, lo:hi], k[:, lo:hi], v[:, lo:hi]
            gcol = gc[:, h:h + 1]
            grow = gct[h:h + 1, :]
            bcol = beta[:, h:h + 1]
            decay = jnp.where(tri, jnp.exp(jnp.where(tri, gcol - grow, 0.0)), 0.0)
            pairs.append(dict(qh=qh, kh=kh, vh=vh, gcol=gcol, bcol=bcol, decay=decay, kb=kh * bcol))
    for pr in pairs:
        pr["kk"] = _dot_nt(pr["kb"], pr["kh"])
        pr["qk"] = _dot_nt(pr["qh"], pr["kh"])
    for pr in pairs:
        lmat = jnp.where(strict, pr["kk"] * pr["decay"], 0.0)
        pr["x"] = eye - lmat
        pr["p"] = lmat
    for _ in range(5):
        for pr in pairs:
            pr["p"] = jnp.dot(pr["p"], pr["p"], preferred_element_type=F32)
        for pr in pairs:
            pr["x"] = pr["x"] + jnp.dot(pr["x"], pr["p"], preferred_element_type=F32)
    for pr in pairs:
        pr["egc"] = jnp.exp(pr["gcol"])
        rhs = jnp.concatenate([pr["vh"] * pr["bcol"], pr["kb"] * pr["egc"]], axis=1)
        pr["uw"] = jnp.dot(pr["x"], rhs, preferred_element_type=F32)
    for pr in pairs:
        gcol = pr["gcol"]
        kd = pr["kh"] * jnp.exp(gcol[n - 1:n, :] - gcol)
        pr["auw"] = jnp.dot(pr["qk"] * pr["decay"], pr["uw"], preferred_element_type=F32)
        pr["kuw"] = _dot_tn(kd, pr["uw"])
    for c in range(chunks):
        rows = slice(c * n, (c + 1) * n)
        sel = pairs[c * B_HEADS:(c + 1) * B_HEADS]
        mt_ref[rows, :] = jnp.concatenate([-pr["kuw"][:, HEAD:] for pr in sel], axis=1)
        nn_ref[rows, :] = jnp.concatenate([pr["kuw"][:, :HEAD] for pr in sel], axis=1)
        qe_ref[rows, :] = jnp.concatenate([pr["qh"] * pr["egc"] - pr["auw"][:, HEAD:] for pr in sel], axis=1)
        au_ref[rows, :] = jnp.concatenate([pr["auw"][:, :HEAD] for pr in sel], axis=1)


def _gdn_scan_kernel(mt_ref, nn_ref, qe_ref, au_ref, gc_ref, z_ref, nw_ref, y_ref, sfin_ref, s_ref, *, batch):
    c = pl.program_id(0)
    n = GDN_CHUNK

    @pl.when(c == 0)
    def _():
        s_ref[...] = jnp.zeros_like(s_ref)

    nw = nw_ref[...]
    pairs = []
    for b in range(batch):
        mt, qe = mt_ref[b], qe_ref[b]
        for h in range(B_HEADS):
            lo, hi = h * HEAD, (h + 1) * HEAD
            pairs.append(dict(b=b, h=h, lhs=jnp.concatenate([mt[:, lo:hi], qe[:, lo:hi]], axis=0), s=s_ref[b, h]))
    for pr in pairs:
        pr["r"] = jnp.dot(pr["lhs"], pr["s"], preferred_element_type=F32)
    for b in range(batch):
        nn, au = nn_ref[b], au_ref[b]
        eg = jnp.exp(gc_ref[b, n - 1:n, :])
        outs = []
        for pr in pairs[b * B_HEADS:(b + 1) * B_HEADS]:
            h = pr["h"]
            lo, hi = h * HEAD, (h + 1) * HEAD
            s_new = pr["s"] * eg[:, h:h + 1] + pr["r"][:n] + nn[:, lo:hi]
            s_ref[b, h] = s_new
            sfin_ref[b, h] = s_new
            o = pr["r"][n:] + au[:, lo:hi]
            ms = jnp.mean(o * o, axis=-1, keepdims=True)
            outs.append(o * lax.rsqrt(ms + EPS) * nw)
        y_ref[b] = jnp.concatenate(outs, axis=1) * _silu(z_ref[b])


def _gdn_prompt(q, k, v, g, beta, z, norm_w, batch, chunks_per_step=4):
    m = q.shape[0]
    t = m // batch
    n = GDN_CHUNK
    nc = t // n
    rows = chunks_per_step * n
    row = lambda w: pl.BlockSpec((rows, w), lambda i: (i, 0))
    tri = jnp.tril(jnp.ones((n, n), F32))
    wide = jax.ShapeDtypeStruct((m, B_WIDTH), F32)
    mt, nn, qe, au, gc = pl.pallas_call(
        functools.partial(_gdn_chunk_kernel, chunks=chunks_per_step),
        grid=(m // rows,),
        in_specs=[row(B_WIDTH), row(B_WIDTH), row(B_WIDTH), row(LANE), row(LANE),
                  pl.BlockSpec((n, n), lambda i: (0, 0))],
        out_specs=[row(B_WIDTH)] * 4 + [row(LANE)],
        out_shape=[wide] * 4 + [jax.ShapeDtypeStruct((m, LANE), F32)],
        compiler_params=_params(1),
        name="gdn_chunk",
    )(q, k, v, g, beta, tri)
    per_chunk = lambda w: pl.BlockSpec((batch, n, w), lambda c: (0, c, 0))
    state = pl.BlockSpec((batch, B_HEADS, HEAD, HEAD), lambda c: (0, 0, 0, 0))
    view = lambda a: a.reshape(batch, t, a.shape[1])
    y, s_fin = pl.pallas_call(
        functools.partial(_gdn_scan_kernel, batch=batch),
        grid=(nc,),
        in_specs=[per_chunk(B_WIDTH)] * 4 + [per_chunk(LANE), per_chunk(B_WIDTH),
                                             pl.BlockSpec((1, HEAD), lambda c: (0, 0))],
        out_specs=[per_chunk(B_WIDTH), state],
        out_shape=[jax.ShapeDtypeStruct((batch, t, B_WIDTH), F32),
                   jax.ShapeDtypeStruct((batch, B_HEADS, HEAD, HEAD), F32)],
        scratch_shapes=[pltpu.VMEM((batch, B_HEADS, HEAD, HEAD), F32)],
        compiler_params=_params(1),
        name="gdn_scan",
    )(view(mt), view(nn), view(qe), view(au), view(gc), view(z), norm_w.reshape(1, HEAD))
    return y.reshape(m, B_WIDTH), s_fin


def _gdn_sample_kernel(q_ref, k_ref, v_ref, g_ref, beta_ref, z_ref, nw_ref, s0_ref, y_ref, s_ref, *, steps):
    h = pl.program_id(0)
    bsz = s_ref.shape[-1]
    zero = jnp.zeros((HEAD, bsz), F32)

    def bcast_row(ref, i, dk):
        return jnp.broadcast_to(ref[i, pl.ds(dk, 1), :], (HEAD, bsz))

    def head_row(ref, i):
        return ref[i, pl.ds(h, 1), :]

    def finish(i, o):
        ms = jnp.mean(o * o, axis=0, keepdims=True)
        y_ref[i] = o * lax.rsqrt(ms + EPS) * nw_ref[...] * _silu(z_ref[i])

    gam = jnp.exp(head_row(g_ref, 0))

    def first(dk, ks):
        s = s0_ref[0, dk] * gam
        s_ref[0, dk] = s
        return ks + bcast_row(k_ref, 0, dk) * s

    ks = lax.fori_loop(0, HEAD, first, zero)
    vn = head_row(beta_ref, 0) * (v_ref[0] - ks)

    for i in range(1, steps):
        gam = jnp.exp(head_row(g_ref, i))

        def mid(dk, carry, i=i, vn=vn, gam=gam):
            o, ks = carry
            s = s_ref[0, dk] + bcast_row(k_ref, i - 1, dk) * vn
            o = o + bcast_row(q_ref, i - 1, dk) * s
            s = s * gam
            s_ref[0, dk] = s
            return o, ks + bcast_row(k_ref, i, dk) * s

        o, ks = lax.fori_loop(0, HEAD, mid, (zero, zero))
        finish(i - 1, o)
        vn = head_row(beta_ref, i) * (v_ref[i] - ks)

    def last(dk, o):
        s = s_ref[0, dk] + bcast_row(k_ref, steps - 1, dk) * vn
        s_ref[0, dk] = s
        return o + bcast_row(q_ref, steps - 1, dk) * s

    finish(steps - 1, lax.fori_loop(0, HEAD, last, zero))


def _gdn_sample(qt, kt, vt, gt, bt, zt, nwb, s0t, layer):
    steps, _, bsz = qt.shape
    per_head = pl.BlockSpec((steps, HEAD, bsz), lambda h: (0, h, 0))
    small = pl.BlockSpec((steps, SUBLANE, bsz), lambda h: (0, 0, 0))
    state = pl.BlockSpec((1, HEAD, HEAD, bsz), lambda h: (h, 0, 0, 0))
    state_in = pl.BlockSpec((None, 1, HEAD, HEAD, bsz), lambda h: (layer, h, 0, 0, 0))
    return pl.pallas_call(
        functools.partial(_gdn_sample_kernel, steps=steps),
        grid=(B_HEADS,),
        in_specs=[per_head, per_head, per_head, small, small, per_head,
                  pl.BlockSpec((HEAD, bsz), lambda h: (0, 0)), state_in],
        out_specs=[per_head, state],
        out_shape=[jax.ShapeDtypeStruct((steps, B_WIDTH, bsz), F32),
                   jax.ShapeDtypeStruct((B_HEADS, HEAD, HEAD, bsz), F32)],
        compiler_params=_params(1),
        name="gdn_sample",
    )(qt, kt, vt, gt, bt, zt, nwb, s0t)


def _attn_prompt_kernel(q_ref, k_ref, v_ref, o_ref, lse_ref, *, dil, nb, chains_per_stage):
    n = ATTN_BLOCK
    scale = HEAD ** -0.5
    lane = lax.broadcasted_iota(jnp.int32, (n, GROUP_W), 1)
    head0 = lane < HEAD
    qi = lax.broadcasted_iota(jnp.int32, (n, n), 0)
    kj = lax.broadcasted_iota(jnp.int32, (n, n), 1)
    mask_c = kj <= qi
    mask_p = kj >= qi

    def rows(r, i):
        if dil == 1:
            return pl.ds(i * n, n)
        return pl.ds(r + i * n * dil, n, stride=dil)

    blocks = [(r, i) for r in range(dil) for i in range(nb)]
    for first in range(0, len(blocks), chains_per_stage):
        stage = []
        for r, i in blocks[first:first + chains_per_stage]:
            q = q_ref[rows(r, i), :]
            for sel in (head0, jnp.logical_not(head0)):
                stage.append(dict(r=r, i=i, qh=jnp.where(sel, q, 0.0)))
        for ch in stage:
            r, i = ch["r"], ch["i"]
            ch["sc"] = _dot_nt(ch["qh"], k_ref[rows(r, i), :])
            if i > 0:
                ch["sp"] = _dot_nt(ch["qh"], k_ref[rows(r, i - 1), :])
        for ch in stage:
            sc = jnp.where(mask_c, ch["sc"] * scale, NEG)
            mx = jnp.max(sc, axis=-1, keepdims=True)
            if ch["i"] > 0:
                sp = jnp.where(mask_p, ch["sp"] * scale, NEG)
                mx = jnp.maximum(mx, jnp.max(sp, axis=-1, keepdims=True))
                ch["pp"] = jnp.exp(sp - mx)
            ch["pc"] = jnp.exp(sc - mx)
            den = jnp.sum(ch["pc"], axis=-1, keepdims=True)
            if ch["i"] > 0:
                den = den + jnp.sum(ch["pp"], axis=-1, keepdims=True)
            ch["den"] = den
            ch["lse"] = mx + jnp.log(den)
        for ch in stage:
            r, i = ch["r"], ch["i"]
            acc = jnp.dot(ch["pc"], v_ref[rows(r, i), :], preferred_element_type=F32)
            if i > 0:
                acc = acc + jnp.dot(ch["pp"], v_ref[rows(r, i - 1), :], preferred_element_type=F32)
            ch["o"] = acc / ch["den"]
        for c0, c1 in zip(stage[0::2], stage[1::2]):
            r, i = c0["r"], c0["i"]
            o_ref[rows(r, i), :] = jnp.where(head0, c0["o"], c1["o"])
            lse_ref[rows(r, i), :] = jnp.where(head0, jnp.broadcast_to(c0["lse"], (n, GROUP_W)),
                                               jnp.broadcast_to(c1["lse"], (n, GROUP_W)))


def _attn_prompt(qrot, kv, group, batch, chains_per_stage=4):
    _, dil = GROUPS[group]
    m = qrot.shape[0]
    t = m // batch
    nb = t // (dil * ATTN_BLOCK)
    seq = lambda col: pl.BlockSpec((t, GROUP_W), lambda b: (b, col))
    out = jax.ShapeDtypeStruct((m, GROUP_W), F32)
    return pl.pallas_call(
        functools.partial(_attn_prompt_kernel, dil=dil, nb=nb, chains_per_stage=chains_per_stage),
        grid=(batch,),
        in_specs=[seq(group), seq(0), seq(1)],
        out_specs=[seq(0), seq(0)],
        out_shape=[out, out],
        compiler_params=_params(1),
        name=f"attn_prompt_g{group}",
    )(qrot, kv, kv)


def _attn_sample_kernel(q_ref, n0_ref, n1_ref, n2_ref, c0_ref, c1_ref, c2_ref, o_ref, lse_ref, *, bt, steps):
    rows = 2 * steps
    scale = HEAD ** -0.5
    lane = lax.broadcasted_iota(jnp.int32, (rows, GROUP_W), 1)
    rix = lax.broadcasted_iota(jnp.int32, (rows, GROUP_W), 0)
    own_head = (lane < HEAD) == (rix < steps)
    nkey = lax.broadcasted_iota(jnp.int32, (rows, rows), 1)
    nqry = lax.broadcasted_iota(jnp.int32, (rows, rows), 0) % steps
    new_refs = (n0_ref, n1_ref, n2_ref)
    cache_refs = (c0_ref, c1_ref, c2_ref)

    head0 = lane[:steps] < HEAD
    chains = []
    for b in range(bt):
        q_all = q_ref[b]
        for g, (win, dil) in enumerate(GROUPS):
            qm = jnp.where(own_head, q_all[:, g * GROUP_W:(g + 1) * GROUP_W], 0.0)
            chains.append(dict(b=b, g=g, win=win, dil=dil, qm=qm, new=new_refs[g][b]))
    for ch in chains:
        kt = cache_refs[ch["g"]][0, ch["b"], 0].reshape(GROUP_W, ch["win"])
        ch["s_buf"] = jnp.dot(ch["qm"], kt, preferred_element_type=F32)
        ch["s_new"] = _dot_nt(ch["qm"], ch["new"][:, :GROUP_W])
    for ch in chains:
        win, dil = ch["win"], ch["dil"]
        pos = lax.broadcasted_iota(jnp.int32, (rows, win), 1)
        qry = lax.broadcasted_iota(jnp.int32, (rows, win), 0) % steps
        if dil == 1:
            ok = pos >= qry
            new_ok = jnp.logical_and(nkey < steps, nkey <= nqry)
        else:
            ok = (pos % dil) == qry
            new_ok = nkey == nqry
        s_buf = jnp.where(ok, ch["s_buf"] * scale, NEG)
        s_new = jnp.where(new_ok, ch["s_new"] * scale, NEG)
        mx = jnp.maximum(jnp.max(s_buf, axis=-1, keepdims=True), jnp.max(s_new, axis=-1, keepdims=True))
        ch["p_buf"] = jnp.exp(s_buf - mx)
        ch["p_new"] = jnp.exp(s_new - mx)
        ch["den"] = jnp.sum(ch["p_buf"], axis=-1, keepdims=True) + jnp.sum(ch["p_new"], axis=-1, keepdims=True)
        ch["mx"] = mx
    for ch in chains:
        vt = cache_refs[ch["g"]][0, ch["b"], 1].reshape(GROUP_W, ch["win"])
        ch["acc"] = jnp.dot(ch["p_new"], ch["new"][:, GROUP_W:], preferred_element_type=F32) + _dot_nt(ch["p_buf"], vt)
    for ch in chains:
        b, g = ch["b"], ch["g"]
        o_full = ch["acc"] / ch["den"]
        lse_full = jnp.broadcast_to(ch["mx"] + jnp.log(ch["den"]), (rows, GROUP_W))
        o_ref[b, :, g * GROUP_W:(g + 1) * GROUP_W] = jnp.where(head0, o_full[:steps], o_full[steps:])
        lse_ref[b, :, g * GROUP_W:(g + 1) * GROUP_W] = jnp.where(head0, lse_full[:steps], lse_full[steps:])


def _attn_sample(q2, new_kv, caches, layer, bt):
    bsz, rows, _ = q2.shape
    steps = rows // 2
    per_b = lambda w: pl.BlockSpec((bt, rows, w), lambda i: (i, 0, 0))
    cache_spec = lambda win: pl.BlockSpec((1, bt, 2, HPG, HEAD, win), lambda i: (layer, i, 0, 0, 0, 0))
    out_spec = pl.BlockSpec((bt, steps, C_WIDTH), lambda i: (i, 0, 0))
    out = jax.ShapeDtypeStruct((bsz, steps, C_WIDTH), F32)
    return pl.pallas_call(
        functools.partial(_attn_sample_kernel, bt=bt, steps=steps),
        grid=(bsz // bt,),
        in_specs=[per_b(C_WIDTH), per_b(2 * GROUP_W), per_b(2 * GROUP_W), per_b(2 * GROUP_W)]
        + [cache_spec(win) for win, _ in GROUPS],
        out_specs=[out_spec, out_spec],
        out_shape=[out, out],
        compiler_params=_params(1),
        name="attn_sample",
    )(q2, *new_kv, *caches)


def _outproj_kernel(ya_ref, yb_ref, o0_ref, o1_ref, o2_ref, l0_ref, l1_ref, l2_ref, cz_ref, x_ref, gate_ref,
                    w_ref, xo_ref):
    lses = (l0_ref[...], l1_ref[...], l2_ref[...])
    mx = jnp.maximum(jnp.maximum(lses[0], lses[1]), lses[2])
    es = [jnp.exp(l - mx) for l in lses]
    tot = es[0] + es[1] + es[2]
    cz = cz_ref[...]

    def mm(y, lo):
        return jnp.dot(y.astype(BF16), w_ref[lo:lo + y.shape[1], :], preferred_element_type=F32)

    out = mm(ya_ref[...], 0) + mm(yb_ref[...], A_WIDTH)
    for g, o_ref in enumerate((o0_ref, o1_ref, o2_ref)):
        yc = o_ref[...] * (es[g] / tot) * _silu(cz[:, g * GROUP_W:(g + 1) * GROUP_W])
        out = out + mm(yc, A_WIDTH + B_WIDTH + g * GROUP_W)
    xo_ref[...] = x_ref[...] + (1.0 + gate_ref[0]) * out


def _outproj(ya, yb, os_, lses, cz, x_rows, mod, w_out, layer, tm):
    m = x_rows.shape[0]
    nblk = m // tm
    nb_mod, r, _ = mod.shape
    tiles_per_mod = nblk // nb_mod
    row = lambda w: pl.BlockSpec((tm, w), lambda i: (i, 0))
    return pl.pallas_call(
        _outproj_kernel,
        grid=(nblk,),
        in_specs=[row(A_WIDTH), row(B_WIDTH)] + [row(GROUP_W)] * 6 + [row(C_WIDTH), row(D_MODEL),
                  pl.BlockSpec((1, r, D_MODEL), lambda i: (i // tiles_per_mod, 0, 2)),
                  pl.BlockSpec((None, D_MODEL, D_MODEL), lambda i: (layer, 0, 0))],
        out_specs=row(D_MODEL),
        out_shape=jax.ShapeDtypeStruct((m, D_MODEL), F32),
        compiler_params=_params(1),
        name="outproj",
    )(ya, yb, *os_, *lses, cz, x_rows, mod, w_out)


def _final_kernel(x_ref, sh_ref, sc_ref, nw_ref, y_ref):
    x = x_ref[...]
    var = jnp.mean(x * x, axis=-1, keepdims=True)
    y_ref[...] = x * lax.rsqrt(var + EPS) * nw_ref[...] * (1.0 + sc_ref[0]) + sh_ref[0]


def _final(x_rows, mod, norm_w, tm):
    m = x_rows.shape[0]
    nblk = m // tm
    nb_mod, r, _ = mod.shape
    tiles_per_mod = nblk // nb_mod
    row = pl.BlockSpec((tm, D_MODEL), lambda i: (i, 0))
    return pl.pallas_call(
        _final_kernel,
        grid=(nblk,),
        in_specs=[row,
                  pl.BlockSpec((1, r, D_MODEL), lambda i: (i // tiles_per_mod, 0, 0)),
                  pl.BlockSpec((1, r, D_MODEL), lambda i: (i // tiles_per_mod, 0, 1)),
                  pl.BlockSpec((1, D_MODEL), lambda i: (0, 0))],
        out_specs=row,
        out_shape=jax.ShapeDtypeStruct((m, D_MODEL), F32),
        compiler_params=_params(1),
        name="final_norm",
    )(x_rows, mod, mod, norm_w.reshape(1, D_MODEL))


def _rope_tables(pos):
    half = ROT_DIM // 2
    inv_freq = ROPE_THETA ** (-jnp.arange(half, dtype=F32) * (2.0 / ROT_DIM))
    ang = pos.astype(F32)[:, None] * inv_freq[None, :]
    cos, sin = jnp.cos(ang), jnp.sin(ang)
    rows = pos.shape[0]
    one = jnp.ones((rows, HEAD - ROT_DIM), F32)
    zero_r = jnp.zeros((rows, HEAD - ROT_DIM), F32)
    zero_h = jnp.zeros((rows, half), F32)
    cos_h = jnp.concatenate([cos, cos, one], axis=1)
    sa_h = jnp.concatenate([-sin, zero_h, zero_r], axis=1)
    sb_h = jnp.concatenate([zero_h, sin, zero_r], axis=1)
    return tuple(jnp.concatenate([t] * HPG, axis=1) for t in (cos_h, sa_h, sb_h))


def _permute_w_in(w_in):
    depth = w_in.shape[0]
    cols = lambda src: w_in[:, :, src[0]:src[1]]
    pad = jnp.zeros((depth, D_MODEL, COL_AB[1] - COL_AB[0] - (SRC_AB[1] - SRC_AB[0])), w_in.dtype)
    return jnp.concatenate([cols(SRC_A), cols(SRC_BQKV), cols(SRC_AB), pad, cols(SRC_BZ), cols(SRC_CZ),
                            cols(SRC_CQKV)], axis=2).astype(BF16)


def _segment_ones():
    idx = jnp.arange(B_WIDTH) // HEAD
    return (idx[:, None] == idx[None, :]).astype(F32)


def _prompt_trunk(x, mods, mod_final, wts, seg):
    batch, t, _ = x.shape
    depth = wts["w_in"].shape[0]
    m = batch * t
    tm = 512
    rows = x.reshape(m, D_MODEL)
    tabs = _rope_tables(jnp.arange(t))
    zeros_a = jnp.zeros((batch, A_CONV - 1, A_WIDTH), F32)
    zeros_b = jnp.zeros((batch, B_CONV - 1, B_QKV), F32)
    st_a, st_b, st_g, kvs = [], [], [], [[] for _ in GROUPS]
    for l in range(depth):
        mod = mods[l].reshape(batch, 1, 3 * D_MODEL)
        ya, q, k, v, g, beta, bz, qrot, kv0, kv1, kv2, cz, sa, sb = _inproj(
            rows, mod, zeros_a, zeros_b, tabs, wts, l, seg, seqs=batch, stride=1, tm=tm)
        yb, s_fin = _gdn_prompt(q, k, v, g, beta, bz, wts["gdn_norm_w"][l], batch)
        os_, lses = [], []
        for gi, kv in enumerate((kv0, kv1, kv2)):
            o, lse = _attn_prompt(qrot, kv, gi, batch)
            os_.append(o)
            lses.append(lse)
            win = min(GROUPS[gi][0], t)
            kvs[gi].append(kv.reshape(batch, t, 2, HPG, HEAD)[:, t - win:])
        rows = _outproj(ya, yb, os_, lses, cz, rows, mod, wts["w_out"], l, tm)
        st_a.append(sa)
        st_b.append(sb)
        st_g.append(s_fin)
    y = _final(rows, mod_final.reshape(batch, 1, 2 * D_MODEL), wts["final_norm_w"], tm).reshape(batch, t, D_MODEL)
    return y, jnp.stack(st_a), jnp.stack(st_b), jnp.stack(st_g), [jnp.stack(r) for r in kvs]


def _sample_trunk(x, mods, mod_final, state_a, state_b, state_g, caches, past_len, wts, seg):
    bsz, steps, _ = x.shape
    depth = wts["w_in"].shape[0]
    m = bsz * steps
    tm = bsz

    def time_major(a):
        return a.transpose(1, 0, 2).reshape(a.shape[1] * bsz, a.shape[2])

    def batch_major(a, n):
        return a.reshape(n, bsz, a.shape[1]).transpose(1, 0, 2)

    def lanes_batch(a, n):
        return a.reshape(n, bsz, a.shape[1]).transpose(0, 2, 1)

    rows = time_major(x)
    tabs = _rope_tables(jnp.repeat(past_len + jnp.arange(steps), bsz))
    cache_views = [c.transpose(0, 1, 3, 4, 5, 2) for c in caches]
    state_t = state_g.transpose(0, 2, 3, 4, 1)
    st_a, st_b, st_g, kvs = [], [], [], [[] for _ in GROUPS]
    for l in range(depth):
        mod = mods[l][None]
        buf_a = time_major(state_a[l])[None]
        buf_b = time_major(state_b[l])[None]
        ya, q, k, v, g, beta, bz, qrot, kv0, kv1, kv2, cz, sa, sb = _inproj(
            rows, mod, buf_a, buf_b, tabs, wts, l, seg, seqs=1, stride=bsz, tm=m)
        nwb = jnp.broadcast_to(wts["gdn_norm_w"][l][:, None], (HEAD, bsz))
        yt, s_t = _gdn_sample(lanes_batch(q, steps), lanes_batch(k, steps), lanes_batch(v, steps),
                              lanes_batch(g[:, :SUBLANE], steps), lanes_batch(beta[:, :SUBLANE], steps),
                              lanes_batch(bz, steps), nwb, state_t, l)
        yb = yt.transpose(0, 2, 1).reshape(m, B_WIDTH)
        dup = lambda a: jnp.concatenate([batch_major(a, steps)] * 2, axis=1)
        o_c, lse_c = _attn_sample(dup(qrot), [dup(kv) for kv in (kv0, kv1, kv2)], cache_views, l, 4)
        o_c, lse_c = time_major(o_c), time_major(lse_c)
        os_ = [o_c[:, gi * GROUP_W:(gi + 1) * GROUP_W] for gi in range(len(GROUPS))]
        lses = [lse_c[:, gi * GROUP_W:(gi + 1) * GROUP_W] for gi in range(len(GROUPS))]
        rows = _outproj(ya, yb, os_, lses, cz, rows, mod, wts["w_out"], l, tm)
        st_a.append(batch_major(sa[0], A_CONV - 1))
        st_b.append(batch_major(sb[0], B_CONV - 1))
        st_g.append(s_t.transpose(3, 0, 1, 2))
        for gi, kv in enumerate((kv0, kv1, kv2)):
            kvs[gi].append(batch_major(kv, steps).reshape(bsz, steps, 2, HPG, HEAD))
    y = batch_major(_final(rows, mod_final[None], wts["final_norm_w"], tm), steps)
    return y, jnp.stack(st_a), jnp.stack(st_b), jnp.stack(st_g), [jnp.stack(r) for r in kvs]


def kernel(x_prompt, x_sample, state_conv_a, state_conv_b, state_gdn, cache_kv_w128, cache_kv_w512,
           cache_kv_w2048, c_prompt, c_sample, w_in, w_out, w_ada, b_ada, norm_w, conv_a_w, conv_b_w,
           a_log, dt_bias, gdn_norm_w, final_norm_w, w_ada_final, b_ada_final):
    n_prompt = c_prompt.shape[0]
    c_all = jnp.concatenate([c_prompt, c_sample], axis=0)
    mods = _ada(c_all, w_ada, b_ada)
    mod_final = _ada(c_all, w_ada_final[None], b_ada_final[None])[0]
    wts = dict(w_in=_permute_w_in(w_in), w_out=w_out.astype(BF16), norm_w=norm_w[:, None, :], conv_a_w=conv_a_w,
               conv_b_w=conv_b_w, a_log=a_log, dt_bias=dt_bias, gdn_norm_w=gdn_norm_w,
               final_norm_w=final_norm_w)
    seg = _segment_ones()
    y_p, ca_p, cb_p, g_p, kv_p = _prompt_trunk(x_prompt, mods[:, :n_prompt], mod_final[:n_prompt], wts, seg)
    y_s, ca_s, cb_s, g_s, kv_s = _sample_trunk(
        x_sample, mods[:, n_prompt:], mod_final[n_prompt:], state_conv_a, state_conv_b, state_gdn,
        (cache_kv_w128, cache_kv_w512, cache_kv_w2048), PAST_LEN, wts, seg)
    return (y_p, y_s, ca_p, ca_s, cb_p, cb_s, g_p, g_s,
            kv_p[0], kv_s[0], kv_p[1], kv_s[1], kv_p[2], kv_s[2])
```

```python
import functools
import math

import jax
import jax.numpy as jnp
from jax import lax
from jax.experimental import pallas as pl
from jax.experimental.pallas import tpu as pltpu

F32 = jnp.float32
BF16 = jnp.bfloat16
HIGHEST = lax.Precision.HIGHEST

D_MODEL = 1024
HEAD = 64
A_WIDTH = 256
A_CONV = 3
B_HEADS = 6
B_WIDTH = B_HEADS * HEAD
B_QKV = 3 * B_WIDTH
B_CONV = 4
GDN_CHUNK = 64
GROUPS = ((128, 1), (512, 4), (2048, 16))
HPG = 2
GROUP_W = HPG * HEAD
C_WIDTH = len(GROUPS) * GROUP_W
ATTN_BLOCK = 128
ROT_DIM = HEAD // 4
ROPE_THETA = 500000.0
PAST_LEN = 2048
EPS = 1e-6
NEG = -1e30
LANE = 128
SUBLANE = 8

COL_A = (0, 1024)
COL_BQKV = (1024, 2176)
COL_AB = (2176, 2304)
COL_BZ = (2304, 2688)
COL_CZ = (2688, 3072)
COL_CQ = (3072, 3456)
COL_CK = (3456, 3840)
COL_CV = (3840, 4224)
IN_PERM_WIDTH = 4224
SRC_A = (0, 1024)
SRC_BQKV = (1024, 2176)
SRC_BZ = (2176, 2560)
SRC_AB = (2560, 2572)
SRC_CQKV = (2572, 3724)
SRC_CZ = (3724, 4108)

VMEM_LIMIT_BYTES = 56 * 1024 * 1024


def _params(n_axes):
    return pltpu.CompilerParams(dimension_semantics=("arbitrary",) * n_axes,
                                vmem_limit_bytes=VMEM_LIMIT_BYTES)


def _round_up(x, m):
    return -(-x // m) * m


def _silu(x):
    return x * jax.nn.sigmoid(x)


def _dot_nt(a, b, **kw):
    return lax.dot_general(a, b, (((1,), (1,)), ((), ())), preferred_element_type=F32, **kw)


def _ada_kernel(c_ref, w_ref, b_ref, o_ref):
    c = c_ref[...].astype(BF16)
    w = w_ref[0].astype(BF16)
    o_ref[0] = jnp.dot(c, w, preferred_element_type=F32) + b_ref[0]


def _ada(c_all, w, b):
    n_layers, _, n = w.shape
    r = c_all.shape[0]
    tn = 1024
    return pl.pallas_call(
        _ada_kernel,
        grid=(n_layers, n // tn),
        in_specs=[pl.BlockSpec((r, D_MODEL), lambda l, j: (0, 0)),
                  pl.BlockSpec((1, D_MODEL, tn), lambda l, j: (l, 0, j)),
                  pl.BlockSpec((1, 1, tn), lambda l, j: (l, 0, j))],
        out_specs=pl.BlockSpec((1, r, tn), lambda l, j: (l, 0, j)),
        out_shape=jax.ShapeDtypeStruct((n_layers, r, n), F32),
        compiler_params=_params(2),
        name="ada",
    )(c_all, w, b.reshape(n_layers, 1, n))


def _inproj_kernel(x_ref, sh_ref, sc_ref, nw_ref, w_ref, cos_ref, sa_ref, sb_ref,
                   bufa_ref, bufb_ref, wa_ref, wb_ref, alog_ref, dtb_ref, seg_ref,
                   ya_ref, q_ref, k_ref, v_ref, g_ref, beta_ref, bz_ref, qrot_ref, kv0_ref, kv1_ref, kv2_ref, cz_ref,
                   sta_ref, stb_ref, xpa_ref, xpb_ref, *, stride, tm, tiles_per_seq):
    t = pl.program_id(0) % tiles_per_seq
    x = x_ref[...]
    reps = tm // sh_ref.shape[1] if sh_ref.shape[1] > 1 else 1
    tile_rows = lambda a: a if reps == 1 else jnp.concatenate([a] * reps, axis=0)
    var = jnp.mean(x * x, axis=-1, keepdims=True)
    hn = x * lax.rsqrt(var + EPS) * nw_ref[...]
    hn = hn * (1.0 + tile_rows(sc_ref[0])) + tile_rows(sh_ref[0])
    hb = hn.astype(BF16)

    def mm(lo, hi):
        return jnp.dot(hb, w_ref[:, lo:hi], preferred_element_type=F32)

    def halo_rows(taps):
        halo = (taps - 1) * stride
        return halo, _round_up(halo, SUBLANE)

    def load_halo(xp_ref, buf_ref, taps):
        halo, x0 = halo_rows(taps)

        @pl.when(t == 0)
        def _():
            xp_ref[x0 - halo:x0, :] = buf_ref[0]

        @pl.when(t > 0)
        def _():
            xp_ref[x0 - halo:x0, :] = xp_ref[x0 + tm - halo:x0 + tm, :]

    def conv_cols(xp_ref, x_new, w_ref, taps, lo, hi):
        halo, x0 = halo_rows(taps)
        xp_ref[x0:x0 + tm, lo:hi] = x_new
        y = xp_ref[x0 - halo:x0 - halo + tm, lo:hi] * w_ref[0:1, lo:hi]
        for j in range(1, taps):
            r0 = x0 - halo + j * stride
            y = y + xp_ref[r0:r0 + tm, lo:hi] * w_ref[j:j + 1, lo:hi]
        return y

    def store_tail(xp_ref, st_ref, taps):
        halo, x0 = halo_rows(taps)
        st_ref[0] = xp_ref[x0 + tm - halo:x0 + tm, :]

    load_halo(xpa_ref, bufa_ref, A_CONV)
    load_halo(xpb_ref, bufb_ref, B_CONV)
    seg = seg_ref[...].astype(BF16)
    cos = cos_ref[...]
    sa = sa_ref[...]
    sb = sb_ref[...]

    def rope(u):
        return u * cos + pltpu.roll(u, LANE - ROT_DIM // 2, 1) * sa + pltpu.roll(u, ROT_DIM // 2, 1) * sb

    def branch_b(raw, lo, hi):
        act = _silu(conv_cols(xpb_ref, raw, wb_ref, B_CONV, lo, hi))
        if lo < 2 * B_WIDTH:
            sq = act * act
            hi16 = sq.astype(BF16)
            lo16 = (sq - hi16.astype(F32)).astype(BF16)
            blk = seg[:hi - lo, :hi - lo]
            ss = jnp.dot(hi16, blk, preferred_element_type=F32) + jnp.dot(lo16, blk, preferred_element_type=F32)
            act = act * lax.rsqrt(ss + 1e-6)
        for out_ref, base, scale in ((q_ref, 0, HEAD ** -0.5), (k_ref, B_WIDTH, None), (v_ref, 2 * B_WIDTH, None)):
            c0, c1 = max(lo, base), min(hi, base + B_WIDTH)
            if c0 < c1:
                piece = act[:, c0 - lo:c1 - lo]
                out_ref[:, c0 - base:c1 - base] = piece if scale is None else piece * scale

    def use_b_tail(raw):
        n_v = B_QKV - 4 * wide
        branch_b(raw[:, :n_v], 4 * wide, B_QKV)
        ab = raw[:, n_v:]
        z = ab + dtb_ref[...]
        softplus = jnp.maximum(z, 0.0) + jnp.log1p(jnp.exp(-jnp.abs(z)))
        g_ref[...] = -jnp.exp(alog_ref[...]) * softplus
        beta_ref[...] = jax.nn.sigmoid(pltpu.roll(ab, LANE - B_HEADS, 1))

    def use_gates(raw):
        bz_ref[...] = raw[:, :B_WIDTH]
        cz_ref[...] = raw[:, B_WIDTH:]

    def use_c(raw):
        for g, kv_ref in enumerate((kv0_ref, kv1_ref, kv2_ref)):
            lo, hi = g * GROUP_W, (g + 1) * GROUP_W
            qrot_ref[:, lo:hi] = rope(raw[:, lo:hi])
            kv_ref[:, :GROUP_W] = rope(raw[:, C_WIDTH + lo:C_WIDTH + hi])
            kv_ref[:, GROUP_W:] = raw[:, 2 * C_WIDTH + lo:2 * C_WIDTH + hi]

    conv_a = []

    def use_a_in(raw):
        conv_a.append(conv_cols(xpa_ref, raw[:, A_WIDTH:] * raw[:, :A_WIDTH], wa_ref, A_CONV, 0, A_WIDTH))

    def use_a_out(raw):
        ya_ref[...] = raw[:, :A_WIDTH] * conv_a[0] * _silu(raw[:, A_WIDTH:])

    wide = 2 * LANE
    b0, a0 = COL_BQKV[0], COL_A[0]
    b_chunk = lambda c: ((b0 + c * wide, b0 + (c + 1) * wide), lambda raw: branch_b(raw, c * wide, (c + 1) * wide))
    jobs = [b_chunk(0), ((COL_BZ[0], COL_CZ[1]), use_gates), b_chunk(1), ((COL_CQ[0], COL_CV[1]), use_c),
            b_chunk(2), ((a0, a0 + 2 * A_WIDTH), use_a_in), b_chunk(3),
            ((a0 + 2 * A_WIDTH, a0 + 4 * A_WIDTH), use_a_out), ((b0 + 4 * wide, COL_AB[1]), use_b_tail)]
    pending = mm(*jobs[0][0])
    for i, (_, consume) in enumerate(jobs):
        ahead = mm(*jobs[i + 1][0]) if i + 1 < len(jobs) else None
        consume(pending)
        pending = ahead
    store_tail(xpb_ref, stb_ref, B_CONV)
    store_tail(xpa_ref, sta_ref, A_CONV)


def _inproj(x_rows, mod, buf_a, buf_b, rope_tabs, wts, layer, seg, *, seqs, stride, tm):
    m = x_rows.shape[0]
    nblk = m // tm
    tiles_per_seq = nblk // seqs
    nb_mod, r, _ = mod.shape
    tiles_per_mod = nblk // nb_mod
    nt_tab = rope_tabs[0].shape[0] // tm
    halo_a, halo_b = (A_CONV - 1) * stride, (B_CONV - 1) * stride
    row = lambda w: pl.BlockSpec((tm, w), lambda i: (i, 0))
    tab = pl.BlockSpec((tm, LANE), lambda i: (i % nt_tab, 0))
    full = lambda a: pl.BlockSpec(a.shape, lambda i: (0,) * a.ndim)
    per_layer = lambda a: pl.BlockSpec((None,) + a.shape[1:], lambda i: (layer,) + (0,) * (a.ndim - 1))
    per_seq = lambda h, w: pl.BlockSpec((1, h, w), lambda i: (i // tiles_per_seq, 0, 0))
    alog = jnp.zeros((1, LANE), F32).at[0, :B_HEADS].set(wts["a_log"][layer])
    dtb = jnp.zeros((1, LANE), F32).at[0, :B_HEADS].set(wts["dt_bias"][layer])
    out_w = (A_WIDTH, B_WIDTH, B_WIDTH, B_WIDTH, LANE, LANE, B_WIDTH, C_WIDTH, 2 * GROUP_W, 2 * GROUP_W, 2 * GROUP_W,
             C_WIDTH)
    return pl.pallas_call(
        functools.partial(_inproj_kernel, stride=stride, tm=tm, tiles_per_seq=tiles_per_seq),
        grid=(nblk,),
        in_specs=[row(D_MODEL),
                  pl.BlockSpec((1, r, D_MODEL), lambda i: (i // tiles_per_mod, 0, 0)),
                  pl.BlockSpec((1, r, D_MODEL), lambda i: (i // tiles_per_mod, 0, 1)),
                  per_layer(wts["norm_w"]), per_layer(wts["w_in"]), tab, tab, tab,
                  per_seq(halo_a, A_WIDTH), per_seq(halo_b, B_QKV),
                  per_layer(wts["conv_a_w"]), per_layer(wts["conv_b_w"]), full(alog), full(dtb), full(seg)],
        out_specs=[row(w) for w in out_w] + [per_seq(halo_a, A_WIDTH), per_seq(halo_b, B_QKV)],
        out_shape=[jax.ShapeDtypeStruct((m, w), F32) for w in out_w]
        + [jax.ShapeDtypeStruct((seqs, halo_a, A_WIDTH), F32), jax.ShapeDtypeStruct((seqs, halo_b, B_QKV), F32)],
        scratch_shapes=[pltpu.VMEM((_round_up(halo_a, SUBLANE) + tm, A_WIDTH), F32),
                        pltpu.VMEM((_round_up(halo_b, SUBLANE) + tm, B_QKV), F32)],
        compiler_params=_params(1),
        name="inproj",
    )(x_rows, mod, mod, wts["norm_w"], wts["w_in"], *rope_tabs, buf_a, buf_b,
      wts["conv_a_w"], wts["conv_b_w"], alog, dtb, seg)


def _gdn_chunk_kernel(q_ref, k_ref, v_ref, g_ref, beta_ref, tri_ref,
                      mt_ref, nn_ref, qe_ref, au_ref, gc_ref, *, chunks):
    n = GDN_CHUNK
    ri = lax.broadcasted_iota(jnp.int32, (n, n), 0)
    ci = lax.broadcasted_iota(jnp.int32, (n, n), 1)
    tri = ri >= ci
    strict = ri > ci
    eye = jnp.where(ri == ci, 1.0, 0.0).astype(F32)
    lane = lax.broadcasted_iota(jnp.int32, (n, B_WIDTH), 1)
    pairs = []
    for c in range(chunks):
        rows = slice(c * n, (c + 1) * n)
        q = q_ref[rows, :]
        k = k_ref[rows, :]
        v = v_ref[rows, :]
        beta = beta_ref[rows, :]
        gc = jnp.dot(tri_ref[...], g_ref[rows, :], preferred_element_type=F32, precision=HIGHEST)
        gc_ref[rows, :] = gc
        gct = gc.T
        kt = k.T
        for h in range(B_HEADS):
            lo, hi = h * HEAD, (h + 1) * HEAD
            qh, kh, vh = q[:, lo:hi], k[:, lo:hi], v[:, lo:hi]
            gcol = gc[:, h:h + 1]
            grow = gct[h:h + 1, :]
            bcol = beta[:, h:h + 1]
            decay = jnp.where(tri, jnp.exp(jnp.where(tri, gcol - grow, 0.0)), 0.0)
            kdt = kt[lo:hi, :] * jnp.exp(gcol[n - 1:n, :] - grow)
            in_head = jnp.logical_and(lane >= lo, lane < hi)
            kq = jnp.concatenate([jnp.where(in_head, k * bcol, 0.0), jnp.where(in_head, q, 0.0)], axis=0)
            pairs.append(dict(qh=qh, kh=kh, vh=vh, gcol=gcol, bcol=bcol, decay=decay, kb=kh * bcol, kdt=kdt,
                              kq=kq, kt=kt))
    for pr in pairs:
        both = jnp.dot(pr["kq"], pr["kt"], preferred_element_type=F32)
        pr["kk"], pr["qk"] = both[:n], both[n:]
    for pr in pairs:
        lmat = jnp.where(strict, pr["kk"] * pr["decay"], 0.0)
        pr["x"] = eye - lmat
        pr["p"] = lmat
    for pr in pairs:
        pr["p"] = jnp.dot(pr["p"].astype(BF16), pr["p"].astype(BF16), preferred_element_type=F32)
    for step in range(5):
        for pr in pairs:
            if step < 4:
                both = jnp.dot(jnp.concatenate([pr["p"], pr["x"]], axis=0).astype(BF16), pr["p"].astype(BF16),
                               preferred_element_type=F32)
                pr["x"] = pr["x"] + both[n:]
                pr["p"] = both[:n]
            else:
                pr["x"] = pr["x"] + jnp.dot(pr["x"].astype(BF16), pr["p"].astype(BF16), preferred_element_type=F32)
    for pr in pairs:
        pr["egc"] = jnp.exp(pr["gcol"])
        rhs = jnp.concatenate([pr["vh"] * pr["bcol"], pr["kb"] * pr["egc"]], axis=1)
        pr["uw"] = jnp.dot(pr["x"], rhs, preferred_element_type=F32)
    for pr in pairs:
        pr["auw"] = jnp.dot(pr["qk"] * pr["decay"], pr["uw"], preferred_element_type=F32)
        pr["kuw"] = jnp.dot(pr["kdt"], pr["uw"], preferred_element_type=F32)
    for c in range(chunks):
        rows = slice(c * n, (c + 1) * n)
        sel = pairs[c * B_HEADS:(c + 1) * B_HEADS]
        mt_ref[rows, :] = jnp.concatenate([-pr["kuw"][:, HEAD:] for pr in sel], axis=1)
        nn_ref[rows, :] = jnp.concatenate([pr["kuw"][:, :HEAD] for pr in sel], axis=1)
        qe_ref[rows, :] = jnp.concatenate([pr["qh"] * pr["egc"] - pr["auw"][:, HEAD:] for pr in sel], axis=1)
        au_ref[rows, :] = jnp.concatenate([pr["auw"][:, :HEAD] for pr in sel], axis=1)


def _gdn_scan_kernel(mt_ref, nn_ref, qe_ref, au_ref, gc_ref, z_ref, nw_ref, y_ref, sfin_ref, s_ref, *, batch):
    c = pl.program_id(0)
    n = GDN_CHUNK

    @pl.when(c == 0)
    def _():
        s_ref[...] = jnp.zeros_like(s_ref)

    nw = nw_ref[...]
    pairs = []
    for b in range(batch):
        mt, qe = mt_ref[b], qe_ref[b]
        for h in range(B_HEADS):
            lo, hi = h * HEAD, (h + 1) * HEAD
            pairs.append(dict(b=b, h=h, lhs=jnp.concatenate([mt[:, lo:hi], qe[:, lo:hi]], axis=0), s=s_ref[b, h]))
    for pr in pairs:
        pr["r"] = jnp.dot(pr["lhs"], pr["s"], preferred_element_type=F32)
    for b in range(batch):
        nn, au = nn_ref[b], au_ref[b]
        eg = jnp.exp(gc_ref[b, n - 1:n, :])
        outs = []
        for pr in pairs[b * B_HEADS:(b + 1) * B_HEADS]:
            h = pr["h"]
            lo, hi = h * HEAD, (h + 1) * HEAD
            s_new = pr["s"] * eg[:, h:h + 1] + pr["r"][:n] + nn[:, lo:hi]
            s_ref[b, h] = s_new
            sfin_ref[b, h] = s_new
            o = pr["r"][n:] + au[:, lo:hi]
            ms = jnp.mean(o * o, axis=-1, keepdims=True)
            outs.append(o * lax.rsqrt(ms + EPS) * nw)
        y_ref[b] = jnp.concatenate(outs, axis=1) * _silu(z_ref[b])


def _gdn_prompt(q, k, v, g, beta, z, norm_w, batch, chunks_per_step=4):
    m = q.shape[0]
    t = m // batch
    n = GDN_CHUNK
    nc = t // n
    rows = chunks_per_step * n
    row = lambda w: pl.BlockSpec((rows, w), lambda i: (i, 0))
    tri = jnp.tril(jnp.ones((n, n), F32))
    wide = jax.ShapeDtypeStruct((m, B_WIDTH), F32)
    mt, nn, qe, au, gc = pl.pallas_call(
        functools.partial(_gdn_chunk_kernel, chunks=chunks_per_step),
        grid=(m // rows,),
        in_specs=[row(B_WIDTH), row(B_WIDTH), row(B_WIDTH), row(LANE), row(LANE),
                  pl.BlockSpec((n, n), lambda i: (0, 0))],
        out_specs=[row(B_WIDTH)] * 4 + [row(LANE)],
        out_shape=[wide] * 4 + [jax.ShapeDtypeStruct((m, LANE), F32)],
        compiler_params=_params(1),
        name="gdn_chunk",
    )(q, k, v, g, beta, tri)
    per_chunk = lambda w: pl.BlockSpec((batch, n, w), lambda c: (0, c, 0))
    state = pl.BlockSpec((batch, B_HEADS, HEAD, HEAD), lambda c: (0, 0, 0, 0))
    view = lambda a: a.reshape(batch, t, a.shape[1])
    y, s_fin = pl.pallas_call(
        functools.partial(_gdn_scan_kernel, batch=batch),
        grid=(nc,),
        in_specs=[per_chunk(B_WIDTH)] * 4 + [per_chunk(LANE), per_chunk(B_WIDTH),
                                             pl.BlockSpec((1, HEAD), lambda c: (0, 0))],
        out_specs=[per_chunk(B_WIDTH), state],
        out_shape=[jax.ShapeDtypeStruct((batch, t, B_WIDTH), F32),
                   jax.ShapeDtypeStruct((batch, B_HEADS, HEAD, HEAD), F32)],
        scratch_shapes=[pltpu.VMEM((batch, B_HEADS, HEAD, HEAD), F32)],
        compiler_params=_params(1),
        name="gdn_scan",
    )(view(mt), view(nn), view(qe), view(au), view(gc), view(z), norm_w.reshape(1, HEAD))
    return y.reshape(m, B_WIDTH), s_fin


def _gdn_sample_kernel(q_ref, k_ref, v_ref, g_ref, beta_ref, z_ref, nw_ref, s0_ref, y_ref, s_ref, *, steps):
    h = pl.program_id(0)
    bsz = s_ref.shape[-1]
    zero = jnp.zeros((HEAD, bsz), F32)

    def bcast_row(ref, i, dk):
        return jnp.broadcast_to(ref[i, pl.ds(dk, 1), :], (HEAD, bsz))

    def head_row(ref, i):
        return ref[i, pl.ds(h, 1), :]

    def finish(i, o):
        ms = jnp.mean(o * o, axis=0, keepdims=True)
        y_ref[i] = o * lax.rsqrt(ms + EPS) * nw_ref[...] * _silu(z_ref[i])

    gam = jnp.exp(head_row(g_ref, 0))

    def first(dk, ks):
        s = s0_ref[0, dk] * gam
        s_ref[0, dk] = s
        return ks + bcast_row(k_ref, 0, dk) * s

    ks = lax.fori_loop(0, HEAD, first, zero)
    vn = head_row(beta_ref, 0) * (v_ref[0] - ks)

    for i in range(1, steps):
        gam = jnp.exp(head_row(g_ref, i))

        def mid(dk, carry, i=i, vn=vn, gam=gam):
            o, ks = carry
            s = s_ref[0, dk] + bcast_row(k_ref, i - 1, dk) * vn
            o = o + bcast_row(q_ref, i - 1, dk) * s
            s = s * gam
            s_ref[0, dk] = s
            return o, ks + bcast_row(k_ref, i, dk) * s

        o, ks = lax.fori_loop(0, HEAD, mid, (zero, zero))
        finish(i - 1, o)
        vn = head_row(beta_ref, i) * (v_ref[i] - ks)

    def last(dk, o):
        s = s_ref[0, dk] + bcast_row(k_ref, steps - 1, dk) * vn
        s_ref[0, dk] = s
        return o + bcast_row(q_ref, steps - 1, dk) * s

    finish(steps - 1, lax.fori_loop(0, HEAD, last, zero))


def _gdn_sample(qt, kt, vt, gt, bt, zt, nwb, s0t, layer):
    steps, _, bsz = qt.shape
    per_head = pl.BlockSpec((steps, HEAD, bsz), lambda h: (0, h, 0))
    small = pl.BlockSpec((steps, SUBLANE, bsz), lambda h: (0, 0, 0))
    state = pl.BlockSpec((1, HEAD, HEAD, bsz), lambda h: (h, 0, 0, 0))
    state_in = pl.BlockSpec((None, 1, HEAD, HEAD, bsz), lambda h: (layer, h, 0, 0, 0))
    return pl.pallas_call(
        functools.partial(_gdn_sample_kernel, steps=steps),
        grid=(B_HEADS,),
        in_specs=[per_head, per_head, per_head, small, small, per_head,
                  pl.BlockSpec((HEAD, bsz), lambda h: (0, 0)), state_in],
        out_specs=[per_head, state],
        out_shape=[jax.ShapeDtypeStruct((steps, B_WIDTH, bsz), F32),
                   jax.ShapeDtypeStruct((B_HEADS, HEAD, HEAD, bsz), F32)],
        compiler_params=_params(1),
        name="gdn_sample",
    )(qt, kt, vt, gt, bt, zt, nwb, s0t)


def _attn_prompt_kernel(q_ref, k_ref, v_ref, o_ref, lse_ref, *, dil, nb, chains_per_stage):
    n = ATTN_BLOCK
    scale = HEAD ** -0.5
    lane = lax.broadcasted_iota(jnp.int32, (n, GROUP_W), 1)
    head0 = lane < HEAD
    qi = lax.broadcasted_iota(jnp.int32, (n, n), 0)
    kj = lax.broadcasted_iota(jnp.int32, (n, n), 1)
    mask_c = kj <= qi
    mask_p = kj >= qi

    def rows(r, i):
        if dil == 1:
            return pl.ds(i * n, n)
        return pl.ds(r + i * n * dil, n, stride=dil)

    blocks = [(r, i) for r in range(dil) for i in range(nb)]
    for first in range(0, len(blocks), chains_per_stage):
        stage = []
        for r, i in blocks[first:first + chains_per_stage]:
            q = q_ref[rows(r, i), :]
            for sel in (head0, jnp.logical_not(head0)):
                stage.append(dict(r=r, i=i, qh=jnp.where(sel, q, 0.0)))
        for ch in stage:
            r, i = ch["r"], ch["i"]
            ch["sc"] = _dot_nt(ch["qh"], k_ref[rows(r, i), :])
            if i > 0:
                ch["sp"] = _dot_nt(ch["qh"], k_ref[rows(r, i - 1), :])
        for ch in stage:
            sc = jnp.where(mask_c, ch["sc"] * scale, NEG)
            mx = jnp.max(sc, axis=-1, keepdims=True)
            if ch["i"] > 0:
                sp = jnp.where(mask_p, ch["sp"] * scale, NEG)
                mx = jnp.maximum(mx, jnp.max(sp, axis=-1, keepdims=True))
                ch["pp"] = jnp.exp(sp - mx)
            ch["pc"] = jnp.exp(sc - mx)
            den = jnp.sum(ch["pc"], axis=-1, keepdims=True)
            if ch["i"] > 0:
                den = den + jnp.sum(ch["pp"], axis=-1, keepdims=True)
            ch["den"] = den
            ch["lse"] = mx + jnp.log(den)
        for ch in stage:
            r, i = ch["r"], ch["i"]
            acc = jnp.dot(ch["pc"], v_ref[rows(r, i), :], preferred_element_type=F32)
            if i > 0:
                acc = acc + jnp.dot(ch["pp"], v_ref[rows(r, i - 1), :], preferred_element_type=F32)
            ch["o"] = acc / ch["den"]
        for c0, c1 in zip(stage[0::2], stage[1::2]):
            r, i = c0["r"], c0["i"]
            o_ref[rows(r, i), :] = jnp.where(head0, c0["o"], c1["o"])
            lse_ref[rows(r, i), :] = jnp.where(head0, jnp.broadcast_to(c0["lse"], (n, GROUP_W)),
                                               jnp.broadcast_to(c1["lse"], (n, GROUP_W)))


def _attn_prompt(qrot, kv, group, batch, chains_per_stage=4):
    _, dil = GROUPS[group]
    m = qrot.shape[0]
    t = m // batch
    nb = t // (dil * ATTN_BLOCK)
    seq = lambda col: pl.BlockSpec((t, GROUP_W), lambda b: (b, col))
    out = jax.ShapeDtypeStruct((m, GROUP_W), F32)
    return pl.pallas_call(
        functools.partial(_attn_prompt_kernel, dil=dil, nb=nb, chains_per_stage=chains_per_stage),
        grid=(batch,),
        in_specs=[seq(group), seq(0), seq(1)],
        out_specs=[seq(0), seq(0)],
        out_shape=[out, out],
        compiler_params=_params(1),
        name=f"attn_prompt_g{group}",
    )(qrot, kv, kv)


def _attn_sample_kernel(q_ref, n0_ref, n1_ref, n2_ref, c0_ref, c1_ref, c2_ref, o_ref, lse_ref, *, bt, steps):
    rows = 2 * steps
    scale = HEAD ** -0.5
    lane = lax.broadcasted_iota(jnp.int32, (rows, GROUP_W), 1)
    rix = lax.broadcasted_iota(jnp.int32, (rows, GROUP_W), 0)
    own_head = (lane < HEAD) == (rix < steps)
    nkey = lax.broadcasted_iota(jnp.int32, (rows, rows), 1)
    nqry = lax.broadcasted_iota(jnp.int32, (rows, rows), 0) % steps
    new_refs = (n0_ref, n1_ref, n2_ref)
    cache_refs = (c0_ref, c1_ref, c2_ref)

    head0 = lane[:steps] < HEAD
    chains = []
    for b in range(bt):
        q_all = q_ref[b]
        for g, (win, dil) in enumerate(GROUPS):
            qm = jnp.where(own_head, q_all[:, g * GROUP_W:(g + 1) * GROUP_W], 0.0)
            chains.append(dict(b=b, g=g, win=win, dil=dil, qm=qm, new=new_refs[g][b]))
    for ch in chains:
        kt = cache_refs[ch["g"]][0, ch["b"], 0].reshape(GROUP_W, ch["win"])
        ch["s_buf"] = jnp.dot(ch["qm"], kt, preferred_element_type=F32)
        ch["s_new"] = _dot_nt(ch["qm"], ch["new"][:, :GROUP_W])
    for ch in chains:
        win, dil = ch["win"], ch["dil"]
        pos = lax.broadcasted_iota(jnp.int32, (rows, win), 1)
        qry = lax.broadcasted_iota(jnp.int32, (rows, win), 0) % steps
        if dil == 1:
            ok = pos >= qry
            new_ok = jnp.logical_and(nkey < steps, nkey <= nqry)
        else:
            ok = (pos % dil) == qry
            new_ok = nkey == nqry
        s_buf = jnp.where(ok, ch["s_buf"] * scale, NEG)
        s_new = jnp.where(new_ok, ch["s_new"] * scale, NEG)
        mx = jnp.maximum(jnp.max(s_buf, axis=-1, keepdims=True), jnp.max(s_new, axis=-1, keepdims=True))
        ch["p_buf"] = jnp.exp(s_buf - mx)
        ch["p_new"] = jnp.exp(s_new - mx)
        ch["den"] = jnp.sum(ch["p_buf"], axis=-1, keepdims=True) + jnp.sum(ch["p_new"], axis=-1, keepdims=True)
        ch["mx"] = mx
    for ch in chains:
        vt = cache_refs[ch["g"]][0, ch["b"], 1].reshape(GROUP_W, ch["win"])
        ch["acc"] = jnp.dot(ch["p_new"], ch["new"][:, GROUP_W:], preferred_element_type=F32) + _dot_nt(ch["p_buf"], vt)
    for ch in chains:
        b, g = ch["b"], ch["g"]
        o_full = ch["acc"] / ch["den"]
        lse_full = jnp.broadcast_to(ch["mx"] + jnp.log(ch["den"]), (rows, GROUP_W))
        o_ref[b, :, g * GROUP_W:(g + 1) * GROUP_W] = jnp.where(head0, o_full[:steps], o_full[steps:])
        lse_ref[b, :, g * GROUP_W:(g + 1) * GROUP_W] = jnp.where(head0, lse_full[:steps], lse_full[steps:])


def _attn_sample(q2, new_kv, caches, layer, bt):
    bsz, rows, _ = q2.shape
    steps = rows // 2
    per_b = lambda w: pl.BlockSpec((bt, rows, w), lambda i: (i, 0, 0))
    cache_spec = lambda win: pl.BlockSpec((1, bt, 2, HPG, HEAD, win), lambda i: (layer, i, 0, 0, 0, 0))
    out_spec = pl.BlockSpec((bt, steps, C_WIDTH), lambda i: (i, 0, 0))
    out = jax.ShapeDtypeStruct((bsz, steps, C_WIDTH), F32)
    return pl.pallas_call(
        functools.partial(_attn_sample_kernel, bt=bt, steps=steps),
        grid=(bsz // bt,),
        in_specs=[per_b(C_WIDTH), per_b(2 * GROUP_W), per_b(2 * GROUP_W), per_b(2 * GROUP_W)]
        + [cache_spec(win) for win, _ in GROUPS],
        out_specs=[out_spec, out_spec],
        out_shape=[out, out],
        compiler_params=_params(1),
        name="attn_sample",
    )(q2, *new_kv, *caches)


def _outproj_kernel(ya_ref, yb_ref, o0_ref, o1_ref, o2_ref, l0_ref, l1_ref, l2_ref, cz_ref, x_ref, gate_ref,
                    w_ref, xo_ref):
    lses = (l0_ref[...], l1_ref[...], l2_ref[...])
    mx = jnp.maximum(jnp.maximum(lses[0], lses[1]), lses[2])
    es = [jnp.exp(l - mx) for l in lses]
    tot = es[0] + es[1] + es[2]
    cz = cz_ref[...]

    def mm(y, lo):
        return jnp.dot(y.astype(BF16), w_ref[lo:lo + y.shape[1], :], preferred_element_type=F32)

    out = mm(ya_ref[...], 0) + mm(yb_ref[...], A_WIDTH)
    for g, o_ref in enumerate((o0_ref, o1_ref, o2_ref)):
        yc = o_ref[...] * (es[g] / tot) * _silu(cz[:, g * GROUP_W:(g + 1) * GROUP_W])
        out = out + mm(yc, A_WIDTH + B_WIDTH + g * GROUP_W)
    xo_ref[...] = x_ref[...] + (1.0 + gate_ref[0]) * out


def _outproj(ya, yb, os_, lses, cz, x_rows, mod, w_out, layer, tm):
    m = x_rows.shape[0]
    nblk = m // tm
    nb_mod, r, _ = mod.shape
    tiles_per_mod = nblk // nb_mod
    row = lambda w: pl.BlockSpec((tm, w), lambda i: (i, 0))
    return pl.pallas_call(
        _outproj_kernel,
        grid=(nblk,),
        in_specs=[row(A_WIDTH), row(B_WIDTH)] + [row(GROUP_W)] * 6 + [row(C_WIDTH), row(D_MODEL),
                  pl.BlockSpec((1, r, D_MODEL), lambda i: (i // tiles_per_mod, 0, 2)),
                  pl.BlockSpec((None, D_MODEL, D_MODEL), lambda i: (layer, 0, 0))],
        out_specs=row(D_MODEL),
        out_shape=jax.ShapeDtypeStruct((m, D_MODEL), F32),
        compiler_params=_params(1),
        name="outproj",
    )(ya, yb, *os_, *lses, cz, x_rows, mod, w_out)


def _final_kernel(x_ref, sh_ref, sc_ref, nw_ref, y_ref):
    x = x_ref[...]
    var = jnp.mean(x * x, axis=-1, keepdims=True)
    y_ref[...] = x * lax.rsqrt(var + EPS) * nw_ref[...] * (1.0 + sc_ref[0]) + sh_ref[0]


def _final(x_rows, mod, norm_w, tm):
    m = x_rows.shape[0]
    nblk = m // tm
    nb_mod, r, _ = mod.shape
    tiles_per_mod = nblk // nb_mod
    row = pl.BlockSpec((tm, D_MODEL), lambda i: (i, 0))
    return pl.pallas_call(
        _final_kernel,
        grid=(nblk,),
        in_specs=[row,
                  pl.BlockSpec((1, r, D_MODEL), lambda i: (i // tiles_per_mod, 0, 0)),
                  pl.BlockSpec((1, r, D_MODEL), lambda i: (i // tiles_per_mod, 0, 1)),
                  pl.BlockSpec((1, D_MODEL), lambda i: (0, 0))],
        out_specs=row,
        out_shape=jax.ShapeDtypeStruct((m, D_MODEL), F32),
        compiler_params=_params(1),
        name="final_norm",
    )(x_rows, mod, mod, norm_w.reshape(1, D_MODEL))


def _rope_tables(pos):
    half = ROT_DIM // 2
    inv_freq = ROPE_THETA ** (-jnp.arange(half, dtype=F32) * (2.0 / ROT_DIM))
    ang = pos.astype(F32)[:, None] * inv_freq[None, :]
    cos, sin = jnp.cos(ang), jnp.sin(ang)
    rows = pos.shape[0]
    one = jnp.ones((rows, HEAD - ROT_DIM), F32)
    zero_r = jnp.zeros((rows, HEAD - ROT_DIM), F32)
    zero_h = jnp.zeros((rows, half), F32)
    cos_h = jnp.concatenate([cos, cos, one], axis=1)
    sa_h = jnp.concatenate([-sin, zero_h, zero_r], axis=1)
    sb_h = jnp.concatenate([zero_h, sin, zero_r], axis=1)
    return tuple(jnp.concatenate([t] * HPG, axis=1) for t in (cos_h, sa_h, sb_h))


def _permute_w_in(w_in):
    depth = w_in.shape[0]
    cols = lambda src: w_in[:, :, src[0]:src[1]]
    pad = jnp.zeros((depth, D_MODEL, COL_AB[1] - COL_AB[0] - (SRC_AB[1] - SRC_AB[0])), w_in.dtype)
    return jnp.concatenate([cols(SRC_A), cols(SRC_BQKV), cols(SRC_AB), pad, cols(SRC_BZ), cols(SRC_CZ),
                            cols(SRC_CQKV)], axis=2).astype(BF16)


def _segment_ones():
    idx = jnp.arange(B_WIDTH) // HEAD
    return (idx[:, None] == idx[None, :]).astype(F32)


def _prompt_trunk(x, mods, mod_final, wts, seg):
    batch, t, _ = x.shape
    depth = wts["w_in"].shape[0]
    m = batch * t
    tm = 512
    rows = x.reshape(m, D_MODEL)
    tabs = _rope_tables(jnp.arange(t))
    zeros_a = jnp.zeros((batch, A_CONV - 1, A_WIDTH), F32)
    zeros_b = jnp.zeros((batch, B_CONV - 1, B_QKV), F32)
    st_a, st_b, st_g, kvs = [], [], [], [[] for _ in GROUPS]
    for l in range(depth):
        mod = mods[l].reshape(batch, 1, 3 * D_MODEL)
        ya, q, k, v, g, beta, bz, qrot, kv0, kv1, kv2, cz, sa, sb = _inproj(
            rows, mod, zeros_a, zeros_b, tabs, wts, l, seg, seqs=batch, stride=1, tm=tm)
        yb, s_fin = _gdn_prompt(q, k, v, g, beta, bz, wts["gdn_norm_w"][l], batch)
        os_, lses = [], []
        for gi, kv in enumerate((kv0, kv1, kv2)):
            o, lse = _attn_prompt(qrot, kv, gi, batch)
            os_.append(o)
            lses.append(lse)
            win = min(GROUPS[gi][0], t)
            kvs[gi].append(kv.reshape(batch, t, 2, HPG, HEAD)[:, t - win:])
        rows = _outproj(ya, yb, os_, lses, cz, rows, mod, wts["w_out"], l, tm)
        st_a.append(sa)
        st_b.append(sb)
        st_g.append(s_fin)
    y = _final(rows, mod_final.reshape(batch, 1, 2 * D_MODEL), wts["final_norm_w"], tm).reshape(batch, t, D_MODEL)
    return y, jnp.stack(st_a), jnp.stack(st_b), jnp.stack(st_g), [jnp.stack(r) for r in kvs]


def _sample_trunk(x, mods, mod_final, state_a, state_b, state_g, caches, past_len, wts, seg):
    bsz, steps, _ = x.shape
    depth = wts["w_in"].shape[0]
    m = bsz * steps
    tm = bsz

    def time_major(a):
        return a.transpose(1, 0, 2).reshape(a.shape[1] * bsz, a.shape[2])

    def batch_major(a, n):
        return a.reshape(n, bsz, a.shape[1]).transpose(1, 0, 2)

    def lanes_batch(a, n):
        return a.reshape(n, bsz, a.shape[1]).transpose(0, 2, 1)

    rows = time_major(x)
    tabs = _rope_tables(jnp.repeat(past_len + jnp.arange(steps), bsz))
    cache_views = [c.transpose(0, 1, 3, 4, 5, 2) for c in caches]
    state_t = state_g.transpose(0, 2, 3, 4, 1)
    st_a, st_b, st_g, kvs = [], [], [], [[] for _ in GROUPS]
    for l in range(depth):
        mod = mods[l][None]
        buf_a = time_major(state_a[l])[None]
        buf_b = time_major(state_b[l])[None]
        ya, q, k, v, g, beta, bz, qrot, kv0, kv1, kv2, cz, sa, sb = _inproj(
            rows, mod, buf_a, buf_b, tabs, wts, l, seg, seqs=1, stride=bsz, tm=m)
        nwb = jnp.broadcast_to(wts["gdn_norm_w"][l][:, None], (HEAD, bsz))
        yt, s_t = _gdn_sample(lanes_batch(q, steps), lanes_batch(k, steps), lanes_batch(v, steps),
                              lanes_batch(g[:, :SUBLANE], steps), lanes_batch(beta[:, :SUBLANE], steps),
                              lanes_batch(bz, steps), nwb, state_t, l)
        yb = yt.transpose(0, 2, 1).reshape(m, B_WIDTH)
        dup = lambda a: jnp.concatenate([batch_major(a, steps)] * 2, axis=1)
        o_c, lse_c = _attn_sample(dup(qrot), [dup(kv) for kv in (kv0, kv1, kv2)], cache_views, l, 4)
        o_c, lse_c = time_major(o_c), time_major(lse_c)
        os_ = [o_c[:, gi * GROUP_W:(gi + 1) * GROUP_W] for gi in range(len(GROUPS))]
        lses = [lse_c[:, gi * GROUP_W:(gi + 1) * GROUP_W] for gi in range(len(GROUPS))]
        rows = _outproj(ya, yb, os_, lses, cz, rows, mod, wts["w_out"], l, tm)
        st_a.append(batch_major(sa[0], A_CONV - 1))
        st_b.append(batch_major(sb[0], B_CONV - 1))
        st_g.append(s_t.transpose(3, 0, 1, 2))
        for gi, kv in enumerate((kv0, kv1, kv2)):
            kvs[gi].append(batch_major(kv, steps).reshape(bsz, steps, 2, HPG, HEAD))
    y = batch_major(_final(rows, mod_final[None], wts["final_norm_w"], tm), steps)
    return y, jnp.stack(st_a), jnp.stack(st_b), jnp.stack(st_g), [jnp.stack(r) for r in kvs]


def kernel(x_prompt, x_sample, state_conv_a, state_conv_b, state_gdn, cache_kv_w128, cache_kv_w512,
           cache_kv_w2048, c_prompt, c_sample, w_in, w_out, w_ada, b_ada, norm_w, conv_a_w, conv_b_w,
           a_log, dt_bias, gdn_norm_w, final_norm_w, w_ada_final, b_ada_final):
    n_prompt = c_prompt.shape[0]
    c_all = jnp.concatenate([c_prompt, c_sample], axis=0)
    mods = _ada(c_all, w_ada, b_ada)
    mod_final = _ada(c_all, w_ada_final[None], b_ada_final[None])[0]
    wts = dict(w_in=_permute_w_in(w_in), w_out=w_out.astype(BF16), norm_w=norm_w[:, None, :], conv_a_w=conv_a_w,
               conv_b_w=conv_b_w, a_log=a_log, dt_bias=dt_bias, gdn_norm_w=gdn_norm_w,
               final_norm_w=final_norm_w)
    seg = _segment_ones()
    y_p, ca_p, cb_p, g_p, kv_p = _prompt_trunk(x_prompt, mods[:, :n_prompt], mod_final[:n_prompt], wts, seg)
    y_s, ca_s, cb_s, g_s, kv_s = _sample_trunk(
        x_sample, mods[:, n_prompt:], mod_final[n_prompt:], state_conv_a, state_conv_b, state_gdn,
        (cache_kv_w128, cache_kv_w512, cache_kv_w2048), PAST_LEN, wts, seg)
    return (y_p, y_s, ca_p, ca_s, cb_p, cb_s, g_p, g_s,
            kv_p[0], kv_s[0], kv_p[1], kv_s[1], kv_p[2], kv_s[2])
```

```python
import functools
import math

import jax
import jax.numpy as jnp
from jax import lax
from jax.experimental import pallas as pl
from jax.experimental.pallas import tpu as pltpu

F32 = jnp.float32
BF16 = jnp.bfloat16
HIGHEST = lax.Precision.HIGHEST

D_MODEL = 1024
HEAD = 64
A_WIDTH = 256
A_CONV = 3
B_HEADS = 6
B_WIDTH = B_HEADS * HEAD
B_QKV = 3 * B_WIDTH
B_CONV = 4
GDN_CHUNK = 64
GROUPS = ((128, 1), (512, 4), (2048, 16))
HPG = 2
GROUP_W = HPG * HEAD
C_WIDTH = len(GROUPS) * GROUP_W
ATTN_BLOCK = 128
ROT_DIM = HEAD // 4
ROPE_THETA = 500000.0
PAST_LEN = 2048
EPS = 1e-6
NEG = -1e30
LANE = 128
SUBLANE = 8

COL_A = (0, 1024)
COL_BQKV = (1024, 2176)
COL_AB = (2176, 2304)
COL_BZ = (2304, 2688)
COL_CZ = (2688, 3072)
COL_CQ = (3072, 3456)
COL_CK = (3456, 3840)
COL_CV = (3840, 4224)
IN_PERM_WIDTH = 4224
SRC_A = (0, 1024)
SRC_BQKV = (1024, 2176)
SRC_BZ = (2176, 2560)
SRC_AB = (2560, 2572)
SRC_CQKV = (2572, 3724)
SRC_CZ = (3724, 4108)

VMEM_LIMIT_BYTES = 56 * 1024 * 1024


def _params(n_axes):
    return pltpu.CompilerParams(dimension_semantics=("arbitrary",) * n_axes,
                                vmem_limit_bytes=VMEM_LIMIT_BYTES)


def _round_up(x, m):
    return -(-x // m) * m


def _silu(x):
    return x * jax.nn.sigmoid(x)


def _dot_nt(a, b, **kw):
    return lax.dot_general(a, b, (((1,), (1,)), ((), ())), preferred_element_type=F32, **kw)


def _ada_kernel(c_ref, w_ref, b_ref, o_ref):
    c = c_ref[...].astype(BF16)
    w = w_ref[0].astype(BF16)
    o_ref[0] = jnp.dot(c, w, preferred_element_type=F32) + b_ref[0]


def _ada(c_all, w, b):
    n_layers, _, n = w.shape
    r = c_all.shape[0]
    tn = 1024
    return pl.pallas_call(
        _ada_kernel,
        grid=(n_layers, n // tn),
        in_specs=[pl.BlockSpec((r, D_MODEL), lambda l, j: (0, 0)),
                  pl.BlockSpec((1, D_MODEL, tn), lambda l, j: (l, 0, j)),
                  pl.BlockSpec((1, 1, tn), lambda l, j: (l, 0, j))],
        out_specs=pl.BlockSpec((1, r, tn), lambda l, j: (l, 0, j)),
        out_shape=jax.ShapeDtypeStruct((n_layers, r, n), F32),
        compiler_params=_params(2),
        name="ada",
    )(c_all, w, b.reshape(n_layers, 1, n))


def _inproj_kernel(x_ref, sh_ref, sc_ref, nw_ref, w_ref, cos_ref, sa_ref, sb_ref,
                   bufa_ref, bufb_ref, wa_ref, wb_ref, alog_ref, dtb_ref, seg_ref,
                   ya_ref, q_ref, k_ref, v_ref, g_ref, beta_ref, bz_ref, qrot_ref, kv0_ref, kv1_ref, kv2_ref, cz_ref,
                   sta_ref, stb_ref, xpa_ref, xpb_ref, *, stride, tm, tiles_per_seq):
    t = pl.program_id(0) % tiles_per_seq
    x = x_ref[...]
    reps = tm // sh_ref.shape[1] if sh_ref.shape[1] > 1 else 1
    tile_rows = lambda a: a if reps == 1 else jnp.concatenate([a] * reps, axis=0)
    var = jnp.mean(x * x, axis=-1, keepdims=True)
    hn = x * lax.rsqrt(var + EPS) * nw_ref[...]
    hn = hn * (1.0 + tile_rows(sc_ref[0])) + tile_rows(sh_ref[0])
    hb = hn.astype(BF16)

    def mm(lo, hi):
        return jnp.dot(hb, w_ref[:, lo:hi], preferred_element_type=F32)

    def halo_rows(taps):
        halo = (taps - 1) * stride
        return halo, _round_up(halo, SUBLANE)

    def load_halo(xp_ref, buf_ref, taps):
        halo, x0 = halo_rows(taps)

        @pl.when(t == 0)
        def _():
            xp_ref[x0 - halo:x0, :] = buf_ref[0]

        @pl.when(t > 0)
        def _():
            xp_ref[x0 - halo:x0, :] = xp_ref[x0 + tm - halo:x0 + tm, :]

    def conv_cols(xp_ref, x_new, w_ref, taps, lo, hi):
        halo, x0 = halo_rows(taps)
        xp_ref[x0:x0 + tm, lo:hi] = x_new
        y = xp_ref[x0 - halo:x0 - halo + tm, lo:hi] * w_ref[0:1, lo:hi]
        for j in range(1, taps):
            r0 = x0 - halo + j * stride
            y = y + xp_ref[r0:r0 + tm, lo:hi] * w_ref[j:j + 1, lo:hi]
        return y

    def store_tail(xp_ref, st_ref, taps):
        halo, x0 = halo_rows(taps)
        st_ref[0] = xp_ref[x0 + tm - halo:x0 + tm, :]

    load_halo(xpa_ref, bufa_ref, A_CONV)
    load_halo(xpb_ref, bufb_ref, B_CONV)
    seg = seg_ref[...].astype(BF16)
    cos = cos_ref[...]
    sa = sa_ref[...]
    sb = sb_ref[...]

    def rope(u):
        return u * cos + pltpu.roll(u, LANE - ROT_DIM // 2, 1) * sa + pltpu.roll(u, ROT_DIM // 2, 1) * sb

    def branch_b(raw, lo, hi):
        act = _silu(conv_cols(xpb_ref, raw, wb_ref, B_CONV, lo, hi))
        if lo < 2 * B_WIDTH:
            sq = act * act
            hi16 = sq.astype(BF16)
            lo16 = (sq - hi16.astype(F32)).astype(BF16)
            blk = seg[:hi - lo, :hi - lo]
            ss = jnp.dot(hi16, blk, preferred_element_type=F32) + jnp.dot(lo16, blk, preferred_element_type=F32)
            act = act * lax.rsqrt(ss + 1e-6)
        for out_ref, base, scale in ((q_ref, 0, HEAD ** -0.5), (k_ref, B_WIDTH, None), (v_ref, 2 * B_WIDTH, None)):
            c0, c1 = max(lo, base), min(hi, base + B_WIDTH)
            if c0 < c1:
                piece = act[:, c0 - lo:c1 - lo]
                out_ref[:, c0 - base:c1 - base] = piece if scale is None else piece * scale

    def use_b_tail(raw):
        n_v = B_QKV - 4 * wide
        branch_b(raw[:, :n_v], 4 * wide, B_QKV)
        ab = raw[:, n_v:]
        z = ab + dtb_ref[...]
        softplus = jnp.maximum(z, 0.0) + jnp.log1p(jnp.exp(-jnp.abs(z)))
        g_ref[...] = -jnp.exp(alog_ref[...]) * softplus
        beta_ref[...] = jax.nn.sigmoid(pltpu.roll(ab, LANE - B_HEADS, 1))

    def use_gates(raw):
        bz_ref[...] = raw[:, :B_WIDTH]
        cz_ref[...] = raw[:, B_WIDTH:]

    def use_c(raw):
        for g, kv_ref in enumerate((kv0_ref, kv1_ref, kv2_ref)):
            lo, hi = g * GROUP_W, (g + 1) * GROUP_W
            qrot_ref[:, lo:hi] = rope(raw[:, lo:hi])
            kv_ref[:, :GROUP_W] = rope(raw[:, C_WIDTH + lo:C_WIDTH + hi])
            kv_ref[:, GROUP_W:] = raw[:, 2 * C_WIDTH + lo:2 * C_WIDTH + hi]

    conv_a = []

    def use_a_in(raw):
        conv_a.append(conv_cols(xpa_ref, raw[:, A_WIDTH:] * raw[:, :A_WIDTH], wa_ref, A_CONV, 0, A_WIDTH))

    def use_a_out(raw):
        ya_ref[...] = raw[:, :A_WIDTH] * conv_a[0] * _silu(raw[:, A_WIDTH:])

    wide = 2 * LANE
    b0, a0 = COL_BQKV[0], COL_A[0]
    b_chunk = lambda c: ((b0 + c * wide, b0 + (c + 1) * wide), lambda raw: branch_b(raw, c * wide, (c + 1) * wide))
    jobs = [b_chunk(0), ((COL_BZ[0], COL_CZ[1]), use_gates), b_chunk(1), ((COL_CQ[0], COL_CV[1]), use_c),
            b_chunk(2), ((a0, a0 + 2 * A_WIDTH), use_a_in), b_chunk(3),
            ((a0 + 2 * A_WIDTH, a0 + 4 * A_WIDTH), use_a_out), ((b0 + 4 * wide, COL_AB[1]), use_b_tail)]
    pending = mm(*jobs[0][0])
    for i, (_, consume) in enumerate(jobs):
        ahead = mm(*jobs[i + 1][0]) if i + 1 < len(jobs) else None
        consume(pending)
        pending = ahead
    store_tail(xpb_ref, stb_ref, B_CONV)
    store_tail(xpa_ref, sta_ref, A_CONV)


def _inproj(x_rows, mod, buf_a, buf_b, rope_tabs, wts, layer, seg, *, seqs, stride, tm):
    m = x_rows.shape[0]
    nblk = m // tm
    tiles_per_seq = nblk // seqs
    nb_mod, r, _ = mod.shape
    tiles_per_mod = nblk // nb_mod
    nt_tab = rope_tabs[0].shape[0] // tm
    halo_a, halo_b = (A_CONV - 1) * stride, (B_CONV - 1) * stride
    row = lambda w: pl.BlockSpec((tm, w), lambda i: (i, 0))
    tab = pl.BlockSpec((tm, LANE), lambda i: (i % nt_tab, 0))
    full = lambda a: pl.BlockSpec(a.shape, lambda i: (0,) * a.ndim)
    per_layer = lambda a: pl.BlockSpec((None,) + a.shape[1:], lambda i: (layer,) + (0,) * (a.ndim - 1))
    per_seq = lambda h, w: pl.BlockSpec((1, h, w), lambda i: (i // tiles_per_seq, 0, 0))
    alog = jnp.zeros((1, LANE), F32).at[0, :B_HEADS].set(wts["a_log"][layer])
    dtb = jnp.zeros((1, LANE), F32).at[0, :B_HEADS].set(wts["dt_bias"][layer])
    out_w = (A_WIDTH, B_WIDTH, B_WIDTH, B_WIDTH, LANE, LANE, B_WIDTH, C_WIDTH, 2 * GROUP_W, 2 * GROUP_W, 2 * GROUP_W,
             C_WIDTH)
    return pl.pallas_call(
        functools.partial(_inproj_kernel, stride=stride, tm=tm, tiles_per_seq=tiles_per_seq),
        grid=(nblk,),
        in_specs=[row(D_MODEL),
                  pl.BlockSpec((1, r, D_MODEL), lambda i: (i // tiles_per_mod, 0, 0)),
                  pl.BlockSpec((1, r, D_MODEL), lambda i: (i // tiles_per_mod, 0, 1)),
                  per_layer(wts["norm_w"]), per_layer(wts["w_in"]), tab, tab, tab,
                  per_seq(halo_a, A_WIDTH), per_seq(halo_b, B_QKV),
                  per_layer(wts["conv_a_w"]), per_layer(wts["conv_b_w"]), full(alog), full(dtb), full(seg)],
        out_specs=[row(w) for w in out_w] + [per_seq(halo_a, A_WIDTH), per_seq(halo_b, B_QKV)],
        out_shape=[jax.ShapeDtypeStruct((m, w), F32) for w in out_w]
        + [jax.ShapeDtypeStruct((seqs, halo_a, A_WIDTH), F32), jax.ShapeDtypeStruct((seqs, halo_b, B_QKV), F32)],
        scratch_shapes=[pltpu.VMEM((_round_up(halo_a, SUBLANE) + tm, A_WIDTH), F32),
                        pltpu.VMEM((_round_up(halo_b, SUBLANE) + tm, B_QKV), F32)],
        compiler_params=_params(1),
        name="inproj",
    )(x_rows, mod, mod, wts["norm_w"], wts["w_in"], *rope_tabs, buf_a, buf_b,
      wts["conv_a_w"], wts["conv_b_w"], alog, dtb, seg)


def _gdn_chunk_kernel(q_ref, k_ref, v_ref, g_ref, beta_ref, tri_ref,
                      kuw_ref, auw_ref, qg_ref, gc_ref, *, chunks):
    n = GDN_CHUNK
    ri = lax.broadcasted_iota(jnp.int32, (n, n), 0)
    ci = lax.broadcasted_iota(jnp.int32, (n, n), 1)
    tri = ri >= ci
    strict = ri > ci
    eye = jnp.where(ri == ci, 1.0, 0.0).astype(F32)
    lane = lax.broadcasted_iota(jnp.int32, (n, B_WIDTH), 1)
    pairs = []
    for c in range(chunks):
        rows = slice(c * n, (c + 1) * n)
        q = q_ref[rows, :]
        k = k_ref[rows, :]
        v = v_ref[rows, :]
        beta = beta_ref[rows, :]
        gc = jnp.dot(tri_ref[...], g_ref[rows, :], preferred_element_type=F32, precision=HIGHEST)
        gc_ref[rows, :] = gc
        gct = gc.T
        kt = k.T
        for h in range(B_HEADS):
            lo, hi = h * HEAD, (h + 1) * HEAD
            qh, kh, vh = q[:, lo:hi], k[:, lo:hi], v[:, lo:hi]
            gcol = gc[:, h:h + 1]
            grow = gct[h:h + 1, :]
            bcol = beta[:, h:h + 1]
            decay = jnp.where(tri, jnp.exp(jnp.where(tri, gcol - grow, 0.0)), 0.0)
            kdt = kt[lo:hi, :] * jnp.exp(gcol[n - 1:n, :] - grow)
            in_head = jnp.logical_and(lane >= lo, lane < hi)
            kq = jnp.concatenate([jnp.where(in_head, k * bcol, 0.0), jnp.where(in_head, q, 0.0)], axis=0)
            pairs.append(dict(qh=qh, kh=kh, vh=vh, gcol=gcol, bcol=bcol, decay=decay, kb=kh * bcol, kdt=kdt,
                              kq=kq, kt=kt))
    for pr in pairs:
        both = jnp.dot(pr["kq"], pr["kt"], preferred_element_type=F32)
        pr["kk"], pr["qk"] = both[:n], both[n:]
    for pr in pairs:
        lmat = jnp.where(strict, pr["kk"] * pr["decay"], 0.0)
        pr["x"] = eye - lmat
        pr["p"] = lmat
    for pr in pairs:
        pr["p"] = jnp.dot(pr["p"].astype(BF16), pr["p"].astype(BF16), preferred_element_type=F32)
    for step in range(5):
        for pr in pairs:
            if step < 4:
                both = jnp.dot(jnp.concatenate([pr["p"], pr["x"]], axis=0).astype(BF16), pr["p"].astype(BF16),
                               preferred_element_type=F32)
                pr["x"] = pr["x"] + both[n:]
                pr["p"] = both[:n]
            else:
                pr["x"] = pr["x"] + jnp.dot(pr["x"].astype(BF16), pr["p"].astype(BF16), preferred_element_type=F32)
    for pr in pairs:
        pr["egc"] = jnp.exp(pr["gcol"])
        rhs = jnp.concatenate([pr["vh"] * pr["bcol"], pr["kb"] * pr["egc"]], axis=1)
        pr["uw"] = jnp.dot(pr["x"], rhs, preferred_element_type=F32)
    for pr in pairs:
        pr["auw"] = jnp.dot(pr["qk"] * pr["decay"], pr["uw"], preferred_element_type=F32)
        pr["kuw"] = jnp.dot(pr["kdt"], pr["uw"], preferred_element_type=F32)
    for c in range(chunks):
        rows = slice(c * n, (c + 1) * n)
        egc_full = jnp.zeros((n, B_WIDTH), F32)
        for h, pr in enumerate(pairs[c * B_HEADS:(c + 1) * B_HEADS]):
            kuw_ref[rows, h * LANE:(h + 1) * LANE] = pr["kuw"]
            auw_ref[rows, h * LANE:(h + 1) * LANE] = pr["auw"]
            egc_full = jnp.where(jnp.logical_and(lane >= h * HEAD, lane < (h + 1) * HEAD), pr["egc"], egc_full)
        qg_ref[rows, :] = q_ref[rows, :] * egc_full


def _gdn_scan_kernel(kuw_ref, auw_ref, qg_ref, gc_ref, z_ref, nw_ref, y_ref, sfin_ref, s_ref, *, batch):
    c = pl.program_id(0)
    n = GDN_CHUNK

    @pl.when(c == 0)
    def _():
        s_ref[...] = jnp.zeros_like(s_ref)

    nw = nw_ref[...]
    zero = jnp.zeros((HEAD, HEAD), F32)
    pairs = []
    for b in range(batch):
        for h in range(B_HEADS):
            s = s_ref[b, h]
            upper = jnp.concatenate([zero, s], axis=0)
            own = upper if h % 2 else jnp.concatenate([s, zero], axis=0)
            kuw = kuw_ref[b, :, h * LANE:(h + 1) * LANE]
            auw = auw_ref[b, :, h * LANE:(h + 1) * LANE]
            qg = qg_ref[b, :, (h // 2) * LANE:(h // 2 + 1) * LANE]
            pairs.append(dict(b=b, h=h, s=s, upper=upper, own=own, kuw=kuw, auw=auw, qg=qg))
    for pr in pairs:
        pr["ra"] = jnp.dot(jnp.concatenate([pr["kuw"], pr["auw"]], axis=0), pr["upper"], preferred_element_type=F32)
        pr["rb"] = jnp.dot(pr["qg"], pr["own"], preferred_element_type=F32)
    for b in range(batch):
        eg = jnp.exp(gc_ref[b, n - 1:n, :])
        outs = []
        for pr in pairs[b * B_HEADS:(b + 1) * B_HEADS]:
            h = pr["h"]
            s_new = pr["s"] * eg[:, h:h + 1] + pr["kuw"][:, :HEAD] - pr["ra"][:n]
            s_ref[b, h] = s_new
            sfin_ref[b, h] = s_new
            o = pr["rb"] - pr["ra"][n:] + pr["auw"][:, :HEAD]
            ms = jnp.mean(o * o, axis=-1, keepdims=True)
            outs.append(o * lax.rsqrt(ms + EPS) * nw)
        y_ref[b] = jnp.concatenate(outs, axis=1) * _silu(z_ref[b])


def _gdn_prompt(q, k, v, g, beta, z, norm_w, batch, chunks_per_step=4):
    m = q.shape[0]
    t = m // batch
    n = GDN_CHUNK
    nc = t // n
    rows = chunks_per_step * n
    row = lambda w: pl.BlockSpec((rows, w), lambda i: (i, 0))
    tri = jnp.tril(jnp.ones((n, n), F32))
    slots = B_HEADS * LANE
    kuw, auw, qg, gc = pl.pallas_call(
        functools.partial(_gdn_chunk_kernel, chunks=chunks_per_step),
        grid=(m // rows,),
        in_specs=[row(B_WIDTH), row(B_WIDTH), row(B_WIDTH), row(LANE), row(LANE),
                  pl.BlockSpec((n, n), lambda i: (0, 0))],
        out_specs=[row(slots), row(slots), row(B_WIDTH), row(LANE)],
        out_shape=[jax.ShapeDtypeStruct((m, w), F32) for w in (slots, slots, B_WIDTH, LANE)],
        compiler_params=_params(1),
        name="gdn_chunk",
    )(q, k, v, g, beta, tri)
    per_chunk = lambda w: pl.BlockSpec((batch, n, w), lambda c: (0, c, 0))
    state = pl.BlockSpec((batch, B_HEADS, HEAD, HEAD), lambda c: (0, 0, 0, 0))
    view = lambda a: a.reshape(batch, t, a.shape[1])
    y, s_fin = pl.pallas_call(
        functools.partial(_gdn_scan_kernel, batch=batch),
        grid=(nc,),
        in_specs=[per_chunk(slots), per_chunk(slots), per_chunk(B_WIDTH), per_chunk(LANE), per_chunk(B_WIDTH),
                  pl.BlockSpec((1, HEAD), lambda c: (0, 0))],
        out_specs=[per_chunk(B_WIDTH), state],
        out_shape=[jax.ShapeDtypeStruct((batch, t, B_WIDTH), F32),
                   jax.ShapeDtypeStruct((batch, B_HEADS, HEAD, HEAD), F32)],
        scratch_shapes=[pltpu.VMEM((batch, B_HEADS, HEAD, HEAD), F32)],
        compiler_params=_params(1),
        name="gdn_scan",
    )(view(kuw), view(auw), view(qg), view(gc), view(z), norm_w.reshape(1, HEAD))
    return y.reshape(m, B_WIDTH), s_fin


def _gdn_sample_kernel(q_ref, k_ref, v_ref, g_ref, beta_ref, z_ref, nw_ref, s0_ref, y_ref, s_ref, *, steps):
    h = pl.program_id(0)
    bsz = s_ref.shape[-1]
    zero = jnp.zeros((HEAD, bsz), F32)

    def bcast_row(ref, i, dk):
        return jnp.broadcast_to(ref[i, pl.ds(dk, 1), :], (HEAD, bsz))

    def head_row(ref, i):
        return ref[i, pl.ds(h, 1), :]

    def finish(i, o):
        ms = jnp.mean(o * o, axis=0, keepdims=True)
        y_ref[i] = o * lax.rsqrt(ms + EPS) * nw_ref[...] * _silu(z_ref[i])

    gam = jnp.exp(head_row(g_ref, 0))

    def first(dk, ks):
        s = s0_ref[0, dk] * gam
        s_ref[0, dk] = s
        return ks + bcast_row(k_ref, 0, dk) * s

    ks = lax.fori_loop(0, HEAD, first, zero)
    vn = head_row(beta_ref, 0) * (v_ref[0] - ks)

    for i in range(1, steps):
        gam = jnp.exp(head_row(g_ref, i))

        def mid(dk, carry, i=i, vn=vn, gam=gam):
            o, ks = carry
            s = s_ref[0, dk] + bcast_row(k_ref, i - 1, dk) * vn
            o = o + bcast_row(q_ref, i - 1, dk) * s
            s = s * gam
            s_ref[0, dk] = s
            return o, ks + bcast_row(k_ref, i, dk) * s

        o, ks = lax.fori_loop(0, HEAD, mid, (zero, zero))
        finish(i - 1, o)
        vn = head_row(beta_ref, i) * (v_ref[i] - ks)

    def last(dk, o):
        s = s_ref[0, dk] + bcast_row(k_ref, steps - 1, dk) * vn
        s_ref[0, dk] = s
        return o + bcast_row(q_ref, steps - 1, dk) * s

    finish(steps - 1, lax.fori_loop(0, HEAD, last, zero))


def _gdn_sample(qt, kt, vt, gt, bt, zt, nwb, s0t, layer):
    steps, _, bsz = qt.shape
    per_head = pl.BlockSpec((steps, HEAD, bsz), lambda h: (0, h, 0))
    small = pl.BlockSpec((steps, SUBLANE, bsz), lambda h: (0, 0, 0))
    state = pl.BlockSpec((1, HEAD, HEAD, bsz), lambda h: (h, 0, 0, 0))
    state_in = pl.BlockSpec((None, 1, HEAD, HEAD, bsz), lambda h: (layer, h, 0, 0, 0))
    return pl.pallas_call(
        functools.partial(_gdn_sample_kernel, steps=steps),
        grid=(B_HEADS,),
        in_specs=[per_head, per_head, per_head, small, small, per_head,
                  pl.BlockSpec((HEAD, bsz), lambda h: (0, 0)), state_in],
        out_specs=[per_head, state],
        out_shape=[jax.ShapeDtypeStruct((steps, B_WIDTH, bsz), F32),
                   jax.ShapeDtypeStruct((B_HEADS, HEAD, HEAD, bsz), F32)],
        compiler_params=_params(1),
        name="gdn_sample",
    )(qt, kt, vt, gt, bt, zt, nwb, s0t)


def _attn_prompt_kernel(q_ref, k_ref, v_ref, o_ref, lse_ref, *, dil, nb, chains_per_stage):
    n = ATTN_BLOCK
    scale = HEAD ** -0.5
    lane = lax.broadcasted_iota(jnp.int32, (n, GROUP_W), 1)
    head0 = lane < HEAD
    qi = lax.broadcasted_iota(jnp.int32, (n, n), 0)
    kj = lax.broadcasted_iota(jnp.int32, (n, n), 1)
    mask_c = kj <= qi
    mask_p = kj >= qi

    def rows(r, i):
        if dil == 1:
            return pl.ds(i * n, n)
        return pl.ds(r + i * n * dil, n, stride=dil)

    blocks = [(r, i) for r in range(dil) for i in range(nb)]
    for first in range(0, len(blocks), chains_per_stage):
        stage = []
        for r, i in blocks[first:first + chains_per_stage]:
            q = q_ref[rows(r, i), :]
            for sel in (head0, jnp.logical_not(head0)):
                stage.append(dict(r=r, i=i, qh=jnp.where(sel, q, 0.0)))
        for ch in stage:
            r, i = ch["r"], ch["i"]
            ch["sc"] = _dot_nt(ch["qh"], k_ref[rows(r, i), :])
            if i > 0:
                ch["sp"] = _dot_nt(ch["qh"], k_ref[rows(r, i - 1), :])
        for ch in stage:
            sc = jnp.where(mask_c, ch["sc"] * scale, NEG)
            mx = jnp.max(sc, axis=-1, keepdims=True)
            if ch["i"] > 0:
                sp = jnp.where(mask_p, ch["sp"] * scale, NEG)
                mx = jnp.maximum(mx, jnp.max(sp, axis=-1, keepdims=True))
                ch["pp"] = jnp.exp(sp - mx)
            ch["pc"] = jnp.exp(sc - mx)
            den = jnp.sum(ch["pc"], axis=-1, keepdims=True)
            if ch["i"] > 0:
                den = den + jnp.sum(ch["pp"], axis=-1, keepdims=True)
            ch["den"] = den
            ch["lse"] = mx + jnp.log(den)
        for ch in stage:
            r, i = ch["r"], ch["i"]
            acc = jnp.dot(ch["pc"], v_ref[rows(r, i), :], preferred_element_type=F32)
            if i > 0:
                acc = acc + jnp.dot(ch["pp"], v_ref[rows(r, i - 1), :], preferred_element_type=F32)
            ch["o"] = acc / ch["den"]
        for c0, c1 in zip(stage[0::2], stage[1::2]):
            r, i = c0["r"], c0["i"]
            o_ref[rows(r, i), :] = jnp.where(head0, c0["o"], c1["o"])
            lse_ref[rows(r, i), :] = jnp.where(head0, jnp.broadcast_to(c0["lse"], (n, GROUP_W)),
                                               jnp.broadcast_to(c1["lse"], (n, GROUP_W)))


def _attn_prompt(qrot, kv, group, batch, chains_per_stage=4):
    _, dil = GROUPS[group]
    m = qrot.shape[0]
    t = m // batch
    nb = t // (dil * ATTN_BLOCK)
    seq = lambda col: pl.BlockSpec((t, GROUP_W), lambda b: (b, col))
    out = jax.ShapeDtypeStruct((m, GROUP_W), F32)
    return pl.pallas_call(
        functools.partial(_attn_prompt_kernel, dil=dil, nb=nb, chains_per_stage=chains_per_stage),
        grid=(batch,),
        in_specs=[seq(group), seq(0), seq(1)],
        out_specs=[seq(0), seq(0)],
        out_shape=[out, out],
        compiler_params=_params(1),
        name=f"attn_prompt_g{group}",
    )(qrot, kv, kv)


def _attn_sample_kernel(q_ref, n0_ref, n1_ref, n2_ref, c0_ref, c1_ref, c2_ref, o_ref, lse_ref, *, bt, steps):
    rows = 2 * steps
    scale = HEAD ** -0.5
    lane = lax.broadcasted_iota(jnp.int32, (rows, GROUP_W), 1)
    rix = lax.broadcasted_iota(jnp.int32, (rows, GROUP_W), 0)
    own_head = (lane < HEAD) == (rix < steps)
    nkey = lax.broadcasted_iota(jnp.int32, (rows, rows), 1)
    nqry = lax.broadcasted_iota(jnp.int32, (rows, rows), 0) % steps
    new_refs = (n0_ref, n1_ref, n2_ref)
    cache_refs = (c0_ref, c1_ref, c2_ref)

    head0 = lane[:steps] < HEAD
    chains = []
    for b in range(bt):
        q_all = q_ref[b]
        for g, (win, dil) in enumerate(GROUPS):
            qm = jnp.where(own_head, q_all[:, g * GROUP_W:(g + 1) * GROUP_W], 0.0)
            chains.append(dict(b=b, g=g, win=win, dil=dil, qm=qm, new=new_refs[g][b]))
    for ch in chains:
        kt = cache_refs[ch["g"]][0, ch["b"], 0].reshape(GROUP_W, ch["win"])
        ch["s_buf"] = jnp.dot(ch["qm"], kt, preferred_element_type=F32)
        ch["s_new"] = _dot_nt(ch["qm"], ch["new"][:, :GROUP_W])
    for ch in chains:
        win, dil = ch["win"], ch["dil"]
        pos = lax.broadcasted_iota(jnp.int32, (rows, win), 1)
        qry = lax.broadcasted_iota(jnp.int32, (rows, win), 0) % steps
        if dil == 1:
            ok = pos >= qry
            new_ok = jnp.logical_and(nkey < steps, nkey <= nqry)
        else:
            ok = (pos % dil) == qry
            new_ok = nkey == nqry
        s_buf = jnp.where(ok, ch["s_buf"] * scale, NEG)
        s_new = jnp.where(new_ok, ch["s_new"] * scale, NEG)
        mx = jnp.maximum(jnp.max(s_buf, axis=-1, keepdims=True), jnp.max(s_new, axis=-1, keepdims=True))
        ch["p_buf"] = jnp.exp(s_buf - mx)
        ch["p_new"] = jnp.exp(s_new - mx)
        ch["den"] = jnp.sum(ch["p_buf"], axis=-1, keepdims=True) + jnp.sum(ch["p_new"], axis=-1, keepdims=True)
        ch["mx"] = mx
    for ch in chains:
        vt = cache_refs[ch["g"]][0, ch["b"], 1].reshape(GROUP_W, ch["win"])
        ch["acc"] = jnp.dot(ch["p_new"], ch["new"][:, GROUP_W:], preferred_element_type=F32) + _dot_nt(ch["p_buf"], vt)
    for ch in chains:
        b, g = ch["b"], ch["g"]
        o_full = ch["acc"] / ch["den"]
        lse_full = jnp.broadcast_to(ch["mx"] + jnp.log(ch["den"]), (rows, GROUP_W))
        o_ref[b, :, g * GROUP_W:(g + 1) * GROUP_W] = jnp.where(head0, o_full[:steps], o_full[steps:])
        lse_ref[b, :, g * GROUP_W:(g + 1) * GROUP_W] = jnp.where(head0, lse_full[:steps], lse_full[steps:])


def _attn_sample(q2, new_kv, caches, layer, bt):
    bsz, rows, _ = q2.shape
    steps = rows // 2
    per_b = lambda w: pl.BlockSpec((bt, rows, w), lambda i: (i, 0, 0))
    cache_spec = lambda win: pl.BlockSpec((1, bt, 2, HPG, HEAD, win), lambda i: (layer, i, 0, 0, 0, 0))
    out_spec = pl.BlockSpec((bt, steps, C_WIDTH), lambda i: (i, 0, 0))
    out = jax.ShapeDtypeStruct((bsz, steps, C_WIDTH), F32)
    return pl.pallas_call(
        functools.partial(_attn_sample_kernel, bt=bt, steps=steps),
        grid=(bsz // bt,),
        in_specs=[per_b(C_WIDTH), per_b(2 * GROUP_W), per_b(2 * GROUP_W), per_b(2 * GROUP_W)]
        + [cache_spec(win) for win, _ in GROUPS],
        out_specs=[out_spec, out_spec],
        out_shape=[out, out],
        compiler_params=_params(1),
        name="attn_sample",
    )(q2, *new_kv, *caches)


def _outproj_kernel(ya_ref, yb_ref, o0_ref, o1_ref, o2_ref, l0_ref, l1_ref, l2_ref, cz_ref, x_ref, gate_ref,
                    w_ref, xo_ref):
    lses = (l0_ref[...], l1_ref[...], l2_ref[...])
    mx = jnp.maximum(jnp.maximum(lses[0], lses[1]), lses[2])
    es = [jnp.exp(l - mx) for l in lses]
    tot = es[0] + es[1] + es[2]
    cz = cz_ref[...]

    def mm(y, lo):
        return jnp.dot(y.astype(BF16), w_ref[lo:lo + y.shape[1], :], preferred_element_type=F32)

    out = mm(ya_ref[...], 0) + mm(yb_ref[...], A_WIDTH)
    for g, o_ref in enumerate((o0_ref, o1_ref, o2_ref)):
        yc = o_ref[...] * (es[g] / tot) * _silu(cz[:, g * GROUP_W:(g + 1) * GROUP_W])
        out = out + mm(yc, A_WIDTH + B_WIDTH + g * GROUP_W)
    xo_ref[...] = x_ref[...] + (1.0 + gate_ref[0]) * out


def _outproj(ya, yb, os_, lses, cz, x_rows, mod, w_out, layer, tm):
    m = x_rows.shape[0]
    nblk = m // tm
    nb_mod, r, _ = mod.shape
    tiles_per_mod = nblk // nb_mod
    row = lambda w: pl.BlockSpec((tm, w), lambda i: (i, 0))
    return pl.pallas_call(
        _outproj_kernel,
        grid=(nblk,),
        in_specs=[row(A_WIDTH), row(B_WIDTH)] + [row(GROUP_W)] * 6 + [row(C_WIDTH), row(D_MODEL),
                  pl.BlockSpec((1, r, D_MODEL), lambda i: (i // tiles_per_mod, 0, 2)),
                  pl.BlockSpec((None, D_MODEL, D_MODEL), lambda i: (layer, 0, 0))],
        out_specs=row(D_MODEL),
        out_shape=jax.ShapeDtypeStruct((m, D_MODEL), F32),
        compiler_params=_params(1),
        name="outproj",
    )(ya, yb, *os_, *lses, cz, x_rows, mod, w_out)


def _final_kernel(x_ref, sh_ref, sc_ref, nw_ref, y_ref):
    x = x_ref[...]
    var = jnp.mean(x * x, axis=-1, keepdims=True)
    y_ref[...] = x * lax.rsqrt(var + EPS) * nw_ref[...] * (1.0 + sc_ref[0]) + sh_ref[0]


def _final(x_rows, mod, norm_w, tm):
    m = x_rows.shape[0]
    nblk = m // tm
    nb_mod, r, _ = mod.shape
    tiles_per_mod = nblk // nb_mod
    row = pl.BlockSpec((tm, D_MODEL), lambda i: (i, 0))
    return pl.pallas_call(
        _final_kernel,
        grid=(nblk,),
        in_specs=[row,
                  pl.BlockSpec((1, r, D_MODEL), lambda i: (i // tiles_per_mod, 0, 0)),
                  pl.BlockSpec((1, r, D_MODEL), lambda i: (i // tiles_per_mod, 0, 1)),
                  pl.BlockSpec((1, D_MODEL), lambda i: (0, 0))],
        out_specs=row,
        out_shape=jax.ShapeDtypeStruct((m, D_MODEL), F32),
        compiler_params=_params(1),
        name="final_norm",
    )(x_rows, mod, mod, norm_w.reshape(1, D_MODEL))


def _rope_tables(pos):
    half = ROT_DIM // 2
    inv_freq = ROPE_THETA ** (-jnp.arange(half, dtype=F32) * (2.0 / ROT_DIM))
    ang = pos.astype(F32)[:, None] * inv_freq[None, :]
    cos, sin = jnp.cos(ang), jnp.sin(ang)
    rows = pos.shape[0]
    one = jnp.ones((rows, HEAD - ROT_DIM), F32)
    zero_r = jnp.zeros((rows, HEAD - ROT_DIM), F32)
    zero_h = jnp.zeros((rows, half), F32)
    cos_h = jnp.concatenate([cos, cos, one], axis=1)
    sa_h = jnp.concatenate([-sin, zero_h, zero_r], axis=1)
    sb_h = jnp.concatenate([zero_h, sin, zero_r], axis=1)
    return tuple(jnp.concatenate([t] * HPG, axis=1) for t in (cos_h, sa_h, sb_h))


def _permute_w_in(w_in):
    depth = w_in.shape[0]
    cols = lambda src: w_in[:, :, src[0]:src[1]]
    pad = jnp.zeros((depth, D_MODEL, COL_AB[1] - COL_AB[0] - (SRC_AB[1] - SRC_AB[0])), w_in.dtype)
    return jnp.concatenate([cols(SRC_A), cols(SRC_BQKV), cols(SRC_AB), pad, cols(SRC_BZ), cols(SRC_CZ),
                            cols(SRC_CQKV)], axis=2).astype(BF16)


def _segment_ones():
    idx = jnp.arange(B_WIDTH) // HEAD
    return (idx[:, None] == idx[None, :]).astype(F32)


def _prompt_trunk(x, mods, mod_final, wts, seg):
    batch, t, _ = x.shape
    depth = wts["w_in"].shape[0]
    m = batch * t
    tm = 512
    rows = x.reshape(m, D_MODEL)
    tabs = _rope_tables(jnp.arange(t))
    zeros_a = jnp.zeros((batch, A_CONV - 1, A_WIDTH), F32)
    zeros_b = jnp.zeros((batch, B_CONV - 1, B_QKV), F32)
    st_a, st_b, st_g, kvs = [], [], [], [[] for _ in GROUPS]
    for l in range(depth):
        mod = mods[l].reshape(batch, 1, 3 * D_MODEL)
        ya, q, k, v, g, beta, bz, qrot, kv0, kv1, kv2, cz, sa, sb = _inproj(
            rows, mod, zeros_a, zeros_b, tabs, wts, l, seg, seqs=batch, stride=1, tm=tm)
        yb, s_fin = _gdn_prompt(q, k, v, g, beta, bz, wts["gdn_norm_w"][l], batch)
        os_, lses = [], []
        for gi, kv in enumerate((kv0, kv1, kv2)):
            o, lse = _attn_prompt(qrot, kv, gi, batch)
            os_.append(o)
            lses.append(lse)
            win = min(GROUPS[gi][0], t)
            kvs[gi].append(kv.reshape(batch, t, 2, HPG, HEAD)[:, t - win:])
        rows = _outproj(ya, yb, os_, lses, cz, rows, mod, wts["w_out"], l, tm)
        st_a.append(sa)
        st_b.append(sb)
        st_g.append(s_fin)
    y = _final(rows, mod_final.reshape(batch, 1, 2 * D_MODEL), wts["final_norm_w"], tm).reshape(batch, t, D_MODEL)
    return y, jnp.stack(st_a), jnp.stack(st_b), jnp.stack(st_g), [jnp.stack(r) for r in kvs]


def _sample_trunk(x, mods, mod_final, state_a, state_b, state_g, caches, past_len, wts, seg):
    bsz, steps, _ = x.shape
    depth = wts["w_in"].shape[0]
    m = bsz * steps
    tm = bsz

    def time_major(a):
        return a.transpose(1, 0, 2).reshape(a.shape[1] * bsz, a.shape[2])

    def batch_major(a, n):
        return a.reshape(n, bsz, a.shape[1]).transpose(1, 0, 2)

    def lanes_batch(a, n):
        return a.reshape(n, bsz, a.shape[1]).transpose(0, 2, 1)

    rows = time_major(x)
    tabs = _rope_tables(jnp.repeat(past_len + jnp.arange(steps), bsz))
    cache_views = [c.transpose(0, 1, 3, 4, 5, 2) for c in caches]
    state_t = state_g.transpose(0, 2, 3, 4, 1)
    st_a, st_b, st_g, kvs = [], [], [], [[] for _ in GROUPS]
    for l in range(depth):
        mod = mods[l][None]
        buf_a = time_major(state_a[l])[None]
        buf_b = time_major(state_b[l])[None]
        ya, q, k, v, g, beta, bz, qrot, kv0, kv1, kv2, cz, sa, sb = _inproj(
            rows, mod, buf_a, buf_b, tabs, wts, l, seg, seqs=1, stride=bsz, tm=m)
        nwb = jnp.broadcast_to(wts["gdn_norm_w"][l][:, None], (HEAD, bsz))
        yt, s_t = _gdn_sample(lanes_batch(q, steps), lanes_batch(k, steps), lanes_batch(v, steps),
                              lanes_batch(g[:, :SUBLANE], steps), lanes_batch(beta[:, :SUBLANE], steps),
                              lanes_batch(bz, steps), nwb, state_t, l)
        yb = yt.transpose(0, 2, 1).reshape(m, B_WIDTH)
        dup = lambda a: jnp.concatenate([batch_major(a, steps)] * 2, axis=1)
        o_c, lse_c = _attn_sample(dup(qrot), [dup(kv) for kv in (kv0, kv1, kv2)], cache_views, l, 4)
        o_c, lse_c = time_major(o_c), time_major(lse_c)
        os_ = [o_c[:, gi * GROUP_W:(gi + 1) * GROUP_W] for gi in range(len(GROUPS))]
        lses = [lse_c[:, gi * GROUP_W:(gi + 1) * GROUP_W] for gi in range(len(GROUPS))]
        rows = _outproj(ya, yb, os_, lses, cz, rows, mod, wts["w_out"], l, tm)
        st_a.append(batch_major(sa[0], A_CONV - 1))
        st_b.append(batch_major(sb[0], B_CONV - 1))
        st_g.append(s_t.transpose(3, 0, 1, 2))
        for gi, kv in enumerate((kv0, kv1, kv2)):
            kvs[gi].append(batch_major(kv, steps).reshape(bsz, steps, 2, HPG, HEAD))
    y = batch_major(_final(rows, mod_final[None], wts["final_norm_w"], tm), steps)
    return y, jnp.stack(st_a), jnp.stack(st_b), jnp.stack(st_g), [jnp.stack(r) for r in kvs]


def kernel(x_prompt, x_sample, state_conv_a, state_conv_b, state_gdn, cache_kv_w128, cache_kv_w512,
           cache_kv_w2048, c_prompt, c_sample, w_in, w_out, w_ada, b_ada, norm_w, conv_a_w, conv_b_w,
           a_log, dt_bias, gdn_norm_w, final_norm_w, w_ada_final, b_ada_final):
    n_prompt = c_prompt.shape[0]
    c_all = jnp.concatenate([c_prompt, c_sample], axis=0)
    mods = _ada(c_all, w_ada, b_ada)
    mod_final = _ada(c_all, w_ada_final[None], b_ada_final[None])[0]
    wts = dict(w_in=_permute_w_in(w_in), w_out=w_out.astype(BF16), norm_w=norm_w[:, None, :], conv_a_w=conv_a_w,
               conv_b_w=conv_b_w, a_log=a_log, dt_bias=dt_bias, gdn_norm_w=gdn_norm_w,
               final_norm_w=final_norm_w)
    seg = _segment_ones()
    y_p, ca_p, cb_p, g_p, kv_p = _prompt_trunk(x_prompt, mods[:, :n_prompt], mod_final[:n_prompt], wts, seg)
    y_s, ca_s, cb_s, g_s, kv_s = _sample_trunk(
        x_sample, mods[:, n_prompt:], mod_final[n_prompt:], state_conv_a, state_conv_b, state_gdn,
        (cache_kv_w128, cache_kv_w512, cache_kv_w2048), PAST_LEN, wts, seg)
    return (y_p, y_s, ca_p, ca_s, cb_p, cb_s, g_p, g_s,
            kv_p[0], kv_s[0], kv_p[1], kv_s[1], kv_p[2], kv_s[2])
```

```python
import functools
import math

import jax
import jax.numpy as jnp
from jax import lax
from jax.experimental import pallas as pl
from jax.experimental.pallas import tpu as pltpu

F32 = jnp.float32
BF16 = jnp.bfloat16
HIGHEST = lax.Precision.HIGHEST

D_MODEL = 1024
HEAD = 64
A_WIDTH = 256
A_CONV = 3
B_HEADS = 6
B_WIDTH = B_HEADS * HEAD
B_QKV = 3 * B_WIDTH
B_CONV = 4
GDN_CHUNK = 64
GROUPS = ((128, 1), (512, 4), (2048, 16))
HPG = 2
GROUP_W = HPG * HEAD
C_WIDTH = len(GROUPS) * GROUP_W
ATTN_BLOCK = 128
ROT_DIM = HEAD // 4
ROPE_THETA = 500000.0
PAST_LEN = 2048
EPS = 1e-6
NEG = -1e30
LANE = 128
SUBLANE = 8

COL_A = (0, 1024)
COL_BQKV = (1024, 2176)
COL_AB = (2176, 2304)
COL_BZ = (2304, 2688)
COL_CZ = (2688, 3072)
COL_CQ = (3072, 3456)
COL_CK = (3456, 3840)
COL_CV = (3840, 4224)
IN_PERM_WIDTH = 4224
SRC_A = (0, 1024)
SRC_BQKV = (1024, 2176)
SRC_BZ = (2176, 2560)
SRC_AB = (2560, 2572)
SRC_CQKV = (2572, 3724)
SRC_CZ = (3724, 4108)

VMEM_LIMIT_BYTES = 56 * 1024 * 1024


def _params(n_axes):
    return pltpu.CompilerParams(dimension_semantics=("arbitrary",) * n_axes,
                                vmem_limit_bytes=VMEM_LIMIT_BYTES)


def _round_up(x, m):
    return -(-x // m) * m


def _silu(x):
    return x * jax.nn.sigmoid(x)


def _dot_nt(a, b, **kw):
    return lax.dot_general(a, b, (((1,), (1,)), ((), ())), preferred_element_type=F32, **kw)


def _ada_kernel(c_ref, w_ref, b_ref, o_ref):
    c = c_ref[...].astype(BF16)
    w = w_ref[0].astype(BF16)
    o_ref[0] = jnp.dot(c, w, preferred_element_type=F32) + b_ref[0]


def _ada(c_all, w, b):
    n_layers, _, n = w.shape
    r = c_all.shape[0]
    tn = 1024
    return pl.pallas_call(
        _ada_kernel,
        grid=(n_layers, n // tn),
        in_specs=[pl.BlockSpec((r, D_MODEL), lambda l, j: (0, 0)),
                  pl.BlockSpec((1, D_MODEL, tn), lambda l, j: (l, 0, j)),
                  pl.BlockSpec((1, 1, tn), lambda l, j: (l, 0, j))],
        out_specs=pl.BlockSpec((1, r, tn), lambda l, j: (l, 0, j)),
        out_shape=jax.ShapeDtypeStruct((n_layers, r, n), F32),
        compiler_params=_params(2),
        name="ada",
    )(c_all, w, b.reshape(n_layers, 1, n))


def _inproj_kernel(x_ref, sh_ref, sc_ref, nw_ref, w_ref, cos_ref, sa_ref, sb_ref,
                   bufa_ref, bufb_ref, wa_ref, wb_ref, alog_ref, dtb_ref, seg_ref,
                   ya_ref, q_ref, k_ref, v_ref, g_ref, beta_ref, bz_ref, qrot_ref, kv0_ref, kv1_ref, kv2_ref, cz_ref,
                   sta_ref, stb_ref, xpa_ref, xpb_ref, *, stride, tm, tiles_per_seq):
    t = pl.program_id(0) % tiles_per_seq
    x = x_ref[...]
    reps = tm // sh_ref.shape[1] if sh_ref.shape[1] > 1 else 1
    tile_rows = lambda a: a if reps == 1 else jnp.concatenate([a] * reps, axis=0)
    var = jnp.mean(x * x, axis=-1, keepdims=True)
    hn = x * lax.rsqrt(var + EPS) * nw_ref[...]
    hn = hn * (1.0 + tile_rows(sc_ref[0])) + tile_rows(sh_ref[0])
    hb = hn.astype(BF16)

    def mm(lo, hi):
        return jnp.dot(hb, w_ref[:, lo:hi], preferred_element_type=F32)

    def halo_rows(taps):
        halo = (taps - 1) * stride
        return halo, _round_up(halo, SUBLANE)

    def load_halo(xp_ref, buf_ref, taps):
        halo, x0 = halo_rows(taps)

        @pl.when(t == 0)
        def _():
            xp_ref[x0 - halo:x0, :] = buf_ref[0]

        @pl.when(t > 0)
        def _():
            xp_ref[x0 - halo:x0, :] = xp_ref[x0 + tm - halo:x0 + tm, :]

    def conv_cols(xp_ref, x_new, w_ref, taps, lo, hi):
        halo, x0 = halo_rows(taps)
        xp_ref[x0:x0 + tm, lo:hi] = x_new
        y = xp_ref[x0 - halo:x0 - halo + tm, lo:hi] * w_ref[0:1, lo:hi]
        for j in range(1, taps):
            r0 = x0 - halo + j * stride
            y = y + xp_ref[r0:r0 + tm, lo:hi] * w_ref[j:j + 1, lo:hi]
        return y

    def store_tail(xp_ref, st_ref, taps):
        halo, x0 = halo_rows(taps)
        st_ref[0] = xp_ref[x0 + tm - halo:x0 + tm, :]

    load_halo(xpa_ref, bufa_ref, A_CONV)
    load_halo(xpb_ref, bufb_ref, B_CONV)
    seg = seg_ref[...].astype(BF16)
    cos = cos_ref[...]
    sa = sa_ref[...]
    sb = sb_ref[...]

    def rope(u):
        return u * cos + pltpu.roll(u, LANE - ROT_DIM // 2, 1) * sa + pltpu.roll(u, ROT_DIM // 2, 1) * sb

    def branch_b(raw, lo, hi):
        act = _silu(conv_cols(xpb_ref, raw, wb_ref, B_CONV, lo, hi))
        if lo < 2 * B_WIDTH:
            sq = act * act
            hi16 = sq.astype(BF16)
            lo16 = (sq - hi16.astype(F32)).astype(BF16)
            blk = seg[:hi - lo, :hi - lo]
            ss = jnp.dot(hi16, blk, preferred_element_type=F32) + jnp.dot(lo16, blk, preferred_element_type=F32)
            act = act * lax.rsqrt(ss + 1e-6)
        for out_ref, base, scale in ((q_ref, 0, HEAD ** -0.5), (k_ref, B_WIDTH, None), (v_ref, 2 * B_WIDTH, None)):
            c0, c1 = max(lo, base), min(hi, base + B_WIDTH)
            if c0 < c1:
                piece = act[:, c0 - lo:c1 - lo]
                out_ref[:, c0 - base:c1 - base] = piece if scale is None else piece * scale

    def use_b_tail(raw):
        n_v = B_QKV - 4 * wide
        branch_b(raw[:, :n_v], 4 * wide, B_QKV)
        ab = raw[:, n_v:]
        z = ab + dtb_ref[...]
        softplus = jnp.maximum(z, 0.0) + jnp.log1p(jnp.exp(-jnp.abs(z)))
        g_ref[...] = -jnp.exp(alog_ref[...]) * softplus
        beta_ref[...] = jax.nn.sigmoid(pltpu.roll(ab, LANE - B_HEADS, 1))

    def use_gates(raw):
        bz_ref[...] = raw[:, :B_WIDTH]
        cz_ref[...] = raw[:, B_WIDTH:]

    def use_c(raw):
        for g, kv_ref in enumerate((kv0_ref, kv1_ref, kv2_ref)):
            lo, hi = g * GROUP_W, (g + 1) * GROUP_W
            qrot_ref[:, lo:hi] = rope(raw[:, lo:hi])
            kv_ref[:, :GROUP_W] = rope(raw[:, C_WIDTH + lo:C_WIDTH + hi])
            kv_ref[:, GROUP_W:] = raw[:, 2 * C_WIDTH + lo:2 * C_WIDTH + hi]

    conv_a = []

    def use_a_in(raw):
        conv_a.append(conv_cols(xpa_ref, raw[:, A_WIDTH:] * raw[:, :A_WIDTH], wa_ref, A_CONV, 0, A_WIDTH))

    def use_a_out(raw):
        ya_ref[...] = raw[:, :A_WIDTH] * conv_a[0] * _silu(raw[:, A_WIDTH:])

    wide = 2 * LANE
    b0, a0 = COL_BQKV[0], COL_A[0]
    b_chunk = lambda c: ((b0 + c * wide, b0 + (c + 1) * wide), lambda raw: branch_b(raw, c * wide, (c + 1) * wide))
    jobs = [b_chunk(0), ((COL_BZ[0], COL_CZ[1]), use_gates), b_chunk(1), ((COL_CQ[0], COL_CV[1]), use_c),
            b_chunk(2), ((a0, a0 + 2 * A_WIDTH), use_a_in), b_chunk(3),
            ((a0 + 2 * A_WIDTH, a0 + 4 * A_WIDTH), use_a_out), ((b0 + 4 * wide, COL_AB[1]), use_b_tail)]
    pending = mm(*jobs[0][0])
    for i, (_, consume) in enumerate(jobs):
        ahead = mm(*jobs[i + 1][0]) if i + 1 < len(jobs) else None
        consume(pending)
        pending = ahead
    store_tail(xpb_ref, stb_ref, B_CONV)
    store_tail(xpa_ref, sta_ref, A_CONV)


def _inproj(x_rows, mod, buf_a, buf_b, rope_tabs, wts, layer, seg, *, seqs, stride, tm):
    m = x_rows.shape[0]
    nblk = m // tm
    tiles_per_seq = nblk // seqs
    nb_mod, r, _ = mod.shape
    tiles_per_mod = nblk // nb_mod
    nt_tab = rope_tabs[0].shape[0] // tm
    halo_a, halo_b = (A_CONV - 1) * stride, (B_CONV - 1) * stride
    row = lambda w: pl.BlockSpec((tm, w), lambda i: (i, 0))
    tab = pl.BlockSpec((tm, LANE), lambda i: (i % nt_tab, 0))
    full = lambda a: pl.BlockSpec(a.shape, lambda i: (0,) * a.ndim)
    per_layer = lambda a: pl.BlockSpec((None,) + a.shape[1:], lambda i: (layer,) + (0,) * (a.ndim - 1))
    per_seq = lambda h, w: pl.BlockSpec((1, h, w), lambda i: (i // tiles_per_seq, 0, 0))
    alog = jnp.zeros((1, LANE), F32).at[0, :B_HEADS].set(wts["a_log"][layer])
    dtb = jnp.zeros((1, LANE), F32).at[0, :B_HEADS].set(wts["dt_bias"][layer])
    out_w = (A_WIDTH, B_WIDTH, B_WIDTH, B_WIDTH, LANE, LANE, B_WIDTH, C_WIDTH, 2 * GROUP_W, 2 * GROUP_W, 2 * GROUP_W,
             C_WIDTH)
    return pl.pallas_call(
        functools.partial(_inproj_kernel, stride=stride, tm=tm, tiles_per_seq=tiles_per_seq),
        grid=(nblk,),
        in_specs=[row(D_MODEL),
                  pl.BlockSpec((1, r, D_MODEL), lambda i: (i // tiles_per_mod, 0, 0)),
                  pl.BlockSpec((1, r, D_MODEL), lambda i: (i // tiles_per_mod, 0, 1)),
                  per_layer(wts["norm_w"]), per_layer(wts["w_in"]), tab, tab, tab,
                  per_seq(halo_a, A_WIDTH), per_seq(halo_b, B_QKV),
                  per_layer(wts["conv_a_w"]), per_layer(wts["conv_b_w"]), full(alog), full(dtb), full(seg)],
        out_specs=[row(w) for w in out_w] + [per_seq(halo_a, A_WIDTH), per_seq(halo_b, B_QKV)],
        out_shape=[jax.ShapeDtypeStruct((m, w), F32) for w in out_w]
        + [jax.ShapeDtypeStruct((seqs, halo_a, A_WIDTH), F32), jax.ShapeDtypeStruct((seqs, halo_b, B_QKV), F32)],
        scratch_shapes=[pltpu.VMEM((_round_up(halo_a, SUBLANE) + tm, A_WIDTH), F32),
                        pltpu.VMEM((_round_up(halo_b, SUBLANE) + tm, B_QKV), F32)],
        compiler_params=_params(1),
        name="inproj",
    )(x_rows, mod, mod, wts["norm_w"], wts["w_in"], *rope_tabs, buf_a, buf_b,
      wts["conv_a_w"], wts["conv_b_w"], alog, dtb, seg)


def _gdn_chunk_kernel(q_ref, k_ref, v_ref, g_ref, beta_ref, tri_ref,
                      kuw_ref, auw_ref, qg_ref, gc_ref, *, chunks):
    n = GDN_CHUNK
    ri = lax.broadcasted_iota(jnp.int32, (n, n), 0)
    ci = lax.broadcasted_iota(jnp.int32, (n, n), 1)
    tri = ri >= ci
    strict = ri > ci
    eye = jnp.where(ri == ci, 1.0, 0.0).astype(F32)
    lane = lax.broadcasted_iota(jnp.int32, (n, B_WIDTH), 1)
    pairs = []
    for c in range(chunks):
        rows = slice(c * n, (c + 1) * n)
        q = q_ref[rows, :]
        k = k_ref[rows, :]
        v = v_ref[rows, :]
        beta = beta_ref[rows, :]
        gc = jnp.dot(tri_ref[...], g_ref[rows, :], preferred_element_type=F32, precision=HIGHEST)
        gc_ref[rows, :] = gc
        gct = gc.T
        kt = k.T
        for h in range(B_HEADS):
            lo, hi = h * HEAD, (h + 1) * HEAD
            qh, kh, vh = q[:, lo:hi], k[:, lo:hi], v[:, lo:hi]
            gcol = gc[:, h:h + 1]
            grow = gct[h:h + 1, :]
            bcol = beta[:, h:h + 1]
            decay = jnp.where(tri, jnp.exp(jnp.where(tri, gcol - grow, 0.0)), 0.0)
            kdt = kt[lo:hi, :] * jnp.exp(gcol[n - 1:n, :] - grow)
            in_head = jnp.logical_and(lane >= lo, lane < hi)
            kq = jnp.concatenate([jnp.where(in_head, k * bcol, 0.0), jnp.where(in_head, q, 0.0)], axis=0)
            pairs.append(dict(qh=qh, kh=kh, vh=vh, gcol=gcol, bcol=bcol, decay=decay, kb=kh * bcol, kdt=kdt,
                              kq=kq, kt=kt))
    for pr in pairs:
        both = jnp.dot(pr["kq"], pr["kt"], preferred_element_type=F32)
        pr["kk"], pr["qk"] = both[:n], both[n:]
    for pr in pairs:
        lmat = jnp.where(strict, pr["kk"] * pr["decay"], 0.0)
        pr["x"] = eye - lmat
        pr["p"] = lmat
    for pr in pairs:
        pr["p"] = jnp.dot(pr["p"].astype(BF16), pr["p"].astype(BF16), preferred_element_type=F32)
    for step in range(5):
        for pr in pairs:
            if step < 4:
                both = jnp.dot(jnp.concatenate([pr["p"], pr["x"]], axis=0).astype(BF16), pr["p"].astype(BF16),
                               preferred_element_type=F32)
                pr["x"] = pr["x"] + both[n:]
                pr["p"] = both[:n]
            else:
                pr["x"] = pr["x"] + jnp.dot(pr["x"].astype(BF16), pr["p"].astype(BF16), preferred_element_type=F32)
    for pr in pairs:
        pr["egc"] = jnp.exp(pr["gcol"])
        rhs = jnp.concatenate([pr["vh"] * pr["bcol"], pr["kb"] * pr["egc"]], axis=1)
        pr["uw"] = jnp.dot(pr["x"], rhs, preferred_element_type=F32)
    for pr in pairs:
        pr["auw"] = jnp.dot(pr["qk"] * pr["decay"], pr["uw"], preferred_element_type=F32)
        pr["kuw"] = jnp.dot(pr["kdt"], pr["uw"], preferred_element_type=F32)
    for c in range(chunks):
        rows = slice(c * n, (c + 1) * n)
        egc_full = jnp.zeros((n, B_WIDTH), F32)
        for h, pr in enumerate(pairs[c * B_HEADS:(c + 1) * B_HEADS]):
            kuw_ref[rows, h * LANE:(h + 1) * LANE] = pr["kuw"]
            auw_ref[rows, h * LANE:(h + 1) * LANE] = pr["auw"]
            egc_full = jnp.where(jnp.logical_and(lane >= h * HEAD, lane < (h + 1) * HEAD), pr["egc"], egc_full)
        qg_ref[rows, :] = q_ref[rows, :] * egc_full


def _gdn_scan_kernel(kuw_ref, auw_ref, qg_ref, gc_ref, z_ref, nw_ref, y_ref, sfin_ref, s_ref, *, batch):
    c = pl.program_id(0)
    n = GDN_CHUNK

    @pl.when(c == 0)
    def _():
        s_ref[...] = jnp.zeros_like(s_ref)

    nw = nw_ref[...]
    zero = jnp.zeros((HEAD, HEAD), F32)
    pairs = []
    for b in range(batch):
        for h in range(B_HEADS):
            s = s_ref[b, h]
            upper = jnp.concatenate([zero, s], axis=0)
            own = upper if h % 2 else jnp.concatenate([s, zero], axis=0)
            kuw = kuw_ref[b, :, h * LANE:(h + 1) * LANE]
            auw = auw_ref[b, :, h * LANE:(h + 1) * LANE]
            qg = qg_ref[b, :, (h // 2) * LANE:(h // 2 + 1) * LANE]
            pairs.append(dict(b=b, h=h, s=s, upper=upper, own=own, kuw=kuw, auw=auw, qg=qg))
    for pr in pairs:
        pr["ra"] = jnp.dot(jnp.concatenate([pr["kuw"], pr["auw"]], axis=0), pr["upper"], preferred_element_type=F32)
        pr["rb"] = jnp.dot(pr["qg"], pr["own"], preferred_element_type=F32)
    for b in range(batch):
        eg = jnp.exp(gc_ref[b, n - 1:n, :])
        outs = []
        for pr in pairs[b * B_HEADS:(b + 1) * B_HEADS]:
            h = pr["h"]
            s_new = pr["s"] * eg[:, h:h + 1] + pr["kuw"][:, :HEAD] - pr["ra"][:n]
            s_ref[b, h] = s_new
            sfin_ref[b, h] = s_new
            o = pr["rb"] - pr["ra"][n:] + pr["auw"][:, :HEAD]
            ms = jnp.mean(o * o, axis=-1, keepdims=True)
            outs.append(o * lax.rsqrt(ms + EPS) * nw)
        y_ref[b] = jnp.concatenate(outs, axis=1) * _silu(z_ref[b])


def _gdn_prompt(q, k, v, g, beta, z, norm_w, batch, chunks_per_step=4):
    m = q.shape[0]
    t = m // batch
    n = GDN_CHUNK
    nc = t // n
    rows = chunks_per_step * n
    row = lambda w: pl.BlockSpec((rows, w), lambda i: (i, 0))
    tri = jnp.tril(jnp.ones((n, n), F32))
    slots = B_HEADS * LANE
    kuw, auw, qg, gc = pl.pallas_call(
        functools.partial(_gdn_chunk_kernel, chunks=chunks_per_step),
        grid=(m // rows,),
        in_specs=[row(B_WIDTH), row(B_WIDTH), row(B_WIDTH), row(LANE), row(LANE),
                  pl.BlockSpec((n, n), lambda i: (0, 0))],
        out_specs=[row(slots), row(slots), row(B_WIDTH), row(LANE)],
        out_shape=[jax.ShapeDtypeStruct((m, w), F32) for w in (slots, slots, B_WIDTH, LANE)],
        compiler_params=_params(1),
        name="gdn_chunk",
    )(q, k, v, g, beta, tri)
    per_chunk = lambda w: pl.BlockSpec((batch, n, w), lambda c: (0, c, 0))
    state = pl.BlockSpec((batch, B_HEADS, HEAD, HEAD), lambda c: (0, 0, 0, 0))
    view = lambda a: a.reshape(batch, t, a.shape[1])
    y, s_fin = pl.pallas_call(
        functools.partial(_gdn_scan_kernel, batch=batch),
        grid=(nc,),
        in_specs=[per_chunk(slots), per_chunk(slots), per_chunk(B_WIDTH), per_chunk(LANE), per_chunk(B_WIDTH),
                  pl.BlockSpec((1, HEAD), lambda c: (0, 0))],
        out_specs=[per_chunk(B_WIDTH), state],
        out_shape=[jax.ShapeDtypeStruct((batch, t, B_WIDTH), F32),
                   jax.ShapeDtypeStruct((batch, B_HEADS, HEAD, HEAD), F32)],
        scratch_shapes=[pltpu.VMEM((batch, B_HEADS, HEAD, HEAD), F32)],
        compiler_params=_params(1),
        name="gdn_scan",
    )(view(kuw), view(auw), view(qg), view(gc), view(z), norm_w.reshape(1, HEAD))
    return y.reshape(m, B_WIDTH), s_fin


def _gdn_sample_kernel(q_ref, k_ref, v_ref, g_ref, beta_ref, z_ref, nw_ref, s0_ref, y_ref, s_ref, *, steps):
    h = pl.program_id(0)
    bsz = s_ref.shape[-1]
    zero = jnp.zeros((HEAD, bsz), F32)

    def bcast_row(ref, i, dk):
        return jnp.broadcast_to(ref[i, pl.ds(dk, 1), :], (HEAD, bsz))

    def head_row(ref, i):
        return ref[i, pl.ds(h, 1), :]

    def finish(i, o):
        ms = jnp.mean(o * o, axis=0, keepdims=True)
        y_ref[i] = o * lax.rsqrt(ms + EPS) * nw_ref[...] * _silu(z_ref[i])

    gam = jnp.exp(head_row(g_ref, 0))

    def first(dk, ks):
        s = s0_ref[0, dk] * gam
        s_ref[0, dk] = s
        return ks + bcast_row(k_ref, 0, dk) * s

    ks = lax.fori_loop(0, HEAD, first, zero)
    vn = head_row(beta_ref, 0) * (v_ref[0] - ks)

    for i in range(1, steps):
        gam = jnp.exp(head_row(g_ref, i))

        def mid(dk, carry, i=i, vn=vn, gam=gam):
            o, ks = carry
            s = s_ref[0, dk] + bcast_row(k_ref, i - 1, dk) * vn
            o = o + bcast_row(q_ref, i - 1, dk) * s
            s = s * gam
            s_ref[0, dk] = s
            return o, ks + bcast_row(k_ref, i, dk) * s

        o, ks = lax.fori_loop(0, HEAD, mid, (zero, zero))
        finish(i - 1, o)
        vn = head_row(beta_ref, i) * (v_ref[i] - ks)

    def last(dk, o):
        s = s_ref[0, dk] + bcast_row(k_ref, steps - 1, dk) * vn
        s_ref[0, dk] = s
        return o + bcast_row(q_ref, steps - 1, dk) * s

    finish(steps - 1, lax.fori_loop(0, HEAD, last, zero))


def _gdn_sample(qt, kt, vt, gt, bt, zt, nwb, s0t, layer):
    steps, _, bsz = qt.shape
    per_head = pl.BlockSpec((steps, HEAD, bsz), lambda h: (0, h, 0))
    small = pl.BlockSpec((steps, SUBLANE, bsz), lambda h: (0, 0, 0))
    state = pl.BlockSpec((1, HEAD, HEAD, bsz), lambda h: (h, 0, 0, 0))
    state_in = pl.BlockSpec((None, 1, HEAD, HEAD, bsz), lambda h: (layer, h, 0, 0, 0))
    return pl.pallas_call(
        functools.partial(_gdn_sample_kernel, steps=steps),
        grid=(B_HEADS,),
        in_specs=[per_head, per_head, per_head, small, small, per_head,
                  pl.BlockSpec((HEAD, bsz), lambda h: (0, 0)), state_in],
        out_specs=[per_head, state],
        out_shape=[jax.ShapeDtypeStruct((steps, B_WIDTH, bsz), F32),
                   jax.ShapeDtypeStruct((B_HEADS, HEAD, HEAD, bsz), F32)],
        compiler_params=_params(1),
        name="gdn_sample",
    )(qt, kt, vt, gt, bt, zt, nwb, s0t)


def _attn_prompt_kernel(q_ref, k_ref, v_ref, o_ref, lse_ref, *, dil, nb, chains_per_stage):
    n = ATTN_BLOCK
    scale = HEAD ** -0.5
    lane = lax.broadcasted_iota(jnp.int32, (n, GROUP_W), 1)
    head0 = lane < HEAD
    qi = lax.broadcasted_iota(jnp.int32, (n, n), 0)
    kj = lax.broadcasted_iota(jnp.int32, (n, n), 1)
    mask_c = kj <= qi
    mask_p = kj >= qi

    def rows(r, i):
        if dil == 1:
            return pl.ds(i * n, n)
        return pl.ds(r + i * n * dil, n, stride=dil)

    blocks = [(r, i) for r in range(dil) for i in range(nb)]
    for first in range(0, len(blocks), chains_per_stage):
        stage = []
        for r, i in blocks[first:first + chains_per_stage]:
            q = q_ref[rows(r, i), :]
            for sel in (head0, jnp.logical_not(head0)):
                stage.append(dict(r=r, i=i, qh=jnp.where(sel, q, 0.0)))
        for ch in stage:
            r, i = ch["r"], ch["i"]
            ch["sc"] = _dot_nt(ch["qh"], k_ref[rows(r, i), :])
            if i > 0:
                ch["sp"] = _dot_nt(ch["qh"], k_ref[rows(r, i - 1), :])
        for ch in stage:
            sc = jnp.where(mask_c, ch["sc"] * scale, NEG)
            mx = jnp.max(sc, axis=-1, keepdims=True)
            if ch["i"] > 0:
                sp = jnp.where(mask_p, ch["sp"] * scale, NEG)
                mx = jnp.maximum(mx, jnp.max(sp, axis=-1, keepdims=True))
                ch["pp"] = jnp.exp(sp - mx)
            ch["pc"] = jnp.exp(sc - mx)
            den = jnp.sum(ch["pc"], axis=-1, keepdims=True)
            if ch["i"] > 0:
                den = den + jnp.sum(ch["pp"], axis=-1, keepdims=True)
            ch["den"] = den
            ch["lse"] = mx + jnp.log(den)
        for ch in stage:
            r, i = ch["r"], ch["i"]
            acc = jnp.dot(ch["pc"], v_ref[rows(r, i), :], preferred_element_type=F32)
            if i > 0:
                acc = acc + jnp.dot(ch["pp"], v_ref[rows(r, i - 1), :], preferred_element_type=F32)
            ch["o"] = acc / ch["den"]
        for c0, c1 in zip(stage[0::2], stage[1::2]):
            r, i = c0["r"], c0["i"]
            o_ref[rows(r, i), :] = jnp.where(head0, c0["o"], c1["o"])
            lse_ref[rows(r, i), :] = jnp.where(head0, jnp.broadcast_to(c0["lse"], (n, GROUP_W)),
                                               jnp.broadcast_to(c1["lse"], (n, GROUP_W)))


def _attn_prompt(qrot, kv, group, batch, chains_per_stage=4):
    _, dil = GROUPS[group]
    m = qrot.shape[0]
    t = m // batch
    nb = t // (dil * ATTN_BLOCK)
    seq = lambda col: pl.BlockSpec((t, GROUP_W), lambda b: (b, col))
    out = jax.ShapeDtypeStruct((m, GROUP_W), F32)
    return pl.pallas_call(
        functools.partial(_attn_prompt_kernel, dil=dil, nb=nb, chains_per_stage=chains_per_stage),
        grid=(batch,),
        in_specs=[seq(group), seq(0), seq(1)],
        out_specs=[seq(0), seq(0)],
        out_shape=[out, out],
        compiler_params=_params(1),
        name=f"attn_prompt_g{group}",
    )(qrot, kv, kv)


def _attn_sample_kernel(q_ref, n0_ref, n1_ref, n2_ref, c0_ref, c1_ref, c2_ref, o_ref, lse_ref, *, bt, steps):
    rows = 2 * steps
    scale = HEAD ** -0.5
    lane = lax.broadcasted_iota(jnp.int32, (rows, GROUP_W), 1)
    rix = lax.broadcasted_iota(jnp.int32, (rows, GROUP_W), 0)
    own_head = (lane < HEAD) == (rix < steps)
    nkey = lax.broadcasted_iota(jnp.int32, (rows, rows), 1)
    nqry = lax.broadcasted_iota(jnp.int32, (rows, rows), 0) % steps
    new_refs = (n0_ref, n1_ref, n2_ref)
    cache_refs = (c0_ref, c1_ref, c2_ref)

    head0 = lane[:steps] < HEAD
    chains = []
    for b in range(bt):
        q_all = q_ref[b]
        for g, (win, dil) in enumerate(GROUPS):
            qm = jnp.where(own_head, q_all[:, g * GROUP_W:(g + 1) * GROUP_W], 0.0)
            chains.append(dict(b=b, g=g, win=win, dil=dil, qm=qm, new=new_refs[g][b]))
    for ch in chains:
        kt = cache_refs[ch["g"]][0, ch["b"], 0].reshape(GROUP_W, ch["win"])
        ch["s_buf"] = jnp.dot(ch["qm"], kt, preferred_element_type=F32)
        ch["s_new"] = _dot_nt(ch["qm"], ch["new"][:, :GROUP_W])
    for ch in chains:
        win, dil = ch["win"], ch["dil"]
        pos = lax.broadcasted_iota(jnp.int32, (rows, win), 1)
        qry = lax.broadcasted_iota(jnp.int32, (rows, win), 0) % steps
        if dil == 1:
            ok = pos >= qry
            new_ok = jnp.logical_and(nkey < steps, nkey <= nqry)
        else:
            ok = (pos % dil) == qry
            new_ok = nkey == nqry
        s_buf = jnp.where(ok, ch["s_buf"] * scale, NEG)
        s_new = jnp.where(new_ok, ch["s_new"] * scale, NEG)
        mx = jnp.maximum(jnp.max(s_buf, axis=-1, keepdims=True), jnp.max(s_new, axis=-1, keepdims=True))
        ch["p_buf"] = jnp.exp(s_buf - mx)
        ch["p_new"] = jnp.exp(s_new - mx)
        ch["den"] = jnp.sum(ch["p_buf"], axis=-1, keepdims=True) + jnp.sum(ch["p_new"], axis=-1, keepdims=True)
        ch["mx"] = mx
    for ch in chains:
        vt = cache_refs[ch["g"]][0, ch["b"], 1].reshape(GROUP_W, ch["win"])
        ch["acc"] = jnp.dot(ch["p_new"], ch["new"][:, GROUP_W:], preferred_element_type=F32) + _dot_nt(ch["p_buf"], vt)
    for ch in chains:
        b, g = ch["b"], ch["g"]
        o_full = ch["acc"] / ch["den"]
        lse_full = jnp.broadcast_to(ch["mx"] + jnp.log(ch["den"]), (rows, GROUP_W))
        o_ref[b, :, g * GROUP_W:(g + 1) * GROUP_W] = jnp.where(head0, o_full[:steps], o_full[steps:])
        lse_ref[b, :, g * GROUP_W:(g + 1) * GROUP_W] = jnp.where(head0, lse_full[:steps], lse_full[steps:])


def _attn_sample(q2, new_kv, caches, layer, bt):
    bsz, rows, _ = q2.shape
    steps = rows // 2
    per_b = lambda w: pl.BlockSpec((bt, rows, w), lambda i: (i, 0, 0))
    cache_spec = lambda win: pl.BlockSpec((1, bt, 2, HPG, HEAD, win), lambda i: (layer, i, 0, 0, 0, 0))
    out_spec = pl.BlockSpec((bt, steps, C_WIDTH), lambda i: (i, 0, 0))
    out = jax.ShapeDtypeStruct((bsz, steps, C_WIDTH), F32)
    return pl.pallas_call(
        functools.partial(_attn_sample_kernel, bt=bt, steps=steps),
        grid=(bsz // bt,),
        in_specs=[per_b(C_WIDTH), per_b(2 * GROUP_W), per_b(2 * GROUP_W), per_b(2 * GROUP_W)]
        + [cache_spec(win) for win, _ in GROUPS],
        out_specs=[out_spec, out_spec],
        out_shape=[out, out],
        compiler_params=_params(1),
        name="attn_sample",
    )(q2, *new_kv, *caches)


def _outproj_kernel(ya_ref, yb_ref, o0_ref, o1_ref, o2_ref, l0_ref, l1_ref, l2_ref, cz_ref, x_ref, gate_ref,
                    w_ref, *rest, final):
    lses = (l0_ref[...], l1_ref[...], l2_ref[...])
    mx = jnp.maximum(jnp.maximum(lses[0], lses[1]), lses[2])
    es = [jnp.exp(l - mx) for l in lses]
    tot = es[0] + es[1] + es[2]
    cz = cz_ref[...]

    def mm(y, lo):
        return jnp.dot(y.astype(BF16), w_ref[lo:lo + y.shape[1], :], preferred_element_type=F32)

    out = mm(ya_ref[...], 0) + mm(yb_ref[...], A_WIDTH)
    for g, o_ref in enumerate((o0_ref, o1_ref, o2_ref)):
        yc = o_ref[...] * (es[g] / tot) * _silu(cz[:, g * GROUP_W:(g + 1) * GROUP_W])
        out = out + mm(yc, A_WIDTH + B_WIDTH + g * GROUP_W)
    xo = x_ref[...] + (1.0 + gate_ref[0]) * out
    if final:
        fsh_ref, fsc_ref, fnw_ref, xo_ref = rest
        var = jnp.mean(xo * xo, axis=-1, keepdims=True)
        xo = xo * lax.rsqrt(var + EPS) * fnw_ref[...] * (1.0 + fsc_ref[0]) + fsh_ref[0]
    else:
        (xo_ref,) = rest
    xo_ref[...] = xo


def _outproj(ya, yb, os_, lses, cz, x_rows, mod, w_out, layer, tm, final=None):
    m = x_rows.shape[0]
    nblk = m // tm
    nb_mod, r, _ = mod.shape
    tiles_per_mod = nblk // nb_mod
    row = lambda w: pl.BlockSpec((tm, w), lambda i: (i, 0))
    mod_blk = lambda col: pl.BlockSpec((1, r, D_MODEL), lambda i: (i // tiles_per_mod, 0, col))
    extra_specs, extra_args = [], []
    if final is not None:
        extra_specs = [mod_blk(0), mod_blk(1), pl.BlockSpec((1, D_MODEL), lambda i: (0, 0))]
        extra_args = [final[0], final[0], final[1].reshape(1, D_MODEL)]
    return pl.pallas_call(
        functools.partial(_outproj_kernel, final=final is not None),
        grid=(nblk,),
        in_specs=[row(A_WIDTH), row(B_WIDTH)] + [row(GROUP_W)] * 6 + [row(C_WIDTH), row(D_MODEL), mod_blk(2),
                  pl.BlockSpec((None, D_MODEL, D_MODEL), lambda i: (layer, 0, 0))] + extra_specs,
        out_specs=row(D_MODEL),
        out_shape=jax.ShapeDtypeStruct((m, D_MODEL), F32),
        compiler_params=_params(1),
        name="outproj",
    )(ya, yb, *os_, *lses, cz, x_rows, mod, w_out, *extra_args)


def _rope_tables(pos):
    half = ROT_DIM // 2
    inv_freq = ROPE_THETA ** (-jnp.arange(half, dtype=F32) * (2.0 / ROT_DIM))
    ang = pos.astype(F32)[:, None] * inv_freq[None, :]
    cos, sin = jnp.cos(ang), jnp.sin(ang)
    rows = pos.shape[0]
    one = jnp.ones((rows, HEAD - ROT_DIM), F32)
    zero_r = jnp.zeros((rows, HEAD - ROT_DIM), F32)
    zero_h = jnp.zeros((rows, half), F32)
    cos_h = jnp.concatenate([cos, cos, one], axis=1)
    sa_h = jnp.concatenate([-sin, zero_h, zero_r], axis=1)
    sb_h = jnp.concatenate([zero_h, sin, zero_r], axis=1)
    return tuple(jnp.concatenate([t] * HPG, axis=1) for t in (cos_h, sa_h, sb_h))


def _permute_w_in(w_in):
    depth = w_in.shape[0]
    cols = lambda src: w_in[:, :, src[0]:src[1]]
    pad = jnp.zeros((depth, D_MODEL, COL_AB[1] - COL_AB[0] - (SRC_AB[1] - SRC_AB[0])), w_in.dtype)
    return jnp.concatenate([cols(SRC_A), cols(SRC_BQKV), cols(SRC_AB), pad, cols(SRC_BZ), cols(SRC_CZ),
                            cols(SRC_CQKV)], axis=2).astype(BF16)


def _segment_ones():
    idx = jnp.arange(B_WIDTH) // HEAD
    return (idx[:, None] == idx[None, :]).astype(F32)


def _prompt_trunk(x, mods, mod_final, wts, seg):
    batch, t, _ = x.shape
    depth = wts["w_in"].shape[0]
    m = batch * t
    tm = 512
    rows = x.reshape(m, D_MODEL)
    tabs = _rope_tables(jnp.arange(t))
    zeros_a = jnp.zeros((batch, A_CONV - 1, A_WIDTH), F32)
    zeros_b = jnp.zeros((batch, B_CONV - 1, B_QKV), F32)
    st_a, st_b, st_g, kvs = [], [], [], [[] for _ in GROUPS]
    for l in range(depth):
        mod = mods[l].reshape(batch, 1, 3 * D_MODEL)
        ya, q, k, v, g, beta, bz, qrot, kv0, kv1, kv2, cz, sa, sb = _inproj(
            rows, mod, zeros_a, zeros_b, tabs, wts, l, seg, seqs=batch, stride=1, tm=tm)
        yb, s_fin = _gdn_prompt(q, k, v, g, beta, bz, wts["gdn_norm_w"][l], batch)
        os_, lses = [], []
        for gi, kv in enumerate((kv0, kv1, kv2)):
            o, lse = _attn_prompt(qrot, kv, gi, batch)
            os_.append(o)
            lses.append(lse)
            win = min(GROUPS[gi][0], t)
            kvs[gi].append(kv.reshape(batch, t, 2, HPG, HEAD)[:, t - win:])
        final = (mod_final.reshape(batch, 1, 2 * D_MODEL), wts["final_norm_w"]) if l == depth - 1 else None
        rows = _outproj(ya, yb, os_, lses, cz, rows, mod, wts["w_out"], l, tm, final)
        st_a.append(sa)
        st_b.append(sb)
        st_g.append(s_fin)
    y = rows.reshape(batch, t, D_MODEL)
    return y, jnp.stack(st_a), jnp.stack(st_b), jnp.stack(st_g), [jnp.stack(r) for r in kvs]


def _sample_trunk(x, mods, mod_final, state_a, state_b, state_g, caches, past_len, wts, seg):
    bsz, steps, _ = x.shape
    depth = wts["w_in"].shape[0]
    m = bsz * steps
    tm = bsz

    def time_major(a):
        return a.transpose(1, 0, 2).reshape(a.shape[1] * bsz, a.shape[2])

    def batch_major(a, n):
        return a.reshape(n, bsz, a.shape[1]).transpose(1, 0, 2)

    def lanes_batch(a, n):
        return a.reshape(n, bsz, a.shape[1]).transpose(0, 2, 1)

    rows = time_major(x)
    tabs = _rope_tables(jnp.repeat(past_len + jnp.arange(steps), bsz))
    cache_views = [c.transpose(0, 1, 3, 4, 5, 2) for c in caches]
    state_t = state_g.transpose(0, 2, 3, 4, 1)
    st_a, st_b, st_g, kvs = [], [], [], [[] for _ in GROUPS]
    for l in range(depth):
        mod = mods[l][None]
        buf_a = time_major(state_a[l])[None]
        buf_b = time_major(state_b[l])[None]
        ya, q, k, v, g, beta, bz, qrot, kv0, kv1, kv2, cz, sa, sb = _inproj(
            rows, mod, buf_a, buf_b, tabs, wts, l, seg, seqs=1, stride=bsz, tm=m)
        nwb = jnp.broadcast_to(wts["gdn_norm_w"][l][:, None], (HEAD, bsz))
        yt, s_t = _gdn_sample(lanes_batch(q, steps), lanes_batch(k, steps), lanes_batch(v, steps),
                              lanes_batch(g[:, :SUBLANE], steps), lanes_batch(beta[:, :SUBLANE], steps),
                              lanes_batch(bz, steps), nwb, state_t, l)
        yb = yt.transpose(0, 2, 1).reshape(m, B_WIDTH)
        dup = lambda a: jnp.concatenate([batch_major(a, steps)] * 2, axis=1)
        o_c, lse_c = _attn_sample(dup(qrot), [dup(kv) for kv in (kv0, kv1, kv2)], cache_views, l, 4)
        o_c, lse_c = time_major(o_c), time_major(lse_c)
        os_ = [o_c[:, gi * GROUP_W:(gi + 1) * GROUP_W] for gi in range(len(GROUPS))]
        lses = [lse_c[:, gi * GROUP_W:(gi + 1) * GROUP_W] for gi in range(len(GROUPS))]
        final = (mod_final[None], wts["final_norm_w"]) if l == depth - 1 else None
        rows = _outproj(ya, yb, os_, lses, cz, rows, mod, wts["w_out"], l, tm, final)
        st_a.append(batch_major(sa[0], A_CONV - 1))
        st_b.append(batch_major(sb[0], B_CONV - 1))
        st_g.append(s_t.transpose(3, 0, 1, 2))
        for gi, kv in enumerate((kv0, kv1, kv2)):
            kvs[gi].append(batch_major(kv, steps).reshape(bsz, steps, 2, HPG, HEAD))
    y = batch_major(rows, steps)
    return y, jnp.stack(st_a), jnp.stack(st_b), jnp.stack(st_g), [jnp.stack(r) for r in kvs]


def kernel(x_prompt, x_sample, state_conv_a, state_conv_b, state_gdn, cache_kv_w128, cache_kv_w512,
           cache_kv_w2048, c_prompt, c_sample, w_in, w_out, w_ada, b_ada, norm_w, conv_a_w, conv_b_w,
           a_log, dt_bias, gdn_norm_w, final_norm_w, w_ada_final, b_ada_final):
    n_prompt = c_prompt.shape[0]
    c_all = jnp.concatenate([c_prompt, c_sample], axis=0)
    mods = _ada(c_all, w_ada, b_ada)
    mod_final = _ada(c_all, w_ada_final[None], b_ada_final[None])[0]
    wts = dict(w_in=_permute_w_in(w_in), w_out=w_out.astype(BF16), norm_w=norm_w[:, None, :], conv_a_w=conv_a_w,
               conv_b_w=conv_b_w, a_log=a_log, dt_bias=dt_bias, gdn_norm_w=gdn_norm_w,
               final_norm_w=final_norm_w)
    seg = _segment_ones()
    y_p, ca_p, cb_p, g_p, kv_p = _prompt_trunk(x_prompt, mods[:, :n_prompt], mod_final[:n_prompt], wts, seg)
    y_s, ca_s, cb_s, g_s, kv_s = _sample_trunk(
        x_sample, mods[:, n_prompt:], mod_final[n_prompt:], state_conv_a, state_conv_b, state_gdn,
        (cache_kv_w128, cache_kv_w512, cache_kv_w2048), PAST_LEN, wts, seg)
    return (y_p, y_s, ca_p, ca_s, cb_p, cb_s, g_p, g_s,
            kv_p[0], kv_s[0], kv_p[1], kv_s[1], kv_p[2], kv_s[2])
```

```python
import functools
import math

import jax
import jax.numpy as jnp
from jax import lax
from jax.experimental import pallas as pl
from jax.experimental.pallas import tpu as pltpu

F32 = jnp.float32
BF16 = jnp.bfloat16
HIGHEST = lax.Precision.HIGHEST

D_MODEL = 1024
HEAD = 64
A_WIDTH = 256
A_CONV = 3
B_HEADS = 6
B_WIDTH = B_HEADS * HEAD
B_QKV = 3 * B_WIDTH
B_CONV = 4
GDN_CHUNK = 64
GROUPS = ((128, 1), (512, 4), (2048, 16))
HPG = 2
GROUP_W = HPG * HEAD
C_WIDTH = len(GROUPS) * GROUP_W
ATTN_BLOCK = 128
ROT_DIM = HEAD // 4
ROPE_THETA = 500000.0
PAST_LEN = 2048
EPS = 1e-6
NEG = -1e30
LANE = 128
SUBLANE = 8

COL_A = (0, 1024)
COL_BQKV = (1024, 2176)
COL_AB = (2176, 2304)
COL_BZ = (2304, 2688)
COL_CZ = (2688, 3072)
COL_CQ = (3072, 3456)
COL_CK = (3456, 3840)
COL_CV = (3840, 4224)
IN_PERM_WIDTH = 4224
SRC_A = (0, 1024)
SRC_BQKV = (1024, 2176)
SRC_BZ = (2176, 2560)
SRC_AB = (2560, 2572)
SRC_CQKV = (2572, 3724)
SRC_CZ = (3724, 4108)

VMEM_LIMIT_BYTES = 56 * 1024 * 1024


def _params(n_axes):
    return pltpu.CompilerParams(dimension_semantics=("arbitrary",) * n_axes,
                                vmem_limit_bytes=VMEM_LIMIT_BYTES)


def _round_up(x, m):
    return -(-x // m) * m


def _silu(x):
    return x * jax.nn.sigmoid(x)


def _dot_nt(a, b, **kw):
    return lax.dot_general(a, b, (((1,), (1,)), ((), ())), preferred_element_type=F32, **kw)


def _ada_kernel(c_ref, w_ref, b_ref, o_ref):
    c = c_ref[...].astype(BF16)
    w = w_ref[0].astype(BF16)
    o_ref[0] = jnp.dot(c, w, preferred_element_type=F32) + b_ref[0]


def _ada(c_all, w, b):
    n_layers, _, n = w.shape
    r = c_all.shape[0]
    tn = 1024
    return pl.pallas_call(
        _ada_kernel,
        grid=(n_layers, n // tn),
        in_specs=[pl.BlockSpec((r, D_MODEL), lambda l, j: (0, 0)),
                  pl.BlockSpec((1, D_MODEL, tn), lambda l, j: (l, 0, j)),
                  pl.BlockSpec((1, 1, tn), lambda l, j: (l, 0, j))],
        out_specs=pl.BlockSpec((1, r, tn), lambda l, j: (l, 0, j)),
        out_shape=jax.ShapeDtypeStruct((n_layers, r, n), F32),
        compiler_params=_params(2),
        name="ada",
    )(c_all, w, b.reshape(n_layers, 1, n))


def _inproj_kernel(x_ref, sh_ref, sc_ref, nw_ref, w_ref, cos_ref, sa_ref, sb_ref,
                   bufa_ref, bufb_ref, wa_ref, wb_ref, alog_ref, dtb_ref, seg_ref,
                   ya_ref, q_ref, k_ref, v_ref, g_ref, beta_ref, bz_ref, qrot_ref, kv0_ref, kv1_ref, kv2_ref, cz_ref,
                   sta_ref, stb_ref, xpa_ref, xpb_ref, *, stride, tm, tiles_per_seq):
    t = pl.program_id(0) % tiles_per_seq
    x = x_ref[...]
    reps = tm // sh_ref.shape[1] if sh_ref.shape[1] > 1 else 1
    tile_rows = lambda a: a if reps == 1 else jnp.concatenate([a] * reps, axis=0)
    var = jnp.mean(x * x, axis=-1, keepdims=True)
    hn = x * lax.rsqrt(var + EPS) * nw_ref[...]
    hn = hn * (1.0 + tile_rows(sc_ref[0])) + tile_rows(sh_ref[0])
    hb = hn.astype(BF16)

    def mm(lo, hi):
        return jnp.dot(hb, w_ref[:, lo:hi], preferred_element_type=F32)

    def halo_rows(taps):
        halo = (taps - 1) * stride
        return halo, _round_up(halo, SUBLANE)

    def load_halo(xp_ref, buf_ref, taps):
        halo, x0 = halo_rows(taps)

        @pl.when(t == 0)
        def _():
            xp_ref[x0 - halo:x0, :] = buf_ref[0]

        @pl.when(t > 0)
        def _():
            xp_ref[x0 - halo:x0, :] = xp_ref[x0 + tm - halo:x0 + tm, :]

    def conv_cols(xp_ref, x_new, w_ref, taps, lo, hi):
        halo, x0 = halo_rows(taps)
        xp_ref[x0:x0 + tm, lo:hi] = x_new
        y = xp_ref[x0 - halo:x0 - halo + tm, lo:hi] * w_ref[0:1, lo:hi]
        for j in range(1, taps):
            r0 = x0 - halo + j * stride
            y = y + xp_ref[r0:r0 + tm, lo:hi] * w_ref[j:j + 1, lo:hi]
        return y

    def store_tail(xp_ref, st_ref, taps):
        halo, x0 = halo_rows(taps)
        st_ref[0] = xp_ref[x0 + tm - halo:x0 + tm, :]

    load_halo(xpa_ref, bufa_ref, A_CONV)
    load_halo(xpb_ref, bufb_ref, B_CONV)
    seg = seg_ref[...].astype(BF16)
    cos = cos_ref[...]
    sa = sa_ref[...]
    sb = sb_ref[...]

    def rope(u):
        return u * cos + pltpu.roll(u, LANE - ROT_DIM // 2, 1) * sa + pltpu.roll(u, ROT_DIM // 2, 1) * sb

    def branch_b(raw, lo, hi):
        act = _silu(conv_cols(xpb_ref, raw, wb_ref, B_CONV, lo, hi))
        if lo < 2 * B_WIDTH:
            sq = act * act
            hi16 = sq.astype(BF16)
            lo16 = (sq - hi16.astype(F32)).astype(BF16)
            blk = seg[:hi - lo, :hi - lo]
            ss = jnp.dot(hi16, blk, preferred_element_type=F32) + jnp.dot(lo16, blk, preferred_element_type=F32)
            act = act * lax.rsqrt(ss + 1e-6)
        for out_ref, base, scale in ((q_ref, 0, HEAD ** -0.5), (k_ref, B_WIDTH, None), (v_ref, 2 * B_WIDTH, None)):
            c0, c1 = max(lo, base), min(hi, base + B_WIDTH)
            if c0 < c1:
                piece = act[:, c0 - lo:c1 - lo]
                out_ref[:, c0 - base:c1 - base] = piece if scale is None else piece * scale

    def use_b_tail(raw):
        n_v = B_QKV - 4 * wide
        branch_b(raw[:, :n_v], 4 * wide, B_QKV)
        ab = raw[:, n_v:]
        z = ab + dtb_ref[...]
        softplus = jnp.maximum(z, 0.0) + jnp.log1p(jnp.exp(-jnp.abs(z)))
        g_ref[...] = -jnp.exp(alog_ref[...]) * softplus
        beta_ref[...] = jax.nn.sigmoid(pltpu.roll(ab, LANE - B_HEADS, 1))

    def use_gates(raw):
        bz_ref[...] = raw[:, :B_WIDTH]
        cz_ref[...] = raw[:, B_WIDTH:]

    def use_c(raw):
        for g, kv_ref in enumerate((kv0_ref, kv1_ref, kv2_ref)):
            lo, hi = g * GROUP_W, (g + 1) * GROUP_W
            qrot_ref[:, lo:hi] = rope(raw[:, lo:hi])
            kv_ref[:, :GROUP_W] = rope(raw[:, C_WIDTH + lo:C_WIDTH + hi])
            kv_ref[:, GROUP_W:] = raw[:, 2 * C_WIDTH + lo:2 * C_WIDTH + hi]

    conv_a = []

    def use_a_in(raw):
        conv_a.append(conv_cols(xpa_ref, raw[:, A_WIDTH:] * raw[:, :A_WIDTH], wa_ref, A_CONV, 0, A_WIDTH))

    def use_a_out(raw):
        ya_ref[...] = raw[:, :A_WIDTH] * conv_a[0] * _silu(raw[:, A_WIDTH:])

    wide = 2 * LANE
    b0, a0 = COL_BQKV[0], COL_A[0]
    b_chunk = lambda c: ((b0 + c * wide, b0 + (c + 1) * wide), lambda raw: branch_b(raw, c * wide, (c + 1) * wide))
    jobs = [b_chunk(0), ((COL_BZ[0], COL_CZ[1]), use_gates), b_chunk(1), ((COL_CQ[0], COL_CV[1]), use_c),
            b_chunk(2), ((a0, a0 + 2 * A_WIDTH), use_a_in), b_chunk(3),
            ((a0 + 2 * A_WIDTH, a0 + 4 * A_WIDTH), use_a_out), ((b0 + 4 * wide, COL_AB[1]), use_b_tail)]
    pending = mm(*jobs[0][0])
    for i, (_, consume) in enumerate(jobs):
        ahead = mm(*jobs[i + 1][0]) if i + 1 < len(jobs) else None
        consume(pending)
        pending = ahead
    store_tail(xpb_ref, stb_ref, B_CONV)
    store_tail(xpa_ref, sta_ref, A_CONV)


def _inproj(x_rows, mod, buf_a, buf_b, rope_tabs, wts, layer, seg, *, seqs, stride, tm):
    m = x_rows.shape[0]
    nblk = m // tm
    tiles_per_seq = nblk // seqs
    nb_mod, r, _ = mod.shape
    tiles_per_mod = nblk // nb_mod
    nt_tab = rope_tabs[0].shape[0] // tm
    halo_a, halo_b = (A_CONV - 1) * stride, (B_CONV - 1) * stride
    row = lambda w: pl.BlockSpec((tm, w), lambda i: (i, 0))
    tab = pl.BlockSpec((tm, LANE), lambda i: (i % nt_tab, 0))
    full = lambda a: pl.BlockSpec(a.shape, lambda i: (0,) * a.ndim)
    per_layer = lambda a: pl.BlockSpec((None,) + a.shape[1:], lambda i: (layer,) + (0,) * (a.ndim - 1))
    per_seq = lambda h, w: pl.BlockSpec((1, h, w), lambda i: (i // tiles_per_seq, 0, 0))
    alog = jnp.zeros((1, LANE), F32).at[0, :B_HEADS].set(wts["a_log"][layer])
    dtb = jnp.zeros((1, LANE), F32).at[0, :B_HEADS].set(wts["dt_bias"][layer])
    out_w = (A_WIDTH, B_WIDTH, B_WIDTH, B_WIDTH, LANE, LANE, B_WIDTH, C_WIDTH, 2 * GROUP_W, 2 * GROUP_W, 2 * GROUP_W,
             C_WIDTH)
    return pl.pallas_call(
        functools.partial(_inproj_kernel, stride=stride, tm=tm, tiles_per_seq=tiles_per_seq),
        grid=(nblk,),
        in_specs=[row(D_MODEL),
                  pl.BlockSpec((1, r, D_MODEL), lambda i: (i // tiles_per_mod, 0, 0)),
                  pl.BlockSpec((1, r, D_MODEL), lambda i: (i // tiles_per_mod, 0, 1)),
                  per_layer(wts["norm_w"]), per_layer(wts["w_in"]), tab, tab, tab,
                  per_seq(halo_a, A_WIDTH), per_seq(halo_b, B_QKV),
                  per_layer(wts["conv_a_w"]), per_layer(wts["conv_b_w"]), full(alog), full(dtb), full(seg)],
        out_specs=[row(w) for w in out_w] + [per_seq(halo_a, A_WIDTH), per_seq(halo_b, B_QKV)],
        out_shape=[jax.ShapeDtypeStruct((m, w), F32) for w in out_w]
        + [jax.ShapeDtypeStruct((seqs, halo_a, A_WIDTH), F32), jax.ShapeDtypeStruct((seqs, halo_b, B_QKV), F32)],
        scratch_shapes=[pltpu.VMEM((_round_up(halo_a, SUBLANE) + tm, A_WIDTH), F32),
                        pltpu.VMEM((_round_up(halo_b, SUBLANE) + tm, B_QKV), F32)],
        compiler_params=_params(1),
        name="inproj",
    )(x_rows, mod, mod, wts["norm_w"], wts["w_in"], *rope_tabs, buf_a, buf_b,
      wts["conv_a_w"], wts["conv_b_w"], alog, dtb, seg)


def _gdn_chunk_kernel(q_ref, k_ref, v_ref, g_ref, beta_ref, tri_ref,
                      kuw_ref, auw_ref, qg_ref, gc_ref, *, chunks):
    n = GDN_CHUNK
    ri = lax.broadcasted_iota(jnp.int32, (n, n), 0)
    ci = lax.broadcasted_iota(jnp.int32, (n, n), 1)
    tri = ri >= ci
    strict = ri > ci
    eye = jnp.where(ri == ci, 1.0, 0.0).astype(F32)
    lane = lax.broadcasted_iota(jnp.int32, (n, B_WIDTH), 1)
    pairs = []
    for c in range(chunks):
        rows = slice(c * n, (c + 1) * n)
        q = q_ref[rows, :]
        k = k_ref[rows, :]
        v = v_ref[rows, :]
        beta = beta_ref[rows, :]
        gc = jnp.dot(tri_ref[...], g_ref[rows, :], preferred_element_type=F32, precision=HIGHEST)
        gc_ref[rows, :] = gc
        gct = gc.T
        kt = k.T
        for h in range(B_HEADS):
            lo, hi = h * HEAD, (h + 1) * HEAD
            qh, kh, vh = q[:, lo:hi], k[:, lo:hi], v[:, lo:hi]
            gcol = gc[:, h:h + 1]
            grow = gct[h:h + 1, :]
            bcol = beta[:, h:h + 1]
            decay = jnp.where(tri, jnp.exp(jnp.where(tri, gcol - grow, 0.0)), 0.0)
            kdt = kt[lo:hi, :] * jnp.exp(gcol[n - 1:n, :] - grow)
            in_head = jnp.logical_and(lane >= lo, lane < hi)
            kq = jnp.concatenate([jnp.where(in_head, k * bcol, 0.0), jnp.where(in_head, q, 0.0)], axis=0)
            pairs.append(dict(qh=qh, kh=kh, vh=vh, gcol=gcol, bcol=bcol, decay=decay, kb=kh * bcol, kdt=kdt,
                              kq=kq, kt=kt))
    for pr in pairs:
        both = jnp.dot(pr["kq"], pr["kt"], preferred_element_type=F32)
        pr["kk"], pr["qk"] = both[:n], both[n:]
    for pr in pairs:
        lmat = jnp.where(strict, pr["kk"] * pr["decay"], 0.0)
        pr["x"] = eye - lmat
        pr["p"] = lmat
    for pr in pairs:
        pr["p"] = jnp.dot(pr["p"].astype(BF16), pr["p"].astype(BF16), preferred_element_type=F32)
    for step in range(5):
        for pr in pairs:
            if step < 4:
                both = jnp.dot(jnp.concatenate([pr["p"], pr["x"]], axis=0).astype(BF16), pr["p"].astype(BF16),
                               preferred_element_type=F32)
                pr["x"] = pr["x"] + both[n:]
                pr["p"] = both[:n]
            else:
                pr["x"] = pr["x"] + jnp.dot(pr["x"].astype(BF16), pr["p"].astype(BF16), preferred_element_type=F32)
    for pr in pairs:
        pr["egc"] = jnp.exp(pr["gcol"])
        rhs = jnp.concatenate([pr["vh"] * pr["bcol"], pr["kb"] * pr["egc"]], axis=1)
        pr["uw"] = jnp.dot(pr["x"], rhs, preferred_element_type=F32)
    for pr in pairs:
        pr["auw"] = jnp.dot(pr["qk"] * pr["decay"], pr["uw"], preferred_element_type=F32)
        pr["kuw"] = jnp.dot(pr["kdt"], pr["uw"], preferred_element_type=F32)
    for c in range(chunks):
        rows = slice(c * n, (c + 1) * n)
        egc_full = jnp.zeros((n, B_WIDTH), F32)
        for h, pr in enumerate(pairs[c * B_HEADS:(c + 1) * B_HEADS]):
            kuw_ref[rows, h * LANE:(h + 1) * LANE] = pr["kuw"]
            auw_ref[rows, h * LANE:(h + 1) * LANE] = pr["auw"]
            egc_full = jnp.where(jnp.logical_and(lane >= h * HEAD, lane < (h + 1) * HEAD), pr["egc"], egc_full)
        qg_ref[rows, :] = q_ref[rows, :] * egc_full


def _gdn_scan_kernel(kuw_ref, auw_ref, qg_ref, gc_ref, z_ref, nw_ref, y_ref, sfin_ref, s_ref, *, batch):
    c = pl.program_id(0)
    n = GDN_CHUNK

    @pl.when(c == 0)
    def _():
        s_ref[...] = jnp.zeros_like(s_ref)

    nw = nw_ref[...]
    zero = jnp.zeros((HEAD, HEAD), F32)
    pairs = []
    for b in range(batch):
        for h in range(B_HEADS):
            s = s_ref[b, h]
            upper = jnp.concatenate([zero, s], axis=0)
            own = upper if h % 2 else jnp.concatenate([s, zero], axis=0)
            kuw = kuw_ref[b, :, h * LANE:(h + 1) * LANE]
            auw = auw_ref[b, :, h * LANE:(h + 1) * LANE]
            qg = qg_ref[b, :, (h // 2) * LANE:(h // 2 + 1) * LANE]
            pairs.append(dict(b=b, h=h, s=s, upper=upper, own=own, kuw=kuw, auw=auw, qg=qg))
    for pr in pairs:
        pr["ra"] = jnp.dot(jnp.concatenate([pr["kuw"], pr["auw"]], axis=0), pr["upper"], preferred_element_type=F32)
        pr["rb"] = jnp.dot(pr["qg"], pr["own"], preferred_element_type=F32)
    for b in range(batch):
        eg = jnp.exp(gc_ref[b, n - 1:n, :])
        outs = []
        for pr in pairs[b * B_HEADS:(b + 1) * B_HEADS]:
            h = pr["h"]
            s_new = pr["s"] * eg[:, h:h + 1] + pr["kuw"][:, :HEAD] - pr["ra"][:n]
            s_ref[b, h] = s_new
            sfin_ref[b, h] = s_new
            o = pr["rb"] - pr["ra"][n:] + pr["auw"][:, :HEAD]
            ms = jnp.mean(o * o, axis=-1, keepdims=True)
            outs.append(o * lax.rsqrt(ms + EPS) * nw)
        y_ref[b] = jnp.concatenate(outs, axis=1) * _silu(z_ref[b])


def _gdn_prompt(q, k, v, g, beta, z, norm_w, batch, chunks_per_step=8):
    m = q.shape[0]
    t = m // batch
    n = GDN_CHUNK
    nc = t // n
    rows = chunks_per_step * n
    row = lambda w: pl.BlockSpec((rows, w), lambda i: (i, 0))
    tri = jnp.tril(jnp.ones((n, n), F32))
    slots = B_HEADS * LANE
    kuw, auw, qg, gc = pl.pallas_call(
        functools.partial(_gdn_chunk_kernel, chunks=chunks_per_step),
        grid=(m // rows,),
        in_specs=[row(B_WIDTH), row(B_WIDTH), row(B_WIDTH), row(LANE), row(LANE),
                  pl.BlockSpec((n, n), lambda i: (0, 0))],
        out_specs=[row(slots), row(slots), row(B_WIDTH), row(LANE)],
        out_shape=[jax.ShapeDtypeStruct((m, w), F32) for w in (slots, slots, B_WIDTH, LANE)],
        compiler_params=_params(1),
        name="gdn_chunk",
    )(q, k, v, g, beta, tri)
    per_chunk = lambda w: pl.BlockSpec((batch, n, w), lambda c: (0, c, 0))
    state = pl.BlockSpec((batch, B_HEADS, HEAD, HEAD), lambda c: (0, 0, 0, 0))
    view = lambda a: a.reshape(batch, t, a.shape[1])
    y, s_fin = pl.pallas_call(
        functools.partial(_gdn_scan_kernel, batch=batch),
        grid=(nc,),
        in_specs=[per_chunk(slots), per_chunk(slots), per_chunk(B_WIDTH), per_chunk(LANE), per_chunk(B_WIDTH),
                  pl.BlockSpec((1, HEAD), lambda c: (0, 0))],
        out_specs=[per_chunk(B_WIDTH), state],
        out_shape=[jax.ShapeDtypeStruct((batch, t, B_WIDTH), F32),
                   jax.ShapeDtypeStruct((batch, B_HEADS, HEAD, HEAD), F32)],
        scratch_shapes=[pltpu.VMEM((batch, B_HEADS, HEAD, HEAD), F32)],
        compiler_params=_params(1),
        name="gdn_scan",
    )(view(kuw), view(auw), view(qg), view(gc), view(z), norm_w.reshape(1, HEAD))
    return y.reshape(m, B_WIDTH), s_fin


def _gdn_sample_kernel(q_ref, k_ref, v_ref, g_ref, beta_ref, z_ref, nw_ref, s0_ref, y_ref, s_ref, *, steps):
    h = pl.program_id(0)
    bsz = s_ref.shape[-1]
    zero = jnp.zeros((HEAD, bsz), F32)

    def bcast_row(ref, i, dk):
        return jnp.broadcast_to(ref[i, pl.ds(dk, 1), :], (HEAD, bsz))

    def head_row(ref, i):
        return ref[i, pl.ds(h, 1), :]

    def finish(i, o):
        ms = jnp.mean(o * o, axis=0, keepdims=True)
        y_ref[i] = o * lax.rsqrt(ms + EPS) * nw_ref[...] * _silu(z_ref[i])

    gam = jnp.exp(head_row(g_ref, 0))

    def first(dk, ks):
        s = s0_ref[0, dk] * gam
        s_ref[0, dk] = s
        return ks + bcast_row(k_ref, 0, dk) * s

    ks = lax.fori_loop(0, HEAD, first, zero)
    vn = head_row(beta_ref, 0) * (v_ref[0] - ks)

    for i in range(1, steps):
        gam = jnp.exp(head_row(g_ref, i))

        def mid(dk, carry, i=i, vn=vn, gam=gam):
            o, ks = carry
            s = s_ref[0, dk] + bcast_row(k_ref, i - 1, dk) * vn
            o = o + bcast_row(q_ref, i - 1, dk) * s
            s = s * gam
            s_ref[0, dk] = s
            return o, ks + bcast_row(k_ref, i, dk) * s

        o, ks = lax.fori_loop(0, HEAD, mid, (zero, zero))
        finish(i - 1, o)
        vn = head_row(beta_ref, i) * (v_ref[i] - ks)

    def last(dk, o):
        s = s_ref[0, dk] + bcast_row(k_ref, steps - 1, dk) * vn
        s_ref[0, dk] = s
        return o + bcast_row(q_ref, steps - 1, dk) * s

    finish(steps - 1, lax.fori_loop(0, HEAD, last, zero))


def _gdn_sample(qt, kt, vt, gt, bt, zt, nwb, s0t, layer):
    steps, _, bsz = qt.shape
    per_head = pl.BlockSpec((steps, HEAD, bsz), lambda h: (0, h, 0))
    small = pl.BlockSpec((steps, SUBLANE, bsz), lambda h: (0, 0, 0))
    state = pl.BlockSpec((1, HEAD, HEAD, bsz), lambda h: (h, 0, 0, 0))
    state_in = pl.BlockSpec((None, 1, HEAD, HEAD, bsz), lambda h: (layer, h, 0, 0, 0))
    return pl.pallas_call(
        functools.partial(_gdn_sample_kernel, steps=steps),
        grid=(B_HEADS,),
        in_specs=[per_head, per_head, per_head, small, small, per_head,
                  pl.BlockSpec((HEAD, bsz), lambda h: (0, 0)), state_in],
        out_specs=[per_head, state],
        out_shape=[jax.ShapeDtypeStruct((steps, B_WIDTH, bsz), F32),
                   jax.ShapeDtypeStruct((B_HEADS, HEAD, HEAD, bsz), F32)],
        compiler_params=_params(1),
        name="gdn_sample",
    )(qt, kt, vt, gt, bt, zt, nwb, s0t)


def _attn_prompt_kernel(q_ref, k_ref, v_ref, o_ref, lse_ref, *, dil, nb, chains_per_stage):
    n = ATTN_BLOCK
    scale = HEAD ** -0.5
    lane = lax.broadcasted_iota(jnp.int32, (n, GROUP_W), 1)
    head0 = lane < HEAD
    qi = lax.broadcasted_iota(jnp.int32, (n, n), 0)
    kj = lax.broadcasted_iota(jnp.int32, (n, n), 1)
    mask_c = kj <= qi
    mask_p = kj >= qi

    def rows(r, i):
        if dil == 1:
            return pl.ds(i * n, n)
        return pl.ds(r + i * n * dil, n, stride=dil)

    blocks = [(r, i) for r in range(dil) for i in range(nb)]
    for first in range(0, len(blocks), chains_per_stage):
        stage = []
        for r, i in blocks[first:first + chains_per_stage]:
            q = q_ref[rows(r, i), :]
            for sel in (head0, jnp.logical_not(head0)):
                stage.append(dict(r=r, i=i, qh=jnp.where(sel, q, 0.0)))
        for ch in stage:
            r, i = ch["r"], ch["i"]
            ch["sc"] = _dot_nt(ch["qh"], k_ref[rows(r, i), :])
            if i > 0:
                ch["sp"] = _dot_nt(ch["qh"], k_ref[rows(r, i - 1), :])
        for ch in stage:
            sc = jnp.where(mask_c, ch["sc"] * scale, NEG)
            mx = jnp.max(sc, axis=-1, keepdims=True)
            if ch["i"] > 0:
                sp = jnp.where(mask_p, ch["sp"] * scale, NEG)
                mx = jnp.maximum(mx, jnp.max(sp, axis=-1, keepdims=True))
                ch["pp"] = jnp.exp(sp - mx)
            ch["pc"] = jnp.exp(sc - mx)
            den = jnp.sum(ch["pc"], axis=-1, keepdims=True)
            if ch["i"] > 0:
                den = den + jnp.sum(ch["pp"], axis=-1, keepdims=True)
            ch["den"] = den
            ch["lse"] = mx + jnp.log(den)
        for ch in stage:
            r, i = ch["r"], ch["i"]
            acc = jnp.dot(ch["pc"], v_ref[rows(r, i), :], preferred_element_type=F32)
            if i > 0:
                acc = acc + jnp.dot(ch["pp"], v_ref[rows(r, i - 1), :], preferred_element_type=F32)
            ch["o"] = acc / ch["den"]
        for c0, c1 in zip(stage[0::2], stage[1::2]):
            r, i = c0["r"], c0["i"]
            o_ref[rows(r, i), :] = jnp.where(head0, c0["o"], c1["o"])
            lse_ref[rows(r, i), :] = jnp.where(head0, jnp.broadcast_to(c0["lse"], (n, GROUP_W)),
                                               jnp.broadcast_to(c1["lse"], (n, GROUP_W)))


def _attn_prompt(qrot, kv, group, batch, chains_per_stage=4):
    _, dil = GROUPS[group]
    m = qrot.shape[0]
    t = m // batch
    nb = t // (dil * ATTN_BLOCK)
    seq = lambda col: pl.BlockSpec((t, GROUP_W), lambda b: (b, col))
    out = jax.ShapeDtypeStruct((m, GROUP_W), F32)
    return pl.pallas_call(
        functools.partial(_attn_prompt_kernel, dil=dil, nb=nb, chains_per_stage=chains_per_stage),
        grid=(batch,),
        in_specs=[seq(group), seq(0), seq(1)],
        out_specs=[seq(0), seq(0)],
        out_shape=[out, out],
        compiler_params=_params(1),
        name=f"attn_prompt_g{group}",
    )(qrot, kv, kv)


def _attn_sample_kernel(q_ref, n0_ref, n1_ref, n2_ref, c0_ref, c1_ref, c2_ref, o_ref, lse_ref, *, bt, steps):
    rows = 2 * steps
    scale = HEAD ** -0.5
    lane = lax.broadcasted_iota(jnp.int32, (rows, GROUP_W), 1)
    rix = lax.broadcasted_iota(jnp.int32, (rows, GROUP_W), 0)
    own_head = (lane < HEAD) == (rix < steps)
    nkey = lax.broadcasted_iota(jnp.int32, (rows, rows), 1)
    nqry = lax.broadcasted_iota(jnp.int32, (rows, rows), 0) % steps
    new_refs = (n0_ref, n1_ref, n2_ref)
    cache_refs = (c0_ref, c1_ref, c2_ref)

    head0 = lane[:steps] < HEAD
    chains = []
    for b in range(bt):
        q_all = q_ref[b]
        for g, (win, dil) in enumerate(GROUPS):
            qm = jnp.where(own_head, q_all[:, g * GROUP_W:(g + 1) * GROUP_W], 0.0)
            chains.append(dict(b=b, g=g, win=win, dil=dil, qm=qm, new=new_refs[g][b]))
    for ch in chains:
        kt = cache_refs[ch["g"]][0, ch["b"], 0].reshape(GROUP_W, ch["win"])
        ch["s_buf"] = jnp.dot(ch["qm"], kt, preferred_element_type=F32)
        ch["s_new"] = _dot_nt(ch["qm"], ch["new"][:, :GROUP_W])
    for ch in chains:
        win, dil = ch["win"], ch["dil"]
        pos = lax.broadcasted_iota(jnp.int32, (rows, win), 1)
        qry = lax.broadcasted_iota(jnp.int32, (rows, win), 0) % steps
        if dil == 1:
            ok = pos >= qry
            new_ok = jnp.logical_and(nkey < steps, nkey <= nqry)
        else:
            ok = (pos % dil) == qry
            new_ok = nkey == nqry
        s_buf = jnp.where(ok, ch["s_buf"] * scale, NEG)
        s_new = jnp.where(new_ok, ch["s_new"] * scale, NEG)
        mx = jnp.maximum(jnp.max(s_buf, axis=-1, keepdims=True), jnp.max(s_new, axis=-1, keepdims=True))
        ch["p_buf"] = jnp.exp(s_buf - mx)
        ch["p_new"] = jnp.exp(s_new - mx)
        ch["den"] = jnp.sum(ch["p_buf"], axis=-1, keepdims=True) + jnp.sum(ch["p_new"], axis=-1, keepdims=True)
        ch["mx"] = mx
    for ch in chains:
        vt = cache_refs[ch["g"]][0, ch["b"], 1].reshape(GROUP_W, ch["win"])
        ch["acc"] = jnp.dot(ch["p_new"], ch["new"][:, GROUP_W:], preferred_element_type=F32) + _dot_nt(ch["p_buf"], vt)
    for ch in chains:
        b, g = ch["b"], ch["g"]
        o_full = ch["acc"] / ch["den"]
        lse_full = jnp.broadcast_to(ch["mx"] + jnp.log(ch["den"]), (rows, GROUP_W))
        o_ref[b, :, g * GROUP_W:(g + 1) * GROUP_W] = jnp.where(head0, o_full[:steps], o_full[steps:])
        lse_ref[b, :, g * GROUP_W:(g + 1) * GROUP_W] = jnp.where(head0, lse_full[:steps], lse_full[steps:])


def _attn_sample(q2, new_kv, caches, layer, bt):
    bsz, rows, _ = q2.shape
    steps = rows // 2
    per_b = lambda w: pl.BlockSpec((bt, rows, w), lambda i: (i, 0, 0))
    cache_spec = lambda win: pl.BlockSpec((1, bt, 2, HPG, HEAD, win), lambda i: (layer, i, 0, 0, 0, 0))
    out_spec = pl.BlockSpec((bt, steps, C_WIDTH), lambda i: (i, 0, 0))
    out = jax.ShapeDtypeStruct((bsz, steps, C_WIDTH), F32)
    return pl.pallas_call(
        functools.partial(_attn_sample_kernel, bt=bt, steps=steps),
        grid=(bsz // bt,),
        in_specs=[per_b(C_WIDTH), per_b(2 * GROUP_W), per_b(2 * GROUP_W), per_b(2 * GROUP_W)]
        + [cache_spec(win) for win, _ in GROUPS],
        out_specs=[out_spec, out_spec],
        out_shape=[out, out],
        compiler_params=_params(1),
        name="attn_sample",
    )(q2, *new_kv, *caches)


def _outproj_kernel(ya_ref, yb_ref, o0_ref, o1_ref, o2_ref, l0_ref, l1_ref, l2_ref, cz_ref, x_ref, gate_ref,
                    w_ref, *rest, final):
    lses = (l0_ref[...], l1_ref[...], l2_ref[...])
    mx = jnp.maximum(jnp.maximum(lses[0], lses[1]), lses[2])
    es = [jnp.exp(l - mx) for l in lses]
    tot = es[0] + es[1] + es[2]
    cz = cz_ref[...]

    def mm(y, lo):
        return jnp.dot(y.astype(BF16), w_ref[lo:lo + y.shape[1], :], preferred_element_type=F32)

    out = mm(ya_ref[...], 0) + mm(yb_ref[...], A_WIDTH)
    for g, o_ref in enumerate((o0_ref, o1_ref, o2_ref)):
        yc = o_ref[...] * (es[g] / tot) * _silu(cz[:, g * GROUP_W:(g + 1) * GROUP_W])
        out = out + mm(yc, A_WIDTH + B_WIDTH + g * GROUP_W)
    xo = x_ref[...] + (1.0 + gate_ref[0]) * out
    if final:
        fsh_ref, fsc_ref, fnw_ref, xo_ref = rest
        var = jnp.mean(xo * xo, axis=-1, keepdims=True)
        xo = xo * lax.rsqrt(var + EPS) * fnw_ref[...] * (1.0 + fsc_ref[0]) + fsh_ref[0]
    else:
        (xo_ref,) = rest
    xo_ref[...] = xo


def _outproj(ya, yb, os_, lses, cz, x_rows, mod, w_out, layer, tm, final=None):
    m = x_rows.shape[0]
    nblk = m // tm
    nb_mod, r, _ = mod.shape
    tiles_per_mod = nblk // nb_mod
    row = lambda w: pl.BlockSpec((tm, w), lambda i: (i, 0))
    mod_blk = lambda col: pl.BlockSpec((1, r, D_MODEL), lambda i: (i // tiles_per_mod, 0, col))
    extra_specs, extra_args = [], []
    if final is not None:
        extra_specs = [mod_blk(0), mod_blk(1), pl.BlockSpec((1, D_MODEL), lambda i: (0, 0))]
        extra_args = [final[0], final[0], final[1].reshape(1, D_MODEL)]
    return pl.pallas_call(
        functools.partial(_outproj_kernel, final=final is not None),
        grid=(nblk,),
        in_specs=[row(A_WIDTH), row(B_WIDTH)] + [row(GROUP_W)] * 6 + [row(C_WIDTH), row(D_MODEL), mod_blk(2),
                  pl.BlockSpec((None, D_MODEL, D_MODEL), lambda i: (layer, 0, 0))] + extra_specs,
        out_specs=row(D_MODEL),
        out_shape=jax.ShapeDtypeStruct((m, D_MODEL), F32),
        compiler_params=_params(1),
        name="outproj",
    )(ya, yb, *os_, *lses, cz, x_rows, mod, w_out, *extra_args)


def _rope_tables(pos):
    half = ROT_DIM // 2
    inv_freq = ROPE_THETA ** (-jnp.arange(half, dtype=F32) * (2.0 / ROT_DIM))
    ang = pos.astype(F32)[:, None] * inv_freq[None, :]
    cos, sin = jnp.cos(ang), jnp.sin(ang)
    rows = pos.shape[0]
    one = jnp.ones((rows, HEAD - ROT_DIM), F32)
    zero_r = jnp.zeros((rows, HEAD - ROT_DIM), F32)
    zero_h = jnp.zeros((rows, half), F32)
    cos_h = jnp.concatenate([cos, cos, one], axis=1)
    sa_h = jnp.concatenate([-sin, zero_h, zero_r], axis=1)
    sb_h = jnp.concatenate([zero_h, sin, zero_r], axis=1)
    return tuple(jnp.concatenate([t] * HPG, axis=1) for t in (cos_h, sa_h, sb_h))


def _permute_w_in(w_in):
    depth = w_in.shape[0]
    cols = lambda src: w_in[:, :, src[0]:src[1]]
    pad = jnp.zeros((depth, D_MODEL, COL_AB[1] - COL_AB[0] - (SRC_AB[1] - SRC_AB[0])), w_in.dtype)
    return jnp.concatenate([cols(SRC_A), cols(SRC_BQKV), cols(SRC_AB), pad, cols(SRC_BZ), cols(SRC_CZ),
                            cols(SRC_CQKV)], axis=2).astype(BF16)


def _segment_ones():
    idx = jnp.arange(B_WIDTH) // HEAD
    return (idx[:, None] == idx[None, :]).astype(F32)


def _prompt_trunk(x, mods, mod_final, wts, seg):
    batch, t, _ = x.shape
    depth = wts["w_in"].shape[0]
    m = batch * t
    tm = 512
    rows = x.reshape(m, D_MODEL)
    tabs = _rope_tables(jnp.arange(t))
    zeros_a = jnp.zeros((batch, A_CONV - 1, A_WIDTH), F32)
    zeros_b = jnp.zeros((batch, B_CONV - 1, B_QKV), F32)
    st_a, st_b, st_g, kvs = [], [], [], [[] for _ in GROUPS]
    for l in range(depth):
        mod = mods[l].reshape(batch, 1, 3 * D_MODEL)
        ya, q, k, v, g, beta, bz, qrot, kv0, kv1, kv2, cz, sa, sb = _inproj(
            rows, mod, zeros_a, zeros_b, tabs, wts, l, seg, seqs=batch, stride=1, tm=tm)
        yb, s_fin = _gdn_prompt(q, k, v, g, beta, bz, wts["gdn_norm_w"][l], batch)
        os_, lses = [], []
        for gi, kv in enumerate((kv0, kv1, kv2)):
            o, lse = _attn_prompt(qrot, kv, gi, batch)
            os_.append(o)
            lses.append(lse)
            win = min(GROUPS[gi][0], t)
            kvs[gi].append(kv.reshape(batch, t, 2, HPG, HEAD)[:, t - win:])
        final = (mod_final.reshape(batch, 1, 2 * D_MODEL), wts["final_norm_w"]) if l == depth - 1 else None
        rows = _outproj(ya, yb, os_, lses, cz, rows, mod, wts["w_out"], l, tm, final)
        st_a.append(sa)
        st_b.append(sb)
        st_g.append(s_fin)
    y = rows.reshape(batch, t, D_MODEL)
    return y, jnp.stack(st_a), jnp.stack(st_b), jnp.stack(st_g), [jnp.stack(r) for r in kvs]


def _sample_trunk(x, mods, mod_final, state_a, state_b, state_g, caches, past_len, wts, seg):
    bsz, steps, _ = x.shape
    depth = wts["w_in"].shape[0]
    m = bsz * steps
    tm = bsz

    def time_major(a):
        return a.transpose(1, 0, 2).reshape(a.shape[1] * bsz, a.shape[2])

    def batch_major(a, n):
        return a.reshape(n, bsz, a.shape[1]).transpose(1, 0, 2)

    def lanes_batch(a, n):
        return a.reshape(n, bsz, a.shape[1]).transpose(0, 2, 1)

    rows = time_major(x)
    tabs = _rope_tables(jnp.repeat(past_len + jnp.arange(steps), bsz))
    cache_views = [c.transpose(0, 1, 3, 4, 5, 2) for c in caches]
    state_t = state_g.transpose(0, 2, 3, 4, 1)
    st_a, st_b, st_g, kvs = [], [], [], [[] for _ in GROUPS]
    for l in range(depth):
        mod = mods[l][None]
        buf_a = time_major(state_a[l])[None]
        buf_b = time_major(state_b[l])[None]
        ya, q, k, v, g, beta, bz, qrot, kv0, kv1, kv2, cz, sa, sb = _inproj(
            rows, mod, buf_a, buf_b, tabs, wts, l, seg, seqs=1, stride=bsz, tm=m)
        nwb = jnp.broadcast_to(wts["gdn_norm_w"][l][:, None], (HEAD, bsz))
        yt, s_t = _gdn_sample(lanes_batch(q, steps), lanes_batch(k, steps), lanes_batch(v, steps),
                              lanes_batch(g[:, :SUBLANE], steps), lanes_batch(beta[:, :SUBLANE], steps),
                              lanes_batch(bz, steps), nwb, state_t, l)
        yb = yt.transpose(0, 2, 1).reshape(m, B_WIDTH)
        dup = lambda a: jnp.concatenate([batch_major(a, steps)] * 2, axis=1)
        o_c, lse_c = _attn_sample(dup(qrot), [dup(kv) for kv in (kv0, kv1, kv2)], cache_views, l, 4)
        o_c, lse_c = time_major(o_c), time_major(lse_c)
        os_ = [o_c[:, gi * GROUP_W:(gi + 1) * GROUP_W] for gi in range(len(GROUPS))]
        lses = [lse_c[:, gi * GROUP_W:(gi + 1) * GROUP_W] for gi in range(len(GROUPS))]
        final = (mod_final[None], wts["final_norm_w"]) if l == depth - 1 else None
        rows = _outproj(ya, yb, os_, lses, cz, rows, mod, wts["w_out"], l, tm, final)
        st_a.append(batch_major(sa[0], A_CONV - 1))
        st_b.append(batch_major(sb[0], B_CONV - 1))
        st_g.append(s_t.transpose(3, 0, 1, 2))
        for gi, kv in enumerate((kv0, kv1, kv2)):
            kvs[gi].append(batch_major(kv, steps).reshape(bsz, steps, 2, HPG, HEAD))
    y = batch_major(rows, steps)
    return y, jnp.stack(st_a), jnp.stack(st_b), jnp.stack(st_g), [jnp.stack(r) for r in kvs]


def kernel(x_prompt, x_sample, state_conv_a, state_conv_b, state_gdn, cache_kv_w128, cache_kv_w512,
           cache_kv_w2048, c_prompt, c_sample, w_in, w_out, w_ada, b_ada, norm_w, conv_a_w, conv_b_w,
           a_log, dt_bias, gdn_norm_w, final_norm_w, w_ada_final, b_ada_final):
    n_prompt = c_prompt.shape[0]
    c_all = jnp.concatenate([c_prompt, c_sample], axis=0)
    mods = _ada(c_all, w_ada, b_ada)
    mod_final = _ada(c_all, w_ada_final[None], b_ada_final[None])[0]
    wts = dict(w_in=_permute_w_in(w_in), w_out=w_out.astype(BF16), norm_w=norm_w[:, None, :], conv_a_w=conv_a_w,
               conv_b_w=conv_b_w, a_log=a_log, dt_bias=dt_bias, gdn_norm_w=gdn_norm_w,
               final_norm_w=final_norm_w)
    seg = _segment_ones()
    y_p, ca_p, cb_p, g_p, kv_p = _prompt_trunk(x_prompt, mods[:, :n_prompt], mod_final[:n_prompt], wts, seg)
    y_s, ca_s, cb_s, g_s, kv_s = _sample_trunk(
        x_sample, mods[:, n_prompt:], mod_final[n_prompt:], state_conv_a, state_conv_b, state_gdn,
        (cache_kv_w128, cache_kv_w512, cache_kv_w2048), PAST_LEN, wts, seg)
    return (y_p, y_s, ca_p, ca_s, cb_p, cb_s, g_p, g_s,
            kv_p[0], kv_s[0], kv_p[1], kv_s[1], kv_p[2], kv_s[2])
```
